```python
import jax, jax.numpy as jnp
from jax import lax
import numpy as np

D_MODEL = 1024
BATCH = 2
SEQ = 8192
DEPTH = 1

D_MIX = D_MODEL
D_CONV = 512
N_CONV_GROUPS = 8
CONV_WIDTH = 3
N_HEADS = 8
HEAD_DIM = 64
D_ATTN = N_HEADS * HEAD_DIM
D_IN = 3 * D_CONV + 3 * D_ATTN
MOBA_BLOCK = 256
MOBA_TOPK = 3
Q_CHUNK = 64
N_GROUPS = 4
EXPERTS_PER_GROUP = 8
N_EXPERTS = N_GROUPS * EXPERTS_PER_GROUP
TOP_K_INNER = 2
D_EXPERT = 512
DISPATCH_BLOCK = 256
PLE_DIM = 256
LN_EPS = 1e-5
DEEPNORM_ALPHA = (2 * DEPTH) ** 0.25
DEEPNORM_BETA = (8 * DEPTH) ** -0.25

kernel_name = "hymba_conv_moba_hmoe_deepnorm"


def layer_norm(x, g, b):
    xf = x.astype(jnp.float32)
    mu = xf.mean(-1, keepdims=True)
    var = jnp.mean(jnp.square(xf - mu), -1, keepdims=True)
    return ((xf - mu) * lax.rsqrt(var + LN_EPS)).astype(x.dtype) * g + b


def short_conv(gate_b, gate_c, h, w_conv):
    s = h.shape[1]
    u = jnp.pad(gate_c * h, ((0, 0), (CONV_WIDTH - 1, 0), (0, 0)))
    y = w_conv[0] * u[:, 0:s] + w_conv[1] * u[:, 1:s + 1] + w_conv[2] * u[:, 2:s + 2]
    return gate_b * y


def moba_attention(q, k, v):
    b, s, h, dh = q.shape
    nb = -(-s // MOBA_BLOCK)
    s_pad = nb * MOBA_BLOCK
    topk = min(MOBA_TOPK, nb)
    scale = dh ** -0.5
    q = q.transpose(0, 2, 1, 3)
    k = k.transpose(0, 2, 1, 3)
    v = v.transpose(0, 2, 1, 3)
    pad = ((0, 0), (0, 0), (0, s_pad - s), (0, 0))
    kb = jnp.pad(k, pad).reshape(b, h, nb, MOBA_BLOCK, dh)
    vb = jnp.pad(v, pad).reshape(b, h, nb, MOBA_BLOCK, dh)
    k_mean = kb.astype(jnp.float32).mean(axis=3)
    n_chunks = s // Q_CHUNK
    q_chunks = q.reshape(b, h, n_chunks, Q_CHUNK, dh).transpose(2, 0, 1, 3, 4)
    gather_blocks = jax.vmap(jax.vmap(lambda blocks, idx: blocks[idx]))

    def chunk_fn(args):
        qc, c = args
        q_pos = c * Q_CHUNK + jnp.arange(Q_CHUNK)
        q_blk = q_pos // MOBA_BLOCK
        own = (c * Q_CHUNK) // MOBA_BLOCK
        gate = jnp.einsum('bhqd,bhnd->bhqn', qc.astype(jnp.float32), k_mean)
        past = jnp.arange(nb)[None, :] < q_blk[:, None]
        gate = jnp.where(past, gate, -jnp.inf)
        _, idx = lax.top_k(gate, topk)
        sel_valid = idx < q_blk[:, None]
        k_sel = gather_blocks(kb, idx)
        v_sel = gather_blocks(vb, idx)
        s_sel = jnp.einsum('bhqd,bhqnkd->bhqnk', qc, k_sel).astype(jnp.float32) * scale
        s_sel = jnp.where(sel_valid[..., None], s_sel, -jnp.inf)
        k_own = lax.dynamic_index_in_dim(kb, own, axis=2, keepdims=False)
        v_own = lax.dynamic_index_in_dim(vb, own, axis=2, keepdims=False)
        k_pos = own * MOBA_BLOCK + jnp.arange(MOBA_BLOCK)
        s_own = jnp.einsum('bhqd,bhkd->bhqk', qc, k_own).astype(jnp.float32) * scale
        s_own = jnp.where(k_pos[None, :] <= q_pos[:, None], s_own, -jnp.inf)
        scores = jnp.concatenate([s_sel.reshape(b, h, Q_CHUNK, topk * MOBA_BLOCK), s_own], axis=-1)
        probs = jax.nn.softmax(scores, axis=-1).astype(v.dtype)
        p_sel = probs[..., :topk * MOBA_BLOCK].reshape(b, h, Q_CHUNK, topk, MOBA_BLOCK)
        p_own = probs[..., topk * MOBA_BLOCK:]
        return (jnp.einsum('bhqnk,bhqnkd->bhqd', p_sel, v_sel)
                + jnp.einsum('bhqk,bhkd->bhqd', p_own, v_own))

    out = lax.map(chunk_fn, (q_chunks, jnp.arange(n_chunks)))
    return out.transpose(1, 0, 3, 2, 4).reshape(b, s, h * dh)


def hier_moe(x, w_router_g, b_router_g, w_router_e, b_router_e, w_gate, w_up, w_down):
    bsz, s, d = x.shape
    xt = x.reshape(-1, d)
    n = xt.shape[0]
    g_prob = jax.nn.softmax((xt @ w_router_g).astype(jnp.float32) + b_router_g, axis=-1)
    g_p, g_idx = lax.top_k(g_prob, 1)
    e_logits = ((xt @ w_router_e).astype(jnp.float32) + b_router_e).reshape(n, N_GROUPS, EXPERTS_PER_GROUP)
    sel = jnp.broadcast_to(g_idx[:, :, None], (n, 1, EXPERTS_PER_GROUP))
    e_logits = jnp.take_along_axis(e_logits, sel, axis=1)[:, 0]
    e_p, e_local = lax.top_k(jax.nn.softmax(e_logits, axis=-1), TOP_K_INNER)
    e_p = e_p / e_p.sum(-1, keepdims=True)
    weights = g_p * e_p
    experts = g_idx * EXPERTS_PER_GROUP + e_local
    a = n * TOP_K_INNER
    r = DISPATCH_BLOCK
    flat_e = experts.reshape(-1)
    flat_w = weights.reshape(-1)
    flat_tok = jnp.repeat(jnp.arange(n), TOP_K_INNER)
    order = jnp.argsort(flat_e)
    se, stok, sw = flat_e[order], flat_tok[order], flat_w[order]
    counts = jnp.bincount(flat_e, length=N_EXPERTS)
    padded = (counts + r - 1) // r * r
    pad_end = jnp.cumsum(padded)
    pad_start = pad_end - padded
    start = jnp.cumsum(counts) - counts
    slot = pad_start[se] + jnp.arange(a) - start[se]
    n_blk = -(-a // r) + N_EXPERTS
    n_pad = n_blk * r
    buf = jnp.zeros((n_pad, d), x.dtype).at[slot].set(xt[stok])
    blk_expert = jnp.minimum(jnp.searchsorted(pad_end, jnp.arange(n_blk) * r, side='right'), N_EXPERTS - 1)

    def expert_block(args):
        xb, e = args
        hid = jax.nn.silu(xb @ w_gate[e]) * (xb @ w_up[e])
        return hid @ w_down[e]

    yb = lax.map(expert_block, (buf.reshape(n_blk, r, d), blk_expert)).reshape(n_pad, d)
    y_sorted = yb[slot] * sw[:, None].astype(x.dtype)
    y = jnp.zeros((n, d), x.dtype).at[stok].add(y_sorted)
    return y.reshape(bsz, s, d)


def setup_inputs(seed: int = 0) -> dict:
    key = jax.random.key(seed)
    ks = jax.random.split(key, 20)
    f32 = jnp.float32
    nrm = lambda k, shape, sc: jax.random.normal(k, shape, f32) * sc
    L = DEPTH
    return {
        "x": nrm(ks[0], (BATCH, SEQ, D_MODEL), 1.0),
        "p": nrm(ks[1], (DEPTH, BATCH, SEQ, PLE_DIM), 1.0),
        "w_in": nrm(ks[2], (L, D_MODEL, D_IN), D_MODEL ** -0.5),
        "w_conv": nrm(ks[3], (L, CONV_WIDTH, D_CONV), CONV_WIDTH ** -0.5),
        "w_out": nrm(ks[4], (L, D_MIX, D_MODEL), D_MIX ** -0.5 * DEEPNORM_BETA),
        "ln1_g": 1.0 + nrm(ks[5], (L, D_MODEL), 0.02),
        "ln1_b": nrm(ks[6], (L, D_MODEL), 0.02),
        "w_router_g": nrm(ks[7], (L, D_MODEL, N_GROUPS), D_MODEL ** -0.5),
        "b_router_g": nrm(ks[8], (L, N_GROUPS), 0.01),
        "w_router_e": nrm(ks[9], (L, D_MODEL, N_EXPERTS), D_MODEL ** -0.5),
        "b_router_e": nrm(ks[10], (L, N_EXPERTS), 0.01),
        "w_gate": nrm(ks[11], (L, N_EXPERTS, D_MODEL, D_EXPERT), D_MODEL ** -0.5),
        "w_up": nrm(ks[12], (L, N_EXPERTS, D_MODEL, D_EXPERT), D_MODEL ** -0.5),
        "w_down": nrm(ks[13], (L, N_EXPERTS, D_EXPERT, D_MODEL), D_EXPERT ** -0.5 * DEEPNORM_BETA),
        "w_ple_gate": nrm(ks[14], (L, D_MODEL, D_MODEL), D_MODEL ** -0.5),
        "w_ple_proj": nrm(ks[15], (L, PLE_DIM, D_MODEL), PLE_DIM ** -0.5 * DEEPNORM_BETA),
        "ln2_g": 1.0 + nrm(ks[16], (L, D_MODEL), 0.02),
        "ln2_b": nrm(ks[17], (L, D_MODEL), 0.02),
    }


def reference(x, p, w_in, w_conv, w_out, ln1_g, ln1_b, w_router_g, b_router_g, w_router_e,
              b_router_e, w_gate, w_up, w_down, w_ple_gate, w_ple_proj, ln2_g, ln2_b):
    b, s, _ = x.shape
    splits = [D_CONV, 2 * D_CONV, 3 * D_CONV, 3 * D_CONV + D_ATTN, 3 * D_CONV + 2 * D_ATTN]
    for i in range(DEPTH):
        proj = x @ w_in[i]
        cb, cc, ch, q, k, v = jnp.split(proj, splits, axis=-1)
        y_conv = short_conv(cb, cc, ch, w_conv[i])
        hd = (b, s, N_HEADS, HEAD_DIM)
        y_attn = moba_attention(q.reshape(hd), k.reshape(hd), v.reshape(hd))
        mix = jnp.concatenate([y_conv, y_attn], axis=-1) @ w_out[i]
        x = layer_norm(DEEPNORM_ALPHA * x + mix, ln1_g[i], ln1_b[i])
        ffn = hier_moe(x, w_router_g[i], b_router_g[i], w_router_e[i], b_router_e[i],
                       w_gate[i], w_up[i], w_down[i])
        ple = jax.nn.sigmoid(x @ w_ple_gate[i]) * (p[i] @ w_ple_proj[i])
        x = layer_norm(DEEPNORM_ALPHA * x + ffn + ple, ln2_g[i], ln2_b[i])
    return x
```

```python
import functools

import jax
import jax.numpy as jnp
from jax import lax
from jax.experimental import pallas as pl
from jax.experimental.pallas import tpu as pltpu

D_MODEL = 1024
D_CONV = 512
N_HEADS = 8
HEAD_DIM = 64
D_ATTN = N_HEADS * HEAD_DIM
D_IN = 3 * D_CONV + 3 * D_ATTN
MOBA_BLOCK = 256
MOBA_TOPK = 3
N_GROUPS = 4
EXPERTS_PER_GROUP = 8
N_EXPERTS = N_GROUPS * EXPERTS_PER_GROUP
D_EXPERT = 512
DISPATCH_BLOCK = 256
PLE_DIM = 256
LN_EPS = 1e-5
DEPTH = 1
DEEPNORM_ALPHA = (2 * DEPTH) ** 0.25

LANES = 128
PROJ_ROWS = 512
TOKEN_TILE = 256
VMEM_LIMIT = 48 * 1024 * 1024

F32 = jnp.float32
BF16 = jnp.bfloat16
NEG_INF = float("-inf")


def _dot(a, b):
    return jnp.dot(a, b, preferred_element_type=F32)


def _dot_nt(a, b):
    return lax.dot_general(a, b, (((1,), (1,)), ((), ())), preferred_element_type=F32)


def _layer_norm(h, g, b):
    mu = jnp.mean(h, axis=-1, keepdims=True)
    d = h - mu
    var = jnp.mean(d * d, axis=-1, keepdims=True)
    return d * lax.rsqrt(var + LN_EPS) * g + b


def _proj_conv_kernel(x_ref, w_ref, wc_ref, yc_ref, q_ref, k_ref, vt_ref, km_ref, ubuf):
    s = pl.program_id(1)
    rows = x_ref.shape[0]
    xb = x_ref[...].astype(BF16)

    def sect(i):
        return _dot(xb, w_ref[:, i * D_CONV:(i + 1) * D_CONV])

    @pl.when(s == 0)
    def _():
        ubuf[0:8, :] = jnp.zeros((8, D_CONV), F32)

    u = sect(1) * sect(2)
    ubuf[8:8 + rows, :] = u
    wc = wc_ref[...]
    conv = wc[0:1, :] * ubuf[6:6 + rows, :] + wc[1:2, :] * ubuf[7:7 + rows, :] + wc[2:3, :] * u
    yc_ref[...] = (sect(0) * conv).astype(BF16)
    ubuf[0:8, :] = ubuf[rows:rows + 8, :]

    q_ref[...] = (sect(3) * (HEAD_DIM ** -0.5)).astype(BF16)
    k = sect(4)
    k_ref[...] = k.astype(BF16)
    for i in range(rows // MOBA_BLOCK):
        km_ref[i] = jnp.mean(k[i * MOBA_BLOCK:(i + 1) * MOBA_BLOCK, :], axis=0, keepdims=True)
    vt_ref[...] = sect(5).T.astype(BF16)


def _proj_conv(x, w_in_bf, w_conv):
    b, s, _ = x.shape
    nb = s // MOBA_BLOCK
    grid = (b, s // PROJ_ROWS)
    seq_spec = pl.BlockSpec((None, PROJ_ROWS, D_CONV), lambda i, j: (i, j, 0))
    return pl.pallas_call(
        _proj_conv_kernel,
        grid=grid,
        in_specs=[
            pl.BlockSpec((None, PROJ_ROWS, D_MODEL), lambda i, j: (i, j, 0)),
            pl.BlockSpec((D_MODEL, D_IN), lambda i, j: (0, 0)),
            pl.BlockSpec((3, D_CONV), lambda i, j: (0, 0)),
        ],
        out_specs=[
            seq_spec, seq_spec, seq_spec,
            pl.BlockSpec((None, D_ATTN, PROJ_ROWS), lambda i, j: (i, 0, j)),
            pl.BlockSpec((None, PROJ_ROWS // MOBA_BLOCK, 1, D_ATTN), lambda i, j: (i, j, 0, 0)),
        ],
        out_shape=[
            jax.ShapeDtypeStruct((b, s, D_CONV), BF16),
            jax.ShapeDtypeStruct((b, s, D_ATTN), BF16),
            jax.ShapeDtypeStruct((b, s, D_ATTN), BF16),
            jax.ShapeDtypeStruct((b, D_ATTN, s), BF16),
            jax.ShapeDtypeStruct((b, nb, 1, D_ATTN), F32),
        ],
        scratch_shapes=[pltpu.VMEM((PROJ_ROWS + 8, D_CONV), F32)],
        compiler_params=pltpu.CompilerParams(
            dimension_semantics=("arbitrary", "arbitrary"), vmem_limit_bytes=VMEM_LIMIT),
        name="proj_conv",
    )(x, w_in_bf, w_conv)


def _moba_kernel(q_ref, k_ref, vt_ref, km_ref, o_ref, bias_ref):
    j = pl.program_id(2)
    blk = MOBA_BLOCK
    nb = km_ref.shape[0]

    q2 = q_ref[...].astype(F32)
    lane = lax.broadcasted_iota(jnp.int32, q2.shape, 1)
    q_heads = (jnp.where(lane < HEAD_DIM, q2, 0.0).astype(BF16),
               jnp.where(lane >= HEAD_DIM, q2, 0.0).astype(BF16))

    km = km_ref[...]
    km_hi = km.astype(BF16)
    km_lo = (km - km_hi.astype(F32)).astype(BF16)
    bidx = lax.broadcasted_iota(jnp.int32, (nb, blk), 0)
    for h in range(2):
        g = _dot_nt(km_hi, q_heads[h]) + _dot_nt(km_lo, q_heads[h])
        g = jnp.where(bidx < j, g, NEG_INF)
        sel = jnp.zeros((nb, blk), jnp.bool_)
        for _ in range(MOBA_TOPK):
            mx = jnp.max(g, axis=0, keepdims=True)
            cand = jnp.where((g == mx) & (mx > NEG_INF), bidx, nb)
            pick = bidx == jnp.min(cand, axis=0, keepdims=True)
            sel = sel | pick
            g = jnp.where(pick, NEG_INF, g)
        bias_ref[h] = jnp.where(sel, 0.0, NEG_INF)

    j0 = pl.multiple_of(j * blk, blk)
    kj = k_ref[pl.ds(j0, blk), :]
    vtj = vt_ref[:, pl.ds(j0, blk)]
    kpos = lax.broadcasted_iota(jnp.int32, (blk, blk), 0)
    qpos = lax.broadcasted_iota(jnp.int32, (blk, blk), 1)
    causal = kpos <= qpos
    state = []
    for h in range(2):
        sc = jnp.where(causal, _dot_nt(kj, q_heads[h]), NEG_INF)
        m = jnp.max(sc, axis=0, keepdims=True)
        p = jnp.exp(sc - m)
        l = jnp.sum(p, axis=0, keepdims=True)
        acc = _dot(vtj[h * HEAD_DIM:(h + 1) * HEAD_DIM, :], p.astype(BF16))
        state += [m, l, acc]

    def body(n, carry):
        n0 = pl.multiple_of(n * blk, blk)
        kn = k_ref[pl.ds(n0, blk), :]
        vtn = vt_ref[:, pl.ds(n0, blk)]
        out = []
        for h in range(2):
            m, l, acc = carry[3 * h:3 * h + 3]
            sc = _dot_nt(kn, q_heads[h]) + bias_ref[h, pl.ds(n, 1), :]
            m_new = jnp.maximum(m, jnp.max(sc, axis=0, keepdims=True))
            alpha = jnp.exp(m - m_new)
            p = jnp.exp(sc - m_new)
            l = alpha * l + jnp.sum(p, axis=0, keepdims=True)
            acc = alpha * acc + _dot(vtn[h * HEAD_DIM:(h + 1) * HEAD_DIM, :], p.astype(BF16))
            out += [m_new, l, acc]
        return tuple(out)

    state = lax.fori_loop(0, j, body, tuple(state))
    ot = jnp.concatenate([state[2] / state[1], state[5] / state[4]], axis=0)
    o_ref[...] = ot.T.astype(BF16)


def _moba(q, k, vt, kmean):
    b, s, _ = q.shape
    nb = s // MOBA_BLOCK
    n_pairs = D_ATTN // LANES
    return pl.pallas_call(
        _moba_kernel,
        grid=(b, n_pairs, nb),
        in_specs=[
            pl.BlockSpec((None, MOBA_BLOCK, LANES), lambda i, p, j: (i, j, p)),
            pl.BlockSpec((None, s, LANES), lambda i, p, j: (i, 0, p)),
            pl.BlockSpec((None, LANES, s), lambda i, p, j: (i, p, 0)),
            pl.BlockSpec((None, nb, LANES), lambda i, p, j: (i, 0, p)),
        ],
        out_specs=pl.BlockSpec((None, MOBA_BLOCK, LANES), lambda i, p, j: (i, j, p)),
        out_shape=jax.ShapeDtypeStruct((b, s, D_ATTN), BF16),
        scratch_shapes=[pltpu.VMEM((2, nb, MOBA_BLOCK), F32)],
        compiler_params=pltpu.CompilerParams(
            dimension_semantics=("arbitrary", "arbitrary", "arbitrary"),
            vmem_limit_bytes=VMEM_LIMIT),
        name="moba",
    )(q, k, vt, kmean)


GROUP_LANE0 = N_EXPERTS


def _mix_route_kernel(yc_ref, ya_ref, x_ref, wo_ref, g_ref, b_ref, wr_ref, br_ref,
                      x1_ref, ri_ref, rw_ref, cnt_ref, base_ref):
    i = pl.program_id(0)
    t = x_ref.shape[0]

    @pl.when(i == 0)
    def _():
        base_ref[...] = jnp.zeros_like(base_ref)

    mix = _dot(yc_ref[...], wo_ref[0:D_CONV, :]) + _dot(ya_ref[...], wo_ref[D_CONV:, :])
    x1 = _layer_norm(DEEPNORM_ALPHA * x_ref[...] + mix, g_ref[...], b_ref[...])
    x1_ref[...] = x1

    x_hi = x1.astype(BF16)
    x_lo = (x1 - x_hi.astype(F32)).astype(BF16)
    wr = wr_ref[...]
    w_hi = wr.astype(BF16)
    w_lo = (wr - w_hi.astype(F32)).astype(BF16)
    logits = _dot(x_hi, w_hi) + _dot(x_lo, w_hi) + _dot(x_hi, w_lo) + br_ref[...]

    lane = lax.broadcasted_iota(jnp.int32, (t, LANES), 1).astype(F32)
    big = float(LANES)

    lg = jnp.where((lane >= GROUP_LANE0) & (lane < GROUP_LANE0 + N_GROUPS), logits, NEG_INF)
    gmax = jnp.max(lg, axis=-1, keepdims=True)
    g_p = 1.0 / jnp.sum(jnp.exp(lg - gmax), axis=-1, keepdims=True)
    g_idx = jnp.min(jnp.where(lg == gmax, lane, big), axis=-1, keepdims=True) - GROUP_LANE0

    lo = g_idx * EXPERTS_PER_GROUP
    le = jnp.where((lane >= lo) & (lane < lo + EXPERTS_PER_GROUP), logits, NEG_INF)
    m1 = jnp.max(le, axis=-1, keepdims=True)
    i1 = jnp.min(jnp.where(le == m1, lane, big), axis=-1, keepdims=True)
    le2 = jnp.where(lane == i1, NEG_INF, le)
    m2 = jnp.max(le2, axis=-1, keepdims=True)
    i2 = jnp.min(jnp.where(le2 == m2, lane, big), axis=-1, keepdims=True)
    p2 = jnp.exp(m2 - m1)
    den = 1.0 + p2
    w1 = g_p * (1.0 / den)
    w2 = g_p * (p2 / den)

    oh = (lane == i1) | (lane == i2)
    ohf = oh.astype(F32)
    r_i = lax.broadcasted_iota(jnp.int32, (t, t), 0)
    c_i = lax.broadcasted_iota(jnp.int32, (t, t), 1)
    tri = (r_i > c_i).astype(BF16)
    tot = base_ref[...] + _dot(tri, ohf.astype(BF16))
    r1 = jnp.sum(jnp.where(lane == i1, tot, 0.0), axis=-1, keepdims=True)
    r2 = jnp.sum(jnp.where(lane == i2, tot, 0.0), axis=-1, keepdims=True)
    base = base_ref[...] + jnp.sum(ohf, axis=0, keepdims=True)
    base_ref[...] = base
    cnt_ref[...] = jnp.broadcast_to(base, cnt_ref.shape)

    ri = jnp.where(lane == 0, i1, jnp.where(lane == 1, i2, jnp.where(lane == 2, r1, r2)))
    ri_ref[...] = ri.astype(jnp.int32)
    rw_ref[...] = jnp.where(lane == 0, w1, w2)


def _mix_route(yc, ya, x2d, w_out_bf, g, b, w_r, b_r):
    n = x2d.shape[0]
    t = TOKEN_TILE
    row = lambda w: pl.BlockSpec((t, w), lambda i: (i, 0))
    full = lambda a: pl.BlockSpec(a.shape, lambda i: (0, 0))
    return pl.pallas_call(
        _mix_route_kernel,
        grid=(n // t,),
        in_specs=[row(D_CONV), row(D_ATTN), row(D_MODEL), full(w_out_bf), full(g), full(b),
                  full(w_r), full(b_r)],
        out_specs=[row(D_MODEL), row(LANES), row(LANES), pl.BlockSpec((8, LANES), lambda i: (0, 0))],
        out_shape=[
            jax.ShapeDtypeStruct((n, D_MODEL), F32),
            jax.ShapeDtypeStruct((n, LANES), jnp.int32),
            jax.ShapeDtypeStruct((n, LANES), F32),
            jax.ShapeDtypeStruct((8, LANES), F32),
        ],
        scratch_shapes=[pltpu.VMEM((1, LANES), F32)],
        compiler_params=pltpu.CompilerParams(
            dimension_semantics=("arbitrary",), vmem_limit_bytes=VMEM_LIMIT),
        name="mix_route",
    )(yc, ya, x2d, w_out_bf, g, b, w_r, b_r)


def _row_gather_start(idx_ref, n_rows, src_hbm, dst, sem):
    def body(r, c):
        pltpu.make_async_copy(src_hbm.at[pl.ds(idx_ref[0, r], 1)], dst.at[pl.ds(r, 1)], sem).start()
        return c
    lax.fori_loop(0, n_rows, body, 0, unroll=8)


def _row_gather_wait(n_rows, src_hbm, dst, sem):
    pltpu.make_async_copy(src_hbm.at[pl.ds(0, n_rows)], dst, sem).wait()


def _experts_kernel(be_ref, nu_ref, cur_ref, nxt_ref, x1_hbm, wg_ref, wu_ref, wd_ref,
                    yb_ref, xbuf, sem):
    i = pl.program_id(0)
    n_used = nu_ref[0]
    r = DISPATCH_BLOCK
    slot = i % 2

    @pl.when((i == 0) & (n_used > 0))
    def _():
        _row_gather_start(cur_ref, r, x1_hbm, xbuf.at[0], sem.at[0])

    @pl.when(i + 1 < n_used)
    def _():
        _row_gather_start(nxt_ref, r, x1_hbm, xbuf.at[1 - slot], sem.at[1 - slot])

    @pl.when(i < n_used)
    def _():
        _row_gather_wait(r, x1_hbm, xbuf.at[slot], sem.at[slot])
        xb = xbuf[slot].astype(BF16)
        gate = _dot(xb, wg_ref[...].astype(BF16))
        up = _dot(xb, wu_ref[...].astype(BF16))
        hid = (jax.nn.silu(gate) * up).astype(BF16)
        yb_ref[...] = _dot(hid, wd_ref[...].astype(BF16))

    @pl.when(i >= n_used)
    def _():
        yb_ref[...] = jnp.zeros_like(yb_ref)


def _experts(blk_expert, n_used, tos, x1, w_gate, w_up, w_down):
    n_blk = blk_expert.shape[0]
    r = DISPATCH_BLOCK
    tos3 = tos.reshape(n_blk, 1, r)
    smem_blk = lambda f: pl.BlockSpec((None, 1, r), f, memory_space=pltpu.SMEM)
    grid_spec = pltpu.PrefetchScalarGridSpec(
        num_scalar_prefetch=2,
        grid=(n_blk,),
        in_specs=[
            smem_blk(lambda i, be, nu: (i, 0, 0)),
            smem_blk(lambda i, be, nu: (jnp.minimum(i + 1, n_blk - 1), 0, 0)),
            pl.BlockSpec(memory_space=pl.ANY),
            pl.BlockSpec((None, D_MODEL, D_EXPERT), lambda i, be, nu: (be[i], 0, 0)),
            pl.BlockSpec((None, D_MODEL, D_EXPERT), lambda i, be, nu: (be[i], 0, 0)),
            pl.BlockSpec((None, D_EXPERT, D_MODEL), lambda i, be, nu: (be[i], 0, 0)),
        ],
        out_specs=pl.BlockSpec((r, D_MODEL), lambda i, be, nu: (i, 0)),
        scratch_shapes=[pltpu.VMEM((2, r, D_MODEL), F32), pltpu.SemaphoreType.DMA((2,))],
    )
    return pl.pallas_call(
        _experts_kernel,
        grid_spec=grid_spec,
        out_shape=jax.ShapeDtypeStruct((n_blk * r, D_MODEL), F32),
        compiler_params=pltpu.CompilerParams(
            dimension_semantics=("arbitrary",), vmem_limit_bytes=VMEM_LIMIT),
        name="experts",
    )(blk_expert, n_used, tos3, tos3, x1, w_gate, w_up, w_down)


def _combine_kernel(cur_ref, nxt_ref, x1_ref, p_ref, rw_ref, wpg_ref, wpp_ref, g_ref, b_ref,
                    yb_hbm, o_ref, gbuf, sem):
    i = pl.program_id(0)
    n_steps = pl.num_programs(0)
    t = x1_ref.shape[0]
    slot = i % 2

    @pl.when(i == 0)
    def _():
        _row_gather_start(cur_ref, 2 * t, yb_hbm, gbuf.at[0], sem.at[0])

    @pl.when(i + 1 < n_steps)
    def _():
        _row_gather_start(nxt_ref, 2 * t, yb_hbm, gbuf.at[1 - slot], sem.at[1 - slot])

    x1 = x1_ref[...]
    ple = jax.nn.sigmoid(_dot(x1.astype(BF16), wpg_ref[...])) * _dot(p_ref[...].astype(BF16), wpp_ref[...])

    _row_gather_wait(2 * t, yb_hbm, gbuf.at[slot], sem.at[slot])
    rw = rw_ref[...]
    ffn = gbuf[slot, 0:t, :] * rw[:, 0:1] + gbuf[slot, t:2 * t, :] * rw[:, 1:2]
    o_ref[...] = _layer_norm(DEEPNORM_ALPHA * x1 + ffn + ple, g_ref[...], b_ref[...])


def _combine(slots, x1, p2d, rw, w_pg_bf, w_pp_bf, g, b, yb):
    n = x1.shape[0]
    t = TOKEN_TILE
    n_steps = n // t
    slots3 = slots.reshape(n_steps, 1, 2 * t)
    smem_blk = lambda f: pl.BlockSpec((None, 1, 2 * t), f, memory_space=pltpu.SMEM)
    row = lambda w: pl.BlockSpec((t, w), lambda i: (i, 0))
    full = lambda a: pl.BlockSpec(a.shape, lambda i: (0, 0))
    return pl.pallas_call(
        _combine_kernel,
        grid=(n_steps,),
        in_specs=[
            smem_blk(lambda i: (i, 0, 0)),
            smem_blk(lambda i: (jnp.minimum(i + 1, n_steps - 1), 0, 0)),
            row(D_MODEL), row(PLE_DIM), row(LANES), full(w_pg_bf), full(w_pp_bf), full(g), full(b),
            pl.BlockSpec(memory_space=pl.ANY),
        ],
        out_specs=row(D_MODEL),
        out_shape=jax.ShapeDtypeStruct((n, D_MODEL), F32),
        scratch_shapes=[pltpu.VMEM((2, 2 * t, D_MODEL), F32), pltpu.SemaphoreType.DMA((2,))],
        compiler_params=pltpu.CompilerParams(
            dimension_semantics=("arbitrary",), vmem_limit_bytes=VMEM_LIMIT),
        name="combine",
    )(slots3, slots3, x1, p2d, rw, w_pg_bf, w_pp_bf, g, b, yb)


def _layer(x, p, w_in, w_conv, w_out, ln1_g, ln1_b, w_router_g, b_router_g, w_router_e,
           b_router_e, w_gate, w_up, w_down, w_ple_gate, w_ple_proj, ln2_g, ln2_b):
    b, s, d = x.shape
    n = b * s
    r = DISPATCH_BLOCK

    yc, q, k, vt, kmean = _proj_conv(x, w_in.astype(BF16), w_conv)
    ya = _moba(q, k, vt, kmean.reshape(b, s // MOBA_BLOCK, D_ATTN))

    w_r = jnp.zeros((d, LANES), F32).at[:, :N_EXPERTS].set(w_router_e)
    w_r = w_r.at[:, GROUP_LANE0:GROUP_LANE0 + N_GROUPS].set(w_router_g)
    b_r = jnp.zeros((1, LANES), F32).at[0, :N_EXPERTS].set(b_router_e)
    b_r = b_r.at[0, GROUP_LANE0:GROUP_LANE0 + N_GROUPS].set(b_router_g)
    x1, ri, rw, cnt = _mix_route(
        yc.reshape(n, D_CONV), ya.reshape(n, D_ATTN), x.reshape(n, d), w_out.astype(BF16),
        ln1_g.reshape(1, d), ln1_b.reshape(1, d), w_r, b_r)

    counts = cnt[0, :N_EXPERTS].astype(jnp.int32)
    padded = (counts + r - 1) // r * r
    pad_end = jnp.cumsum(padded)
    pad_start = pad_end - padded
    n_blk = -(-(2 * n) // r) + N_EXPERTS
    slot1 = pad_start[ri[:, 0]] + ri[:, 2]
    slot2 = pad_start[ri[:, 1]] + ri[:, 3]
    tok = jnp.arange(n, dtype=jnp.int32)
    tos = jnp.zeros((n_blk * r,), jnp.int32).at[jnp.concatenate([slot1, slot2])].set(
        jnp.concatenate([tok, tok]), unique_indices=True)
    blk_row0 = jnp.arange(n_blk, dtype=jnp.int32) * r
    blk_expert = jnp.minimum(
        jnp.sum((pad_end[None, :] <= blk_row0[:, None]).astype(jnp.int32), axis=1), N_EXPERTS - 1)
    n_used = (pad_end[-1:] // r).astype(jnp.int32)

    yb = _experts(blk_expert, n_used, tos, x1, w_gate, w_up, w_down)

    nt = n // TOKEN_TILE
    slots = jnp.concatenate(
        [slot1.reshape(nt, TOKEN_TILE), slot2.reshape(nt, TOKEN_TILE)], axis=1)
    out = _combine(slots, x1, p.reshape(n, PLE_DIM), rw, w_ple_gate.astype(BF16),
                   w_ple_proj.astype(BF16), ln2_g.reshape(1, d), ln2_b.reshape(1, d), yb)
    return out.reshape(b, s, d)


def kernel(x, p, w_in, w_conv, w_out, ln1_g, ln1_b, w_router_g, b_router_g, w_router_e,
           b_router_e, w_gate, w_up, w_down, w_ple_gate, w_ple_proj, ln2_g, ln2_b):
    for i in range(DEPTH):
        x = _layer(x, p[i], w_in[i], w_conv[i], w_out[i], ln1_g[i], ln1_b[i], w_router_g[i],
                   b_router_g[i], w_router_e[i], b_router_e[i], w_gate[i], w_up[i], w_down[i],
                   w_ple_gate[i], w_ple_proj[i], ln2_g[i], ln2_b[i])
    return x
```

```python
import functools

import jax
import jax.numpy as jnp
from jax import lax
from jax.experimental import pallas as pl
from jax.experimental.pallas import tpu as pltpu

D_MODEL = 1024
D_CONV = 512
N_HEADS = 8
HEAD_DIM = 64
D_ATTN = N_HEADS * HEAD_DIM
D_IN = 3 * D_CONV + 3 * D_ATTN
MOBA_BLOCK = 256
MOBA_TOPK = 3
N_GROUPS = 4
EXPERTS_PER_GROUP = 8
N_EXPERTS = N_GROUPS * EXPERTS_PER_GROUP
D_EXPERT = 512
DISPATCH_BLOCK = 256
PLE_DIM = 256
LN_EPS = 1e-5
DEPTH = 1
DEEPNORM_ALPHA = (2 * DEPTH) ** 0.25

LANES = 128
PROJ_ROWS = 512
TOKEN_TILE = 256
VMEM_LIMIT = 48 * 1024 * 1024

F32 = jnp.float32
BF16 = jnp.bfloat16
NEG_INF = float("-inf")


def _dot(a, b):
    return jnp.dot(a, b, preferred_element_type=F32)


def _dot_nt(a, b):
    return lax.dot_general(a, b, (((1,), (1,)), ((), ())), preferred_element_type=F32)


def _layer_norm(h, g, b):
    mu = jnp.mean(h, axis=-1, keepdims=True)
    d = h - mu
    var = jnp.mean(d * d, axis=-1, keepdims=True)
    return d * lax.rsqrt(var + LN_EPS) * g + b


def _proj_conv_kernel(x_ref, w_ref, wc_ref, yc_ref, q_ref, k_ref, vt_ref, km_ref, ubuf):
    s = pl.program_id(1)
    rows = x_ref.shape[0]
    xb = x_ref[...].astype(BF16)

    def sect(i):
        return _dot(xb, w_ref[:, i * D_CONV:(i + 1) * D_CONV])

    @pl.when(s == 0)
    def _():
        ubuf[0:8, :] = jnp.zeros((8, D_CONV), F32)

    u = sect(1) * sect(2)
    ubuf[8:8 + rows, :] = u
    wc = wc_ref[...]
    conv = wc[0:1, :] * ubuf[6:6 + rows, :] + wc[1:2, :] * ubuf[7:7 + rows, :] + wc[2:3, :] * u
    yc_ref[...] = (sect(0) * conv).astype(BF16)
    ubuf[0:8, :] = ubuf[rows:rows + 8, :]

    q_ref[...] = (sect(3) * (HEAD_DIM ** -0.5)).astype(BF16)
    k = sect(4)
    k_ref[...] = k.astype(BF16)
    for i in range(rows // MOBA_BLOCK):
        km_ref[i] = jnp.mean(k[i * MOBA_BLOCK:(i + 1) * MOBA_BLOCK, :], axis=0, keepdims=True)
    vt_ref[...] = sect(5).T.astype(BF16)


def _proj_conv(x, w_in_bf, w_conv):
    b, s, _ = x.shape
    nb = s // MOBA_BLOCK
    grid = (b, s // PROJ_ROWS)
    seq_spec = pl.BlockSpec((None, PROJ_ROWS, D_CONV), lambda i, j: (i, j, 0))
    return pl.pallas_call(
        _proj_conv_kernel,
        grid=grid,
        in_specs=[
            pl.BlockSpec((None, PROJ_ROWS, D_MODEL), lambda i, j: (i, j, 0)),
            pl.BlockSpec((D_MODEL, D_IN), lambda i, j: (0, 0)),
            pl.BlockSpec((3, D_CONV), lambda i, j: (0, 0)),
        ],
        out_specs=[
            seq_spec, seq_spec, seq_spec,
            pl.BlockSpec((None, D_ATTN, PROJ_ROWS), lambda i, j: (i, 0, j)),
            pl.BlockSpec((None, PROJ_ROWS // MOBA_BLOCK, 1, D_ATTN), lambda i, j: (i, j, 0, 0)),
        ],
        out_shape=[
            jax.ShapeDtypeStruct((b, s, D_CONV), BF16),
            jax.ShapeDtypeStruct((b, s, D_ATTN), BF16),
            jax.ShapeDtypeStruct((b, s, D_ATTN), BF16),
            jax.ShapeDtypeStruct((b, D_ATTN, s), BF16),
            jax.ShapeDtypeStruct((b, nb, 1, D_ATTN), F32),
        ],
        scratch_shapes=[pltpu.VMEM((PROJ_ROWS + 8, D_CONV), F32)],
        compiler_params=pltpu.CompilerParams(
            dimension_semantics=("arbitrary", "arbitrary"), vmem_limit_bytes=VMEM_LIMIT),
        name="proj_conv",
    )(x, w_in_bf, w_conv)


def _moba_kernel(q_ref, k_ref, vt_ref, km_ref, o_ref, bias_ref, s_ref, mt_ref, m_ref, l_ref,
                 acc_ref):
    j = pl.program_id(2)
    blk = MOBA_BLOCK
    nb = km_ref.shape[0]

    q2 = q_ref[...].astype(F32)
    lane = lax.broadcasted_iota(jnp.int32, q2.shape, 1)
    q_heads = (jnp.where(lane < HEAD_DIM, q2, 0.0).astype(BF16),
               jnp.where(lane >= HEAD_DIM, q2, 0.0).astype(BF16))

    km = km_ref[...]
    km_hi = km.astype(BF16)
    km_lo = (km - km_hi.astype(F32)).astype(BF16)
    bidx = lax.broadcasted_iota(jnp.int32, (nb, blk), 0)
    for h in range(2):
        g = _dot_nt(km_hi, q_heads[h]) + _dot_nt(km_lo, q_heads[h])
        g = jnp.where(bidx < j, g, NEG_INF)
        sel = jnp.zeros((nb, blk), jnp.bool_)
        for _ in range(MOBA_TOPK):
            mx = jnp.max(g, axis=0, keepdims=True)
            cand = jnp.where((g == mx) & (mx > NEG_INF), bidx, nb)
            pick = bidx == jnp.min(cand, axis=0, keepdims=True)
            sel = sel | pick
            g = jnp.where(pick, NEG_INF, g)
        bias_ref[h] = jnp.where(sel, 0.0, NEG_INF)

    j0 = pl.multiple_of(j * blk, blk)
    kj = k_ref[pl.ds(j0, blk), :]
    kpos = lax.broadcasted_iota(jnp.int32, (blk, blk), 0)
    qpos = lax.broadcasted_iota(jnp.int32, (blk, blk), 1)
    causal = kpos <= qpos
    for h in range(2):
        sc = jnp.where(causal, _dot_nt(kj, q_heads[h]), NEG_INF)
        s_ref[h] = sc
        mt_ref[h] = jnp.max(sc, axis=0, keepdims=True)
        m_ref[h] = jnp.full((1, blk), NEG_INF, F32)
        l_ref[h] = jnp.zeros((1, blk), F32)
        acc_ref[h] = jnp.zeros((HEAD_DIM, blk), F32)

    def body(n, c):
        ta = jnp.maximum(jnp.minimum(n, j - 1), 0)
        tb = jnp.where(n == 0, j, n - 1)
        kn = k_ref[pl.ds(pl.multiple_of(ta * blk, blk), blk), :]
        vtn = vt_ref[:, pl.ds(pl.multiple_of(tb * blk, blk), blk)]
        s_next = [_dot_nt(kn, q_heads[h]) for h in range(2)]
        for h in range(2):
            m_old = m_ref[h]
            m_new = jnp.maximum(m_old, mt_ref[h])
            alpha = jnp.exp(m_old - m_new)
            p = jnp.exp(s_ref[h] - m_new)
            l_ref[h] = alpha * l_ref[h] + jnp.sum(p, axis=0, keepdims=True)
            acc_ref[h] = alpha * acc_ref[h] + _dot(vtn[h * HEAD_DIM:(h + 1) * HEAD_DIM, :], p.astype(BF16))
            m_ref[h] = m_new
        for h in range(2):
            sc = s_next[h] + bias_ref[h, pl.ds(ta, 1), :]
            s_ref[h] = sc
            mt_ref[h] = jnp.max(sc, axis=0, keepdims=True)
        return c

    lax.fori_loop(0, j + 1, body, 0)
    ot = jnp.concatenate([acc_ref[0] / l_ref[0], acc_ref[1] / l_ref[1]], axis=0)
    o_ref[...] = ot.T.astype(BF16)


def _moba(q, k, vt, kmean):
    b, s, _ = q.shape
    nb = s // MOBA_BLOCK
    n_pairs = D_ATTN // LANES
    return pl.pallas_call(
        _moba_kernel,
        grid=(b, n_pairs, nb),
        in_specs=[
            pl.BlockSpec((None, MOBA_BLOCK, LANES), lambda i, p, j: (i, j, p)),
            pl.BlockSpec((None, s, LANES), lambda i, p, j: (i, 0, p)),
            pl.BlockSpec((None, LANES, s), lambda i, p, j: (i, p, 0)),
            pl.BlockSpec((None, nb, LANES), lambda i, p, j: (i, 0, p)),
        ],
        out_specs=pl.BlockSpec((None, MOBA_BLOCK, LANES), lambda i, p, j: (i, j, p)),
        out_shape=jax.ShapeDtypeStruct((b, s, D_ATTN), BF16),
        scratch_shapes=[
            pltpu.VMEM((2, nb, MOBA_BLOCK), F32),
            pltpu.VMEM((2, MOBA_BLOCK, MOBA_BLOCK), F32),
            pltpu.VMEM((2, 1, MOBA_BLOCK), F32),
            pltpu.VMEM((2, 1, MOBA_BLOCK), F32),
            pltpu.VMEM((2, 1, MOBA_BLOCK), F32),
            pltpu.VMEM((2, HEAD_DIM, MOBA_BLOCK), F32),
        ],
        compiler_params=pltpu.CompilerParams(
            dimension_semantics=("arbitrary", "arbitrary", "arbitrary"),
            vmem_limit_bytes=VMEM_LIMIT),
        name="moba",
    )(q, k, vt, kmean)


GROUP_LANE0 = N_EXPERTS


def _mix_route_kernel(yc_ref, ya_ref, x_ref, wo_ref, g_ref, b_ref, wr_ref, br_ref,
                      x1_ref, ri_ref, rw_ref, cnt_ref, base_ref):
    i = pl.program_id(0)
    t = x_ref.shape[0]

    @pl.when(i == 0)
    def _():
        base_ref[...] = jnp.zeros_like(base_ref)

    mix = _dot(yc_ref[...], wo_ref[0:D_CONV, :]) + _dot(ya_ref[...], wo_ref[D_CONV:, :])
    x1 = _layer_norm(DEEPNORM_ALPHA * x_ref[...] + mix, g_ref[...], b_ref[...])
    x1_ref[...] = x1

    x_hi = x1.astype(BF16)
    x_lo = (x1 - x_hi.astype(F32)).astype(BF16)
    wr = wr_ref[...]
    w_hi = wr.astype(BF16)
    w_lo = (wr - w_hi.astype(F32)).astype(BF16)
    logits = _dot(x_hi, w_hi) + _dot(x_lo, w_hi) + _dot(x_hi, w_lo) + br_ref[...]

    lane = lax.broadcasted_iota(jnp.int32, (t, LANES), 1).astype(F32)
    big = float(LANES)

    lg = jnp.where((lane >= GROUP_LANE0) & (lane < GROUP_LANE0 + N_GROUPS), logits, NEG_INF)
    gmax = jnp.max(lg, axis=-1, keepdims=True)
    g_p = 1.0 / jnp.sum(jnp.exp(lg - gmax), axis=-1, keepdims=True)
    g_idx = jnp.min(jnp.where(lg == gmax, lane, big), axis=-1, keepdims=True) - GROUP_LANE0

    lo = g_idx * EXPERTS_PER_GROUP
    le = jnp.where((lane >= lo) & (lane < lo + EXPERTS_PER_GROUP), logits, NEG_INF)
    m1 = jnp.max(le, axis=-1, keepdims=True)
    i1 = jnp.min(jnp.where(le == m1, lane, big), axis=-1, keepdims=True)
    le2 = jnp.where(lane == i1, NEG_INF, le)
    m2 = jnp.max(le2, axis=-1, keepdims=True)
    i2 = jnp.min(jnp.where(le2 == m2, lane, big), axis=-1, keepdims=True)
    p2 = jnp.exp(m2 - m1)
    den = 1.0 + p2
    w1 = g_p * (1.0 / den)
    w2 = g_p * (p2 / den)

    oh = (lane == i1) | (lane == i2)
    ohf = oh.astype(F32)
    r_i = lax.broadcasted_iota(jnp.int32, (t, t), 0)
    c_i = lax.broadcasted_iota(jnp.int32, (t, t), 1)
    tri = (r_i > c_i).astype(BF16)
    tot = base_ref[...] + _dot(tri, ohf.astype(BF16))
    r1 = jnp.sum(jnp.where(lane == i1, tot, 0.0), axis=-1, keepdims=True)
    r2 = jnp.sum(jnp.where(lane == i2, tot, 0.0), axis=-1, keepdims=True)
    base = base_ref[...] + jnp.sum(ohf, axis=0, keepdims=True)
    base_ref[...] = base
    cnt_ref[...] = jnp.broadcast_to(base, cnt_ref.shape)

    ri = jnp.where(lane == 0, i1, jnp.where(lane == 1, i2, jnp.where(lane == 2, r1, r2)))
    ri_ref[...] = ri.astype(jnp.int32)
    rw_ref[...] = jnp.where(lane == 0, w1, w2)


def _mix_route(yc, ya, x2d, w_out_bf, g, b, w_r, b_r):
    n = x2d.shape[0]
    t = TOKEN_TILE
    row = lambda w: pl.BlockSpec((t, w), lambda i: (i, 0))
    full = lambda a: pl.BlockSpec(a.shape, lambda i: (0, 0))
    return pl.pallas_call(
        _mix_route_kernel,
        grid=(n // t,),
        in_specs=[row(D_CONV), row(D_ATTN), row(D_MODEL), full(w_out_bf), full(g), full(b),
                  full(w_r), full(b_r)],
        out_specs=[row(D_MODEL), row(LANES), row(LANES), pl.BlockSpec((8, LANES), lambda i: (0, 0))],
        out_shape=[
            jax.ShapeDtypeStruct((n, D_MODEL), F32),
            jax.ShapeDtypeStruct((n, LANES), jnp.int32),
            jax.ShapeDtypeStruct((n, LANES), F32),
            jax.ShapeDtypeStruct((8, LANES), F32),
        ],
        scratch_shapes=[pltpu.VMEM((1, LANES), F32)],
        compiler_params=pltpu.CompilerParams(
            dimension_semantics=("arbitrary",), vmem_limit_bytes=VMEM_LIMIT),
        name="mix_route",
    )(yc, ya, x2d, w_out_bf, g, b, w_r, b_r)


def _row_gather_start(idx_ref, n_rows, src_hbm, dst, sem):
    def body(r, c):
        pltpu.make_async_copy(src_hbm.at[pl.ds(idx_ref[0, r], 1)], dst.at[pl.ds(r, 1)], sem).start()
        return c
    lax.fori_loop(0, n_rows, body, 0, unroll=8)


def _row_gather_wait(n_rows, src_hbm, dst, sem):
    pltpu.make_async_copy(src_hbm.at[pl.ds(0, n_rows)], dst, sem).wait()


def _experts_kernel(be_ref, nu_ref, cur_ref, nxt_ref, x1_hbm, wg_ref, wu_ref, wd_ref,
                    yb_ref, xbuf, sem):
    i = pl.program_id(0)
    n_used = nu_ref[0]
    r = DISPATCH_BLOCK
    slot = i % 2

    @pl.when((i == 0) & (n_used > 0))
    def _():
        _row_gather_start(cur_ref, r, x1_hbm, xbuf.at[0], sem.at[0])

    @pl.when(i + 1 < n_used)
    def _():
        _row_gather_start(nxt_ref, r, x1_hbm, xbuf.at[1 - slot], sem.at[1 - slot])

    @pl.when(i < n_used)
    def _():
        _row_gather_wait(r, x1_hbm, xbuf.at[slot], sem.at[slot])
        xb = xbuf[slot].astype(BF16)
        gate = _dot(xb, wg_ref[...].astype(BF16))
        up = _dot(xb, wu_ref[...].astype(BF16))
        hid = (jax.nn.silu(gate) * up).astype(BF16)
        yb_ref[...] = _dot(hid, wd_ref[...].astype(BF16))

    @pl.when(i >= n_used)
    def _():
        yb_ref[...] = jnp.zeros_like(yb_ref)


def _experts(blk_expert, n_used, tos, x1, w_gate, w_up, w_down):
    n_blk = blk_expert.shape[0]
    r = DISPATCH_BLOCK
    tos3 = tos.reshape(n_blk, 1, r)
    smem_blk = lambda f: pl.BlockSpec((None, 1, r), f, memory_space=pltpu.SMEM)
    grid_spec = pltpu.PrefetchScalarGridSpec(
        num_scalar_prefetch=2,
        grid=(n_blk,),
        in_specs=[
            smem_blk(lambda i, be, nu: (i, 0, 0)),
            smem_blk(lambda i, be, nu: (jnp.minimum(i + 1, n_blk - 1), 0, 0)),
            pl.BlockSpec(memory_space=pl.ANY),
            pl.BlockSpec((None, D_MODEL, D_EXPERT), lambda i, be, nu: (be[i], 0, 0)),
            pl.BlockSpec((None, D_MODEL, D_EXPERT), lambda i, be, nu: (be[i], 0, 0)),
            pl.BlockSpec((None, D_EXPERT, D_MODEL), lambda i, be, nu: (be[i], 0, 0)),
        ],
        out_specs=pl.BlockSpec((r, D_MODEL), lambda i, be, nu: (i, 0)),
        scratch_shapes=[pltpu.VMEM((2, r, D_MODEL), F32), pltpu.SemaphoreType.DMA((2,))],
    )
    return pl.pallas_call(
        _experts_kernel,
        grid_spec=grid_spec,
        out_shape=jax.ShapeDtypeStruct((n_blk * r, D_MODEL), F32),
        compiler_params=pltpu.CompilerParams(
            dimension_semantics=("arbitrary",), vmem_limit_bytes=VMEM_LIMIT),
        name="experts",
    )(blk_expert, n_used, tos3, tos3, x1, w_gate, w_up, w_down)


def _combine_kernel(cur_ref, nxt_ref, x1_ref, p_ref, rw_ref, wpg_ref, wpp_ref, g_ref, b_ref,
                    yb_hbm, o_ref, gbuf, sem):
    i = pl.program_id(0)
    n_steps = pl.num_programs(0)
    t = x1_ref.shape[0]
    slot = i % 2

    @pl.when(i == 0)
    def _():
        _row_gather_start(cur_ref, 2 * t, yb_hbm, gbuf.at[0], sem.at[0])

    @pl.when(i + 1 < n_steps)
    def _():
        _row_gather_start(nxt_ref, 2 * t, yb_hbm, gbuf.at[1 - slot], sem.at[1 - slot])

    x1 = x1_ref[...]
    ple = jax.nn.sigmoid(_dot(x1.astype(BF16), wpg_ref[...])) * _dot(p_ref[...].astype(BF16), wpp_ref[...])

    _row_gather_wait(2 * t, yb_hbm, gbuf.at[slot], sem.at[slot])
    rw = rw_ref[...]
    ffn = gbuf[slot, 0:t, :] * rw[:, 0:1] + gbuf[slot, t:2 * t, :] * rw[:, 1:2]
    o_ref[...] = _layer_norm(DEEPNORM_ALPHA * x1 + ffn + ple, g_ref[...], b_ref[...])


def _combine(slots, x1, p2d, rw, w_pg_bf, w_pp_bf, g, b, yb):
    n = x1.shape[0]
    t = TOKEN_TILE
    n_steps = n // t
    slots3 = slots.reshape(n_steps, 1, 2 * t)
    smem_blk = lambda f: pl.BlockSpec((None, 1, 2 * t), f, memory_space=pltpu.SMEM)
    row = lambda w: pl.BlockSpec((t, w), lambda i: (i, 0))
    full = lambda a: pl.BlockSpec(a.shape, lambda i: (0, 0))
    return pl.pallas_call(
        _combine_kernel,
        grid=(n_steps,),
        in_specs=[
            smem_blk(lambda i: (i, 0, 0)),
            smem_blk(lambda i: (jnp.minimum(i + 1, n_steps - 1), 0, 0)),
            row(D_MODEL), row(PLE_DIM), row(LANES), full(w_pg_bf), full(w_pp_bf), full(g), full(b),
            pl.BlockSpec(memory_space=pl.ANY),
        ],
        out_specs=row(D_MODEL),
        out_shape=jax.ShapeDtypeStruct((n, D_MODEL), F32),
        scratch_shapes=[pltpu.VMEM((2, 2 * t, D_MODEL), F32), pltpu.SemaphoreType.DMA((2,))],
        compiler_params=pltpu.CompilerParams(
            dimension_semantics=("arbitrary",), vmem_limit_bytes=VMEM_LIMIT),
        name="combine",
    )(slots3, slots3, x1, p2d, rw, w_pg_bf, w_pp_bf, g, b, yb)


def _layer(x, p, w_in, w_conv, w_out, ln1_g, ln1_b, w_router_g, b_router_g, w_router_e,
           b_router_e, w_gate, w_up, w_down, w_ple_gate, w_ple_proj, ln2_g, ln2_b):
    b, s, d = x.shape
    n = b * s
    r = DISPATCH_BLOCK

    yc, q, k, vt, kmean = _proj_conv(x, w_in.astype(BF16), w_conv)
    ya = _moba(q, k, vt, kmean.reshape(b, s // MOBA_BLOCK, D_ATTN))

    w_r = jnp.zeros((d, LANES), F32).at[:, :N_EXPERTS].set(w_router_e)
    w_r = w_r.at[:, GROUP_LANE0:GROUP_LANE0 + N_GROUPS].set(w_router_g)
    b_r = jnp.zeros((1, LANES), F32).at[0, :N_EXPERTS].set(b_router_e)
    b_r = b_r.at[0, GROUP_LANE0:GROUP_LANE0 + N_GROUPS].set(b_router_g)
    x1, ri, rw, cnt = _mix_route(
        yc.reshape(n, D_CONV), ya.reshape(n, D_ATTN), x.reshape(n, d), w_out.astype(BF16),
        ln1_g.reshape(1, d), ln1_b.reshape(1, d), w_r, b_r)

    counts = cnt[0, :N_EXPERTS].astype(jnp.int32)
    padded = (counts + r - 1) // r * r
    pad_end = jnp.cumsum(padded)
    pad_start = pad_end - padded
    n_blk = -(-(2 * n) // r) + N_EXPERTS
    slot1 = pad_start[ri[:, 0]] + ri[:, 2]
    slot2 = pad_start[ri[:, 1]] + ri[:, 3]
    tok = jnp.arange(n, dtype=jnp.int32)
    tos = jnp.zeros((n_blk * r,), jnp.int32).at[jnp.concatenate([slot1, slot2])].set(
        jnp.concatenate([tok, tok]), unique_indices=True)
    blk_row0 = jnp.arange(n_blk, dtype=jnp.int32) * r
    blk_expert = jnp.minimum(
        jnp.sum((pad_end[None, :] <= blk_row0[:, None]).astype(jnp.int32), axis=1), N_EXPERTS - 1)
    n_used = (pad_end[-1:] // r).astype(jnp.int32)

    yb = _experts(blk_expert, n_used, tos, x1, w_gate, w_up, w_down)

    nt = n // TOKEN_TILE
    slots = jnp.concatenate(
        [slot1.reshape(nt, TOKEN_TILE), slot2.reshape(nt, TOKEN_TILE)], axis=1)
    out = _combine(slots, x1, p.reshape(n, PLE_DIM), rw, w_ple_gate.astype(BF16),
                   w_ple_proj.astype(BF16), ln2_g.reshape(1, d), ln2_b.reshape(1, d), yb)
    return out.reshape(b, s, d)


def kernel(x, p, w_in, w_conv, w_out, ln1_g, ln1_b, w_router_g, b_router_g, w_router_e,
           b_router_e, w_gate, w_up, w_down, w_ple_gate, w_ple_proj, ln2_g, ln2_b):
    for i in range(DEPTH):
        x = _layer(x, p[i], w_in[i], w_conv[i], w_out[i], ln1_g[i], ln1_b[i], w_router_g[i],
                   b_router_g[i], w_router_e[i], b_router_e[i], w_gate[i], w_up[i], w_down[i],
                   w_ple_gate[i], w_ple_proj[i], ln2_g[i], ln2_b[i])
    return x
```

```python
import functools

import jax
import jax.numpy as jnp
from jax import lax
from jax.experimental import pallas as pl
from jax.experimental.pallas import tpu as pltpu

D_MODEL = 1024
D_CONV = 512
N_HEADS = 8
HEAD_DIM = 64
D_ATTN = N_HEADS * HEAD_DIM
D_IN = 3 * D_CONV + 3 * D_ATTN
MOBA_BLOCK = 256
MOBA_TOPK = 3
N_GROUPS = 4
EXPERTS_PER_GROUP = 8
N_EXPERTS = N_GROUPS * EXPERTS_PER_GROUP
D_EXPERT = 512
DISPATCH_BLOCK = 256
PLE_DIM = 256
LN_EPS = 1e-5
DEPTH = 1
DEEPNORM_ALPHA = (2 * DEPTH) ** 0.25

LANES = 128
PROJ_ROWS = 512
TOKEN_TILE = 256
KV_CHUNK_SHIFT = 2
KV_CHUNK_BLOCKS = 1 << KV_CHUNK_SHIFT
BF16_SUBLANES = 16
ACC_ROWS = HEAD_DIM + BF16_SUBLANES
Q_SCALE = HEAD_DIM ** -0.5 * 1.4426950408889634
VMEM_LIMIT = 48 * 1024 * 1024

F32 = jnp.float32
BF16 = jnp.bfloat16
NEG_INF = float("-inf")


def _dot(a, b):
    return jnp.dot(a, b, preferred_element_type=F32)


def _dot_nt(a, b):
    return lax.dot_general(a, b, (((1,), (1,)), ((), ())), preferred_element_type=F32)


def _layer_norm(h, g, b):
    mu = jnp.mean(h, axis=-1, keepdims=True)
    d = h - mu
    var = jnp.mean(d * d, axis=-1, keepdims=True)
    return d * lax.rsqrt(var + LN_EPS) * g + b


def _proj_conv_kernel(x_ref, w_ref, wc_ref, yc_ref, q_ref, k_ref, vt_ref, km_ref, ubuf):
    s = pl.program_id(1)
    rows = x_ref.shape[0]
    xb = x_ref[...].astype(BF16)

    def sect(i):
        return _dot(xb, w_ref[:, i * D_CONV:(i + 1) * D_CONV])

    @pl.when(s == 0)
    def _():
        ubuf[0:8, :] = jnp.zeros((8, D_CONV), F32)

    u = sect(1) * sect(2)
    ubuf[8:8 + rows, :] = u
    wc = wc_ref[...]
    conv = wc[0:1, :] * ubuf[6:6 + rows, :] + wc[1:2, :] * ubuf[7:7 + rows, :] + wc[2:3, :] * u
    yc_ref[...] = (sect(0) * conv).astype(BF16)
    ubuf[0:8, :] = ubuf[rows:rows + 8, :]

    q_ref[...] = (sect(3) * Q_SCALE).astype(BF16)
    k = sect(4)
    k_ref[...] = k.astype(BF16)
    for i in range(rows // MOBA_BLOCK):
        km_ref[i] = jnp.mean(k[i * MOBA_BLOCK:(i + 1) * MOBA_BLOCK, :], axis=0, keepdims=True)
    vt_ref[...] = sect(5).T.astype(BF16)


def _proj_conv(x, w_in_bf, w_conv):
    b, s, _ = x.shape
    nb = s // MOBA_BLOCK
    grid = (b, s // PROJ_ROWS)
    seq_spec = pl.BlockSpec((None, PROJ_ROWS, D_CONV), lambda i, j: (i, j, 0))
    return pl.pallas_call(
        _proj_conv_kernel,
        grid=grid,
        in_specs=[
            pl.BlockSpec((None, PROJ_ROWS, D_MODEL), lambda i, j: (i, j, 0)),
            pl.BlockSpec((D_MODEL, D_IN), lambda i, j: (0, 0)),
            pl.BlockSpec((3, D_CONV), lambda i, j: (0, 0)),
        ],
        out_specs=[
            seq_spec, seq_spec, seq_spec,
            pl.BlockSpec((None, D_ATTN, PROJ_ROWS), lambda i, j: (i, 0, j)),
            pl.BlockSpec((None, PROJ_ROWS // MOBA_BLOCK, 1, D_ATTN), lambda i, j: (i, j, 0, 0)),
        ],
        out_shape=[
            jax.ShapeDtypeStruct((b, s, D_CONV), BF16),
            jax.ShapeDtypeStruct((b, s, D_ATTN), BF16),
            jax.ShapeDtypeStruct((b, s, D_ATTN), BF16),
            jax.ShapeDtypeStruct((b, D_ATTN, s), BF16),
            jax.ShapeDtypeStruct((b, nb, 1, D_ATTN), F32),
        ],
        scratch_shapes=[pltpu.VMEM((PROJ_ROWS + 8, D_CONV), F32)],
        compiler_params=pltpu.CompilerParams(
            dimension_semantics=("arbitrary", "arbitrary"), vmem_limit_bytes=VMEM_LIMIT),
        name="proj_conv",
    )(x, w_in_bf, w_conv)


def _moba_kernel(q_ref, k_ref, vt_ref, km_ref, o_ref, bias_ref, sd_ref, s_ref, mt_ref, m_ref,
                 acc_ref):
    j = pl.program_id(2)
    blk = MOBA_BLOCK
    nb = km_ref.shape[0]
    ck = KV_CHUNK_BLOCKS * blk
    n_chunks = lax.shift_right_logical(j + (KV_CHUNK_BLOCKS - 1), KV_CHUNK_SHIFT)

    qt = q_ref[...].astype(F32).T
    row = lax.broadcasted_iota(jnp.int32, qt.shape, 0)
    qt_heads = (jnp.where(row < HEAD_DIM, qt, 0.0).astype(BF16),
                jnp.where(row >= HEAD_DIM, qt, 0.0).astype(BF16))

    km = km_ref[...]
    km_hi = km.astype(BF16)
    km_lo = (km - km_hi.astype(F32)).astype(BF16)
    bidx = lax.broadcasted_iota(jnp.int32, (nb, blk), 0)
    for h in range(2):
        g = _dot(km_hi, qt_heads[h]) + _dot(km_lo, qt_heads[h])
        g = jnp.where(bidx < j, g, NEG_INF)
        sel = jnp.zeros((nb, blk), jnp.bool_)
        for _ in range(MOBA_TOPK):
            mx = jnp.max(g, axis=0, keepdims=True)
            cand = jnp.where((g == mx) & (mx > NEG_INF), bidx, nb)
            pick = bidx == jnp.min(cand, axis=0, keepdims=True)
            sel = sel | pick
            g = jnp.where(pick, NEG_INF, g)
        bias_ref[h] = jnp.where(sel, 0.0, NEG_INF)
        m_ref[h] = jnp.full((1, blk), NEG_INF, F32)
        acc_ref[h] = jnp.zeros(acc_ref.shape[1:], F32)

    def stage_block(c, u, slot):
        n = c * KV_CHUNK_BLOCKS + u
        kn = k_ref[pl.ds(pl.multiple_of(n * blk, blk), blk), :]
        mts = []
        for h in range(2):
            sc = _dot(kn, qt_heads[h]) + bias_ref[h, pl.ds(n, 1), :]
            s_ref[slot, h, u * blk:(u + 1) * blk, :] = sc
            mts.append(jnp.max(sc, axis=0, keepdims=True))
        return mts

    def pv(vt_h, sc, m_new):
        p = jnp.exp2(sc - m_new).astype(BF16)
        lhs = jnp.concatenate([vt_h, jnp.ones((ACC_ROWS - HEAD_DIM, vt_h.shape[1]), BF16)], axis=0)
        return _dot(lhs, p)

    def rescale(h, mt):
        m_old = m_ref[h]
        m_new = jnp.maximum(m_old, mt)
        m_ref[h] = m_new
        return m_new, jnp.exp2(m_old - m_new)

    def step(consume, stage):
        if consume is not None:
            cc, cs = consume
            scale = [rescale(h, mt_ref[1 + cs, h]) for h in range(2)]
            parts = [None, None]
        mts = []
        for u in range(KV_CHUNK_BLOCKS):
            if stage is not None:
                mts.append(stage_block(stage[0], u, stage[1]))
            if consume is not None:
                n0 = pl.multiple_of((cc * KV_CHUNK_BLOCKS + u) * blk, blk)
                for h in range(2):
                    vt_h = vt_ref[h * HEAD_DIM:(h + 1) * HEAD_DIM, pl.ds(n0, blk)]
                    r = pv(vt_h, s_ref[cs, h, u * blk:(u + 1) * blk, :], scale[h][0])
                    parts[h] = r if parts[h] is None else parts[h] + r
        for h in range(2):
            if consume is not None:
                acc_ref[h] = scale[h][1] * acc_ref[h] + parts[h]
            if stage is not None:
                mt_ref[1 + stage[1], h] = functools.reduce(jnp.maximum, [m[h] for m in mts])

    j0 = pl.multiple_of(j * blk, blk)
    kj = k_ref[pl.ds(j0, blk), :]
    vtj = vt_ref[:, pl.ds(j0, blk)]
    kpos = lax.broadcasted_iota(jnp.int32, (blk, blk), 0)
    qpos = lax.broadcasted_iota(jnp.int32, (blk, blk), 1)
    causal = kpos <= qpos
    for h in range(2):
        sc = jnp.where(causal, _dot(kj, qt_heads[h]), NEG_INF)
        sd_ref[h] = sc
        mt_ref[0, h] = jnp.max(sc, axis=0, keepdims=True)
    step(None, (0, 0))
    for h in range(2):
        m_new, alpha = rescale(h, mt_ref[0, h])
        acc_ref[h] = alpha * acc_ref[h] + pv(vtj[h * HEAD_DIM:(h + 1) * HEAD_DIM, :], sd_ref[h], m_new)

    steps = jnp.maximum(n_chunks - 1, 0)

    def body(i, carry):
        c = 2 * i
        step((c, 0), (c + 1, 1))
        step((c + 1, 1), (c + 2, 0))
        return carry

    lax.fori_loop(0, lax.shift_right_logical(steps, 1), body, 0)

    @pl.when((n_chunks > 0) & (steps % 2 == 1))
    def _():
        step((n_chunks - 2, 0), (n_chunks - 1, 1))
        step((n_chunks - 1, 1), None)

    @pl.when((n_chunks > 0) & (steps % 2 == 0))
    def _():
        step((n_chunks - 1, 0), None)

    outs = [acc_ref[h, 0:HEAD_DIM, :] / acc_ref[h, HEAD_DIM:HEAD_DIM + 1, :] for h in range(2)]
    o_ref[...] = jnp.concatenate(outs, axis=0).T.astype(BF16)


def _moba(q, k, vt, kmean):
    b, s, _ = q.shape
    nb = s // MOBA_BLOCK
    n_pairs = D_ATTN // LANES
    return pl.pallas_call(
        _moba_kernel,
        grid=(b, n_pairs, nb),
        in_specs=[
            pl.BlockSpec((None, MOBA_BLOCK, LANES), lambda i, p, j: (i, j, p)),
            pl.BlockSpec((None, s, LANES), lambda i, p, j: (i, 0, p)),
            pl.BlockSpec((None, LANES, s), lambda i, p, j: (i, p, 0)),
            pl.BlockSpec((None, nb, LANES), lambda i, p, j: (i, 0, p)),
        ],
        out_specs=pl.BlockSpec((None, MOBA_BLOCK, LANES), lambda i, p, j: (i, j, p)),
        out_shape=jax.ShapeDtypeStruct((b, s, D_ATTN), BF16),
        scratch_shapes=[
            pltpu.VMEM((2, nb, MOBA_BLOCK), F32),
            pltpu.VMEM((2, MOBA_BLOCK, MOBA_BLOCK), F32),
            pltpu.VMEM((2, 2, KV_CHUNK_BLOCKS * MOBA_BLOCK, MOBA_BLOCK), F32),
            pltpu.VMEM((3, 2, 1, MOBA_BLOCK), F32),
            pltpu.VMEM((2, 1, MOBA_BLOCK), F32),
            pltpu.VMEM((2, ACC_ROWS, MOBA_BLOCK), F32),
        ],
        compiler_params=pltpu.CompilerParams(
            dimension_semantics=("arbitrary", "arbitrary", "arbitrary"),
            vmem_limit_bytes=VMEM_LIMIT),
        name="moba",
    )(q, k, vt, kmean)


GROUP_LANE0 = N_EXPERTS


def _mix_route_kernel(yc_ref, ya_ref, x_ref, wo_ref, g_ref, b_ref, wr_ref, br_ref,
                      x1_ref, ri_ref, rw_ref, cnt_ref, base_ref):
    i = pl.program_id(0)
    t = x_ref.shape[0]

    @pl.when(i == 0)
    def _():
        base_ref[...] = jnp.zeros_like(base_ref)

    mix = _dot(yc_ref[...], wo_ref[0:D_CONV, :]) + _dot(ya_ref[...], wo_ref[D_CONV:, :])
    x1 = _layer_norm(DEEPNORM_ALPHA * x_ref[...] + mix, g_ref[...], b_ref[...])
    x1_ref[...] = x1

    x_hi = x1.astype(BF16)
    x_lo = (x1 - x_hi.astype(F32)).astype(BF16)
    wr = wr_ref[...]
    w_hi = wr.astype(BF16)
    w_lo = (wr - w_hi.astype(F32)).astype(BF16)
    logits = _dot(x_hi, w_hi) + _dot(x_lo, w_hi) + _dot(x_hi, w_lo) + br_ref[...]

    lane = lax.broadcasted_iota(jnp.int32, (t, LANES), 1).astype(F32)
    big = float(LANES)

    lg = jnp.where((lane >= GROUP_LANE0) & (lane < GROUP_LANE0 + N_GROUPS), logits, NEG_INF)
    gmax = jnp.max(lg, axis=-1, keepdims=True)
    g_p = 1.0 / jnp.sum(jnp.exp(lg - gmax), axis=-1, keepdims=True)
    g_idx = jnp.min(jnp.where(lg == gmax, lane, big), axis=-1, keepdims=True) - GROUP_LANE0

    lo = g_idx * EXPERTS_PER_GROUP
    le = jnp.where((lane >= lo) & (lane < lo + EXPERTS_PER_GROUP), logits, NEG_INF)
    m1 = jnp.max(le, axis=-1, keepdims=True)
    i1 = jnp.min(jnp.where(le == m1, lane, big), axis=-1, keepdims=True)
    le2 = jnp.where(lane == i1, NEG_INF, le)
    m2 = jnp.max(le2, axis=-1, keepdims=True)
    i2 = jnp.min(jnp.where(le2 == m2, lane, big), axis=-1, keepdims=True)
    p2 = jnp.exp(m2 - m1)
    den = 1.0 + p2
    w1 = g_p * (1.0 / den)
    w2 = g_p * (p2 / den)

    oh = (lane == i1) | (lane == i2)
    ohf = oh.astype(F32)
    r_i = lax.broadcasted_iota(jnp.int32, (t, t), 0)
    c_i = lax.broadcasted_iota(jnp.int32, (t, t), 1)
    tri = (r_i > c_i).astype(BF16)
    tot = base_ref[...] + _dot(tri, ohf.astype(BF16))
    r1 = jnp.sum(jnp.where(lane == i1, tot, 0.0), axis=-1, keepdims=True)
    r2 = jnp.sum(jnp.where(lane == i2, tot, 0.0), axis=-1, keepdims=True)
    base = base_ref[...] + jnp.sum(ohf, axis=0, keepdims=True)
    base_ref[...] = base
    cnt_ref[...] = jnp.broadcast_to(base, cnt_ref.shape)

    ri = jnp.where(lane == 0, i1, jnp.where(lane == 1, i2, jnp.where(lane == 2, r1, r2)))
    ri_ref[...] = ri.astype(jnp.int32)
    rw_ref[...] = jnp.where(lane == 0, w1, w2)


def _mix_route(yc, ya, x2d, w_out_bf, g, b, w_r, b_r):
    n = x2d.shape[0]
    t = TOKEN_TILE
    row = lambda w: pl.BlockSpec((t, w), lambda i: (i, 0))
    full = lambda a: pl.BlockSpec(a.shape, lambda i: (0, 0))
    return pl.pallas_call(
        _mix_route_kernel,
        grid=(n // t,),
        in_specs=[row(D_CONV), row(D_ATTN), row(D_MODEL), full(w_out_bf), full(g), full(b),
                  full(w_r), full(b_r)],
        out_specs=[row(D_MODEL), row(LANES), row(LANES), pl.BlockSpec((8, LANES), lambda i: (0, 0))],
        out_shape=[
            jax.ShapeDtypeStruct((n, D_MODEL), F32),
            jax.ShapeDtypeStruct((n, LANES), jnp.int32),
            jax.ShapeDtypeStruct((n, LANES), F32),
            jax.ShapeDtypeStruct((8, LANES), F32),
        ],
        scratch_shapes=[pltpu.VMEM((1, LANES), F32)],
        compiler_params=pltpu.CompilerParams(
            dimension_semantics=("arbitrary",), vmem_limit_bytes=VMEM_LIMIT),
        name="mix_route",
    )(yc, ya, x2d, w_out_bf, g, b, w_r, b_r)


def _row_gather_start(idx_ref, n_rows, src_hbm, dst, sem):
    def body(r, c):
        pltpu.make_async_copy(src_hbm.at[pl.ds(idx_ref[0, r], 1)], dst.at[pl.ds(r, 1)], sem).start()
        return c
    lax.fori_loop(0, n_rows, body, 0, unroll=8)


def _row_gather_wait(n_rows, src_hbm, dst, sem):
    pltpu.make_async_copy(src_hbm.at[pl.ds(0, n_rows)], dst, sem).wait()


def _experts_kernel(be_ref, nu_ref, cur_ref, nxt_ref, x1_hbm, wg_ref, wu_ref, wd_ref,
                    yb_ref, xbuf, sem):
    i = pl.program_id(0)
    n_used = nu_ref[0]
    r = DISPATCH_BLOCK
    slot = i % 2

    @pl.when((i == 0) & (n_used > 0))
    def _():
        _row_gather_start(cur_ref, r, x1_hbm, xbuf.at[0], sem.at[0])

    @pl.when(i + 1 < n_used)
    def _():
        _row_gather_start(nxt_ref, r, x1_hbm, xbuf.at[1 - slot], sem.at[1 - slot])

    @pl.when(i < n_used)
    def _():
        _row_gather_wait(r, x1_hbm, xbuf.at[slot], sem.at[slot])
        xb = xbuf[slot].astype(BF16)
        gate = _dot(xb, wg_ref[...].astype(BF16))
        up = _dot(xb, wu_ref[...].astype(BF16))
        hid = (jax.nn.silu(gate) * up).astype(BF16)
        yb_ref[...] = _dot(hid, wd_ref[...].astype(BF16))

    @pl.when(i >= n_used)
    def _():
        yb_ref[...] = jnp.zeros_like(yb_ref)


def _experts(blk_expert, n_used, tos, x1, w_gate, w_up, w_down):
    n_blk = blk_expert.shape[0]
    r = DISPATCH_BLOCK
    tos3 = tos.reshape(n_blk, 1, r)
    smem_blk = lambda f: pl.BlockSpec((None, 1, r), f, memory_space=pltpu.SMEM)
    grid_spec = pltpu.PrefetchScalarGridSpec(
        num_scalar_prefetch=2,
        grid=(n_blk,),
        in_specs=[
            smem_blk(lambda i, be, nu: (i, 0, 0)),
            smem_blk(lambda i, be, nu: (jnp.minimum(i + 1, n_blk - 1), 0, 0)),
            pl.BlockSpec(memory_space=pl.ANY),
            pl.BlockSpec((None, D_MODEL, D_EXPERT), lambda i, be, nu: (be[i], 0, 0)),
            pl.BlockSpec((None, D_MODEL, D_EXPERT), lambda i, be, nu: (be[i], 0, 0)),
            pl.BlockSpec((None, D_EXPERT, D_MODEL), lambda i, be, nu: (be[i], 0, 0)),
        ],
        out_specs=pl.BlockSpec((r, D_MODEL), lambda i, be, nu: (i, 0)),
        scratch_shapes=[pltpu.VMEM((2, r, D_MODEL), F32), pltpu.SemaphoreType.DMA((2,))],
    )
    return pl.pallas_call(
        _experts_kernel,
        grid_spec=grid_spec,
        out_shape=jax.ShapeDtypeStruct((n_blk * r, D_MODEL), F32),
        compiler_params=pltpu.CompilerParams(
            dimension_semantics=("arbitrary",), vmem_limit_bytes=VMEM_LIMIT),
        name="experts",
    )(blk_expert, n_used, tos3, tos3, x1, w_gate, w_up, w_down)


def _combine_kernel(cur_ref, nxt_ref, x1_ref, p_ref, rw_ref, wpg_ref, wpp_ref, g_ref, b_ref,
                    yb_hbm, o_ref, gbuf, sem):
    i = pl.program_id(0)
    n_steps = pl.num_programs(0)
    t = x1_ref.shape[0]
    slot = i % 2

    @pl.when(i == 0)
    def _():
        _row_gather_start(cur_ref, 2 * t, yb_hbm, gbuf.at[0], sem.at[0])

    @pl.when(i + 1 < n_steps)
    def _():
        _row_gather_start(nxt_ref, 2 * t, yb_hbm, gbuf.at[1 - slot], sem.at[1 - slot])

    x1 = x1_ref[...]
    ple = jax.nn.sigmoid(_dot(x1.astype(BF16), wpg_ref[...])) * _dot(p_ref[...].astype(BF16), wpp_ref[...])

    _row_gather_wait(2 * t, yb_hbm, gbuf.at[slot], sem.at[slot])
    rw = rw_ref[...]
    ffn = gbuf[slot, 0:t, :] * rw[:, 0:1] + gbuf[slot, t:2 * t, :] * rw[:, 1:2]
    o_ref[...] = _layer_norm(DEEPNORM_ALPHA * x1 + ffn + ple, g_ref[...], b_ref[...])


def _combine(slots, x1, p2d, rw, w_pg_bf, w_pp_bf, g, b, yb):
    n = x1.shape[0]
    t = TOKEN_TILE
    n_steps = n // t
    slots3 = slots.reshape(n_steps, 1, 2 * t)
    smem_blk = lambda f: pl.BlockSpec((None, 1, 2 * t), f, memory_space=pltpu.SMEM)
    row = lambda w: pl.BlockSpec((t, w), lambda i: (i, 0))
    full = lambda a: pl.BlockSpec(a.shape, lambda i: (0, 0))
    return pl.pallas_call(
        _combine_kernel,
        grid=(n_steps,),
        in_specs=[
            smem_blk(lambda i: (i, 0, 0)),
            smem_blk(lambda i: (jnp.minimum(i + 1, n_steps - 1), 0, 0)),
            row(D_MODEL), row(PLE_DIM), row(LANES), full(w_pg_bf), full(w_pp_bf), full(g), full(b),
            pl.BlockSpec(memory_space=pl.ANY),
        ],
        out_specs=row(D_MODEL),
        out_shape=jax.ShapeDtypeStruct((n, D_MODEL), F32),
        scratch_shapes=[pltpu.VMEM((2, 2 * t, D_MODEL), F32), pltpu.SemaphoreType.DMA((2,))],
        compiler_params=pltpu.CompilerParams(
            dimension_semantics=("arbitrary",), vmem_limit_bytes=VMEM_LIMIT),
        name="combine",
    )(slots3, slots3, x1, p2d, rw, w_pg_bf, w_pp_bf, g, b, yb)


def _layer(x, p, w_in, w_conv, w_out, ln1_g, ln1_b, w_router_g, b_router_g, w_router_e,
           b_router_e, w_gate, w_up, w_down, w_ple_gate, w_ple_proj, ln2_g, ln2_b):
    b, s, d = x.shape
    n = b * s
    r = DISPATCH_BLOCK

    yc, q, k, vt, kmean = _proj_conv(x, w_in.astype(BF16), w_conv)
    ya = _moba(q, k, vt, kmean.reshape(b, s // MOBA_BLOCK, D_ATTN))

    w_r = jnp.zeros((d, LANES), F32).at[:, :N_EXPERTS].set(w_router_e)
    w_r = w_r.at[:, GROUP_LANE0:GROUP_LANE0 + N_GROUPS].set(w_router_g)
    b_r = jnp.zeros((1, LANES), F32).at[0, :N_EXPERTS].set(b_router_e)
    b_r = b_r.at[0, GROUP_LANE0:GROUP_LANE0 + N_GROUPS].set(b_router_g)
    x1, ri, rw, cnt = _mix_route(
        yc.reshape(n, D_CONV), ya.reshape(n, D_ATTN), x.reshape(n, d), w_out.astype(BF16),
        ln1_g.reshape(1, d), ln1_b.reshape(1, d), w_r, b_r)

    counts = cnt[0, :N_EXPERTS].astype(jnp.int32)
    padded = (counts + r - 1) // r * r
    pad_end = jnp.cumsum(padded)
    pad_start = pad_end - padded
    n_blk = -(-(2 * n) // r) + N_EXPERTS
    slot1 = pad_start[ri[:, 0]] + ri[:, 2]
    slot2 = pad_start[ri[:, 1]] + ri[:, 3]
    tok = jnp.arange(n, dtype=jnp.int32)
    tos = jnp.zeros((n_blk * r,), jnp.int32).at[jnp.concatenate([slot1, slot2])].set(
        jnp.concatenate([tok, tok]), unique_indices=True)
    blk_row0 = jnp.arange(n_blk, dtype=jnp.int32) * r
    blk_expert = jnp.minimum(
        jnp.sum((pad_end[None, :] <= blk_row0[:, None]).astype(jnp.int32), axis=1), N_EXPERTS - 1)
    n_used = (pad_end[-1:] // r).astype(jnp.int32)

    yb = _experts(blk_expert, n_used, tos, x1, w_gate, w_up, w_down)

    nt = n // TOKEN_TILE
    slots = jnp.concatenate(
        [slot1.reshape(nt, TOKEN_TILE), slot2.reshape(nt, TOKEN_TILE)], axis=1)
    out = _combine(slots, x1, p.reshape(n, PLE_DIM), rw, w_ple_gate.astype(BF16),
                   w_ple_proj.astype(BF16), ln2_g.reshape(1, d), ln2_b.reshape(1, d), yb)
    return out.reshape(b, s, d)


def kernel(x, p, w_in, w_conv, w_out, ln1_g, ln1_b, w_router_g, b_router_g, w_router_e,
           b_router_e, w_gate, w_up, w_down, w_ple_gate, w_ple_proj, ln2_g, ln2_b):
    for i in range(DEPTH):
        x = _layer(x, p[i], w_in[i], w_conv[i], w_out[i], ln1_g[i], ln1_b[i], w_router_g[i],
                   b_router_g[i], w_router_e[i], b_router_e[i], w_gate[i], w_up[i], w_down[i],
                   w_ple_gate[i], w_ple_proj[i], ln2_g[i], ln2_b[i])
    return x
```

```python
import functools

import jax
import jax.numpy as jnp
from jax import lax
from jax.experimental import pallas as pl
from jax.experimental.pallas import tpu as pltpu

D_MODEL = 1024
D_CONV = 512
N_HEADS = 8
HEAD_DIM = 64
D_ATTN = N_HEADS * HEAD_DIM
D_IN = 3 * D_CONV + 3 * D_ATTN
MOBA_BLOCK = 256
MOBA_TOPK = 3
N_GROUPS = 4
EXPERTS_PER_GROUP = 8
N_EXPERTS = N_GROUPS * EXPERTS_PER_GROUP
D_EXPERT = 512
DISPATCH_BLOCK = 256
PLE_DIM = 256
LN_EPS = 1e-5
DEPTH = 1
DEEPNORM_ALPHA = (2 * DEPTH) ** 0.25

LANES = 128
PROJ_ROWS = 512
TOKEN_TILE = 256
KV_CHUNK_SHIFT = 2
KV_CHUNK_BLOCKS = 1 << KV_CHUNK_SHIFT
BF16_SUBLANES = 16
ACC_ROWS = HEAD_DIM + BF16_SUBLANES
Q_SCALE = HEAD_DIM ** -0.5 * 1.4426950408889634
VMEM_LIMIT = 48 * 1024 * 1024

F32 = jnp.float32
BF16 = jnp.bfloat16
NEG_INF = float("-inf")


def _dot(a, b):
    return jnp.dot(a, b, preferred_element_type=F32)


def _dot_nt(a, b):
    return lax.dot_general(a, b, (((1,), (1,)), ((), ())), preferred_element_type=F32)


def _layer_norm(h, g, b):
    mu = jnp.mean(h, axis=-1, keepdims=True)
    d = h - mu
    var = jnp.mean(d * d, axis=-1, keepdims=True)
    return d * lax.rsqrt(var + LN_EPS) * g + b


def _proj_conv_kernel(x_ref, w_ref, wc_ref, yc_ref, q_ref, k_ref, vt_ref, km_ref, ubuf):
    s = pl.program_id(1)
    rows = x_ref.shape[0]
    xb = x_ref[...].astype(BF16)

    def sect(i):
        return _dot(xb, w_ref[:, i * D_CONV:(i + 1) * D_CONV])

    @pl.when(s == 0)
    def _():
        ubuf[0:8, :] = jnp.zeros((8, D_CONV), F32)

    u = sect(1) * sect(2)
    ubuf[8:8 + rows, :] = u
    wc = wc_ref[...]
    conv = wc[0:1, :] * ubuf[6:6 + rows, :] + wc[1:2, :] * ubuf[7:7 + rows, :] + wc[2:3, :] * u
    yc_ref[...] = (sect(0) * conv).astype(BF16)
    ubuf[0:8, :] = ubuf[rows:rows + 8, :]

    q_ref[...] = (sect(3) * Q_SCALE).astype(BF16)
    k = sect(4)
    k_ref[...] = k.astype(BF16)
    for i in range(rows // MOBA_BLOCK):
        km_ref[i] = jnp.mean(k[i * MOBA_BLOCK:(i + 1) * MOBA_BLOCK, :], axis=0, keepdims=True)
    vt_ref[...] = sect(5).T.astype(BF16)


def _proj_conv(x, w_in_bf, w_conv):
    b, s, _ = x.shape
    nb = s // MOBA_BLOCK
    grid = (b, s // PROJ_ROWS)
    seq_spec = pl.BlockSpec((None, PROJ_ROWS, D_CONV), lambda i, j: (i, j, 0))
    return pl.pallas_call(
        _proj_conv_kernel,
        grid=grid,
        in_specs=[
            pl.BlockSpec((None, PROJ_ROWS, D_MODEL), lambda i, j: (i, j, 0)),
            pl.BlockSpec((D_MODEL, D_IN), lambda i, j: (0, 0)),
            pl.BlockSpec((3, D_CONV), lambda i, j: (0, 0)),
        ],
        out_specs=[
            seq_spec, seq_spec, seq_spec,
            pl.BlockSpec((None, D_ATTN, PROJ_ROWS), lambda i, j: (i, 0, j)),
            pl.BlockSpec((None, PROJ_ROWS // MOBA_BLOCK, 1, D_ATTN), lambda i, j: (i, j, 0, 0)),
        ],
        out_shape=[
            jax.ShapeDtypeStruct((b, s, D_CONV), BF16),
            jax.ShapeDtypeStruct((b, s, D_ATTN), BF16),
            jax.ShapeDtypeStruct((b, s, D_ATTN), BF16),
            jax.ShapeDtypeStruct((b, D_ATTN, s), BF16),
            jax.ShapeDtypeStruct((b, nb, 1, D_ATTN), F32),
        ],
        scratch_shapes=[pltpu.VMEM((PROJ_ROWS + 8, D_CONV), F32)],
        compiler_params=pltpu.CompilerParams(
            dimension_semantics=("arbitrary", "arbitrary"), vmem_limit_bytes=VMEM_LIMIT),
        name="proj_conv",
    )(x, w_in_bf, w_conv)


def _moba_kernel(q_ref, k_ref, vt_ref, km_ref, o_ref, bias_ref, sd_ref, s_ref, mt_ref, m_ref,
                 acc_ref):
    j = pl.program_id(2)
    blk = MOBA_BLOCK
    nb = km_ref.shape[0]
    ck = KV_CHUNK_BLOCKS * blk
    n_chunks = lax.shift_right_logical(j + (KV_CHUNK_BLOCKS - 1), KV_CHUNK_SHIFT)

    qt = q_ref[...].astype(F32).T
    row = lax.broadcasted_iota(jnp.int32, qt.shape, 0)
    qt_heads = (jnp.where(row < HEAD_DIM, qt, 0.0).astype(BF16),
                jnp.where(row >= HEAD_DIM, qt, 0.0).astype(BF16))

    km = km_ref[...]
    km_hi = km.astype(BF16)
    km_lo = (km - km_hi.astype(F32)).astype(BF16)
    bidx = lax.broadcasted_iota(jnp.int32, (nb, blk), 0)
    for h in range(2):
        g = _dot(km_hi, qt_heads[h]) + _dot(km_lo, qt_heads[h])
        g = jnp.where(bidx < j, g, NEG_INF)
        sel = jnp.zeros((nb, blk), jnp.bool_)
        for _ in range(MOBA_TOPK):
            mx = jnp.max(g, axis=0, keepdims=True)
            cand = jnp.where((g == mx) & (mx > NEG_INF), bidx, nb)
            pick = bidx == jnp.min(cand, axis=0, keepdims=True)
            sel = sel | pick
            g = jnp.where(pick, NEG_INF, g)
        bias_ref[h] = jnp.where(sel, 0.0, NEG_INF)
        m_ref[h] = jnp.full((1, blk), NEG_INF, F32)
        acc_ref[h] = jnp.zeros(acc_ref.shape[1:], F32)

    def stage_block(c, u, slot):
        n = c * KV_CHUNK_BLOCKS + u
        kn = k_ref[pl.ds(pl.multiple_of(n * blk, blk), blk), :]
        mts = []
        for h in range(2):
            sc = _dot(kn, qt_heads[h]) + bias_ref[h, pl.ds(n, 1), :]
            s_ref[slot, h, u * blk:(u + 1) * blk, :] = sc
            mts.append(jnp.max(sc, axis=0, keepdims=True))
        return mts

    def pv(vt_h, sc, m_new):
        p = jnp.exp2(sc - m_new).astype(BF16)
        lhs = jnp.concatenate([vt_h, jnp.ones((ACC_ROWS - HEAD_DIM, vt_h.shape[1]), BF16)], axis=0)
        return _dot(lhs, p)

    def rescale(h, mt):
        m_old = m_ref[h]
        m_new = jnp.maximum(m_old, mt)
        m_ref[h] = m_new
        return m_new, jnp.exp2(m_old - m_new)

    def step(consume, stage):
        if consume is not None:
            cc, cs = consume
            scale = [rescale(h, mt_ref[1 + cs, h]) for h in range(2)]
            parts = [None, None]
        mts = []
        for u in range(KV_CHUNK_BLOCKS):
            if stage is not None:
                mts.append(stage_block(stage[0], u, stage[1]))
            if consume is not None:
                n0 = pl.multiple_of((cc * KV_CHUNK_BLOCKS + u) * blk, blk)
                for h in range(2):
                    vt_h = vt_ref[h * HEAD_DIM:(h + 1) * HEAD_DIM, pl.ds(n0, blk)]
                    r = pv(vt_h, s_ref[cs, h, u * blk:(u + 1) * blk, :], scale[h][0])
                    parts[h] = r if parts[h] is None else parts[h] + r
        for h in range(2):
            if consume is not None:
                acc_ref[h] = scale[h][1] * acc_ref[h] + parts[h]
            if stage is not None:
                mt_ref[1 + stage[1], h] = functools.reduce(jnp.maximum, [m[h] for m in mts])

    j0 = pl.multiple_of(j * blk, blk)
    kj = k_ref[pl.ds(j0, blk), :]
    vtj = vt_ref[:, pl.ds(j0, blk)]
    kpos = lax.broadcasted_iota(jnp.int32, (blk, blk), 0)
    qpos = lax.broadcasted_iota(jnp.int32, (blk, blk), 1)
    causal = kpos <= qpos
    for h in range(2):
        sc = jnp.where(causal, _dot(kj, qt_heads[h]), NEG_INF)
        sd_ref[h] = sc
        mt_ref[0, h] = jnp.max(sc, axis=0, keepdims=True)
    step(None, (0, 0))
    for h in range(2):
        m_new, alpha = rescale(h, mt_ref[0, h])
        acc_ref[h] = alpha * acc_ref[h] + pv(vtj[h * HEAD_DIM:(h + 1) * HEAD_DIM, :], sd_ref[h], m_new)

    steps = jnp.maximum(n_chunks - 1, 0)

    def body(i, carry):
        c = 2 * i
        step((c, 0), (c + 1, 1))
        step((c + 1, 1), (c + 2, 0))
        return carry

    lax.fori_loop(0, lax.shift_right_logical(steps, 1), body, 0)

    @pl.when((n_chunks > 0) & (steps % 2 == 1))
    def _():
        step((n_chunks - 2, 0), (n_chunks - 1, 1))
        step((n_chunks - 1, 1), None)

    @pl.when((n_chunks > 0) & (steps % 2 == 0))
    def _():
        step((n_chunks - 1, 0), None)

    outs = [acc_ref[h, 0:HEAD_DIM, :] / acc_ref[h, HEAD_DIM:HEAD_DIM + 1, :] for h in range(2)]
    o_ref[...] = jnp.concatenate(outs, axis=0).T.astype(BF16)


def _moba(q, k, vt, kmean):
    b, s, _ = q.shape
    nb = s // MOBA_BLOCK
    n_pairs = D_ATTN // LANES
    return pl.pallas_call(
        _moba_kernel,
        grid=(b, n_pairs, nb),
        in_specs=[
            pl.BlockSpec((None, MOBA_BLOCK, LANES), lambda i, p, j: (i, j, p)),
            pl.BlockSpec((None, s, LANES), lambda i, p, j: (i, 0, p)),
            pl.BlockSpec((None, LANES, s), lambda i, p, j: (i, p, 0)),
            pl.BlockSpec((None, nb, LANES), lambda i, p, j: (i, 0, p)),
        ],
        out_specs=pl.BlockSpec((None, MOBA_BLOCK, LANES), lambda i, p, j: (i, j, p)),
        out_shape=jax.ShapeDtypeStruct((b, s, D_ATTN), BF16),
        scratch_shapes=[
            pltpu.VMEM((2, nb, MOBA_BLOCK), F32),
            pltpu.VMEM((2, MOBA_BLOCK, MOBA_BLOCK), F32),
            pltpu.VMEM((2, 2, KV_CHUNK_BLOCKS * MOBA_BLOCK, MOBA_BLOCK), F32),
            pltpu.VMEM((3, 2, 1, MOBA_BLOCK), F32),
            pltpu.VMEM((2, 1, MOBA_BLOCK), F32),
            pltpu.VMEM((2, ACC_ROWS, MOBA_BLOCK), F32),
        ],
        compiler_params=pltpu.CompilerParams(
            dimension_semantics=("arbitrary", "arbitrary", "arbitrary"),
            vmem_limit_bytes=VMEM_LIMIT),
        name="moba",
    )(q, k, vt, kmean)


GROUP_LANE0 = N_EXPERTS


def _mix_route_kernel(yc_ref, ya_ref, x_ref, wo_ref, g_ref, b_ref, wr_ref, br_ref,
                      x1_ref, er_ref, rw_ref, cnt_ref, base_ref):
    i = pl.program_id(0)
    t = x_ref.shape[0]

    @pl.when(i == 0)
    def _():
        base_ref[...] = jnp.zeros_like(base_ref)

    mix = _dot(yc_ref[...], wo_ref[0:D_CONV, :]) + _dot(ya_ref[...], wo_ref[D_CONV:, :])
    x1 = _layer_norm(DEEPNORM_ALPHA * x_ref[...] + mix, g_ref[...], b_ref[...])
    x1_ref[...] = x1

    x_hi = x1.astype(BF16)
    x_lo = (x1 - x_hi.astype(F32)).astype(BF16)
    wr = wr_ref[...]
    w_hi = wr.astype(BF16)
    w_lo = (wr - w_hi.astype(F32)).astype(BF16)
    logits = _dot(x_hi, w_hi) + _dot(x_lo, w_hi) + _dot(x_hi, w_lo) + br_ref[...]

    lane = lax.broadcasted_iota(jnp.int32, (t, LANES), 1).astype(F32)
    big = float(LANES)

    lg = jnp.where((lane >= GROUP_LANE0) & (lane < GROUP_LANE0 + N_GROUPS), logits, NEG_INF)
    gmax = jnp.max(lg, axis=-1, keepdims=True)
    g_p = 1.0 / jnp.sum(jnp.exp(lg - gmax), axis=-1, keepdims=True)
    g_idx = jnp.min(jnp.where(lg == gmax, lane, big), axis=-1, keepdims=True) - GROUP_LANE0

    lo = g_idx * EXPERTS_PER_GROUP
    le = jnp.where((lane >= lo) & (lane < lo + EXPERTS_PER_GROUP), logits, NEG_INF)
    m1 = jnp.max(le, axis=-1, keepdims=True)
    i1 = jnp.min(jnp.where(le == m1, lane, big), axis=-1, keepdims=True)
    le2 = jnp.where(lane == i1, NEG_INF, le)
    m2 = jnp.max(le2, axis=-1, keepdims=True)
    i2 = jnp.min(jnp.where(le2 == m2, lane, big), axis=-1, keepdims=True)
    p2 = jnp.exp(m2 - m1)
    den = 1.0 + p2
    w1 = g_p * (1.0 / den)
    w2 = g_p * (p2 / den)

    oh = (lane == i1) | (lane == i2)
    ohf = oh.astype(F32)
    r_i = lax.broadcasted_iota(jnp.int32, (t, t), 0)
    c_i = lax.broadcasted_iota(jnp.int32, (t, t), 1)
    tri = (r_i > c_i).astype(BF16)
    tot = base_ref[...] + _dot(tri, ohf.astype(BF16))
    r1 = jnp.sum(jnp.where(lane == i1, tot, 0.0), axis=-1, keepdims=True)
    r2 = jnp.sum(jnp.where(lane == i2, tot, 0.0), axis=-1, keepdims=True)
    base = base_ref[...] + jnp.sum(ohf, axis=0, keepdims=True)
    base_ref[...] = base
    cnt_ref[...] = jnp.broadcast_to(base, cnt_ref.shape)

    er = jnp.where(lane == 0, i1, jnp.where(lane == 1, i2, jnp.where(lane == 2, r1, r2)))
    er_ref[...] = er.T[0:8, :].astype(jnp.int32)
    rw_ref[...] = jnp.where(lane == 0, w1, w2)


def _mix_route(yc, ya, x2d, w_out_bf, g, b, w_r, b_r):
    n = x2d.shape[0]
    t = TOKEN_TILE
    row = lambda w: pl.BlockSpec((t, w), lambda i: (i, 0))
    full = lambda a: pl.BlockSpec(a.shape, lambda i: (0, 0))
    return pl.pallas_call(
        _mix_route_kernel,
        grid=(n // t,),
        in_specs=[row(D_CONV), row(D_ATTN), row(D_MODEL), full(w_out_bf), full(g), full(b),
                  full(w_r), full(b_r)],
        out_specs=[row(D_MODEL), pl.BlockSpec((8, t), lambda i: (0, i)), row(LANES),
                   pl.BlockSpec((8, LANES), lambda i: (0, 0))],
        out_shape=[
            jax.ShapeDtypeStruct((n, D_MODEL), F32),
            jax.ShapeDtypeStruct((8, n), jnp.int32),
            jax.ShapeDtypeStruct((n, LANES), F32),
            jax.ShapeDtypeStruct((8, LANES), F32),
        ],
        scratch_shapes=[pltpu.VMEM((1, LANES), F32)],
        compiler_params=pltpu.CompilerParams(
            dimension_semantics=("arbitrary",), vmem_limit_bytes=VMEM_LIMIT),
        name="mix_route",
    )(yc, ya, x2d, w_out_bf, g, b, w_r, b_r)


def _row_copies_start(n_rows, src_of, dst_of, sem):
    for r in range(n_rows):
        pltpu.make_async_copy(src_of(r), dst_of(r), sem).start()


def _row_copies_wait(src_rows, dst_rows, sem):
    pltpu.make_async_copy(src_rows, dst_rows, sem).wait()


def _dispatch_kernel(z0_ref, z1_ref, start_ref, er_ref, x1_ref, buf_hbm, zrow, sem, zsem):
    i = pl.program_id(0)
    t = x1_ref.shape[0]

    @pl.when(i == 0)
    def _():
        zrow[...] = jnp.zeros_like(zrow)

        def per_range(e, c):
            def zero_row(k, c2):
                pltpu.make_async_copy(zrow.at[pl.ds(0, 1)], buf_hbm.at[pl.ds(k, 1)], zsem).start()
                return c2
            lax.fori_loop(z0_ref[e], z1_ref[e], zero_row, 0)

            def wait_row(k, c2):
                pltpu.make_async_copy(zrow.at[pl.ds(0, 1)], buf_hbm.at[pl.ds(0, 1)], zsem).wait()
                return c2
            lax.fori_loop(z0_ref[e], z1_ref[e], wait_row, 0)
            return c
        lax.fori_loop(0, N_EXPERTS + 1, per_range, 0)

    for k in range(2):
        _row_copies_start(
            t, lambda q: x1_ref.at[pl.ds(q, 1)],
            lambda q, k=k: buf_hbm.at[pl.ds(start_ref[er_ref[k, q]] + er_ref[2 + k, q], 1)], sem)
    for k in range(2):
        _row_copies_wait(x1_ref, buf_hbm.at[pl.ds(0, t)], sem)


def _dispatch(zero_from, zero_to, row_start, er, x1, n_rows):
    n = x1.shape[0]
    t = TOKEN_TILE
    grid_spec = pltpu.PrefetchScalarGridSpec(
        num_scalar_prefetch=3,
        grid=(n // t,),
        in_specs=[
            pl.BlockSpec((8, t), lambda i, *_: (0, i), memory_space=pltpu.SMEM),
            pl.BlockSpec((t, D_MODEL), lambda i, *_: (i, 0)),
        ],
        out_specs=pl.BlockSpec(memory_space=pl.ANY),
        scratch_shapes=[pltpu.VMEM((8, D_MODEL), F32), pltpu.SemaphoreType.DMA, pltpu.SemaphoreType.DMA],
    )
    return pl.pallas_call(
        _dispatch_kernel,
        grid_spec=grid_spec,
        out_shape=jax.ShapeDtypeStruct((n_rows, D_MODEL), F32),
        compiler_params=pltpu.CompilerParams(
            dimension_semantics=("arbitrary",), vmem_limit_bytes=VMEM_LIMIT),
        name="dispatch",
    )(zero_from, zero_to, row_start, er, x1)


def _experts_kernel(be_ref, nu_ref, xb_ref, wg_ref, wu_ref, wd_ref, yb_ref):
    used = pl.program_id(0) < nu_ref[0]

    @pl.when(used)
    def _():
        xb = xb_ref[...].astype(BF16)
        gate = _dot(xb, wg_ref[...].astype(BF16))
        up = _dot(xb, wu_ref[...].astype(BF16))
        hid = (jax.nn.silu(gate) * up).astype(BF16)
        yb_ref[...] = _dot(hid, wd_ref[...].astype(BF16))

    @pl.when(jnp.logical_not(used))
    def _():
        yb_ref[...] = jnp.zeros_like(yb_ref)


def _experts(blk_expert, n_used, buf, w_gate, w_up, w_down):
    n_blk = blk_expert.shape[0]
    r = DISPATCH_BLOCK
    grid_spec = pltpu.PrefetchScalarGridSpec(
        num_scalar_prefetch=2,
        grid=(n_blk,),
        in_specs=[
            pl.BlockSpec((r, D_MODEL), lambda i, be, nu: (i, 0)),
            pl.BlockSpec((None, D_MODEL, D_EXPERT), lambda i, be, nu: (be[i], 0, 0)),
            pl.BlockSpec((None, D_MODEL, D_EXPERT), lambda i, be, nu: (be[i], 0, 0)),
            pl.BlockSpec((None, D_EXPERT, D_MODEL), lambda i, be, nu: (be[i], 0, 0)),
        ],
        out_specs=pl.BlockSpec((r, D_MODEL), lambda i, be, nu: (i, 0)),
    )
    return pl.pallas_call(
        _experts_kernel,
        grid_spec=grid_spec,
        out_shape=jax.ShapeDtypeStruct(buf.shape, F32),
        compiler_params=pltpu.CompilerParams(
            dimension_semantics=("arbitrary",), vmem_limit_bytes=VMEM_LIMIT),
        name="experts",
    )(blk_expert, n_used, buf, w_gate, w_up, w_down)


def _combine_kernel(start_ref, cur_ref, nxt_ref, x1_ref, p_ref, rw_ref, wpg_ref, wpp_ref, g_ref,
                    b_ref, yb_hbm, o_ref, gbuf, sem):
    i = pl.program_id(0)
    n_steps = pl.num_programs(0)
    t = x1_ref.shape[0]
    slot = i % 2

    def gather(idx_ref, s):
        for k in range(2):
            _row_copies_start(
                t, lambda q, k=k: yb_hbm.at[pl.ds(start_ref[idx_ref[k, q]] + idx_ref[2 + k, q], 1)],
                lambda q, k=k: gbuf.at[s, pl.ds(k * t + q, 1)], sem.at[s])

    @pl.when(i == 0)
    def _():
        gather(cur_ref, 0)

    gather(nxt_ref, 1 - slot)

    x1 = x1_ref[...]
    ple = jax.nn.sigmoid(_dot(x1.astype(BF16), wpg_ref[...])) * _dot(p_ref[...].astype(BF16), wpp_ref[...])

    _row_copies_wait(yb_hbm.at[pl.ds(0, 2 * t)], gbuf.at[slot], sem.at[slot])
    rw = rw_ref[...]
    ffn = gbuf[slot, 0:t, :] * rw[:, 0:1] + gbuf[slot, t:2 * t, :] * rw[:, 1:2]
    o_ref[...] = _layer_norm(DEEPNORM_ALPHA * x1 + ffn + ple, g_ref[...], b_ref[...])

    @pl.when(i == n_steps - 1)
    def _():
        _row_copies_wait(yb_hbm.at[pl.ds(0, 2 * t)], gbuf.at[1 - slot], sem.at[1 - slot])


def _combine(row_start, er, x1, p2d, rw, w_pg_bf, w_pp_bf, g, b, yb):
    n = x1.shape[0]
    t = TOKEN_TILE
    n_steps = n // t
    smem_blk = lambda f: pl.BlockSpec((8, t), f, memory_space=pltpu.SMEM)
    row = lambda w: pl.BlockSpec((t, w), lambda i, st: (i, 0))
    full = lambda a: pl.BlockSpec(a.shape, lambda i, st: (0, 0))
    grid_spec = pltpu.PrefetchScalarGridSpec(
        num_scalar_prefetch=1,
        grid=(n_steps,),
        in_specs=[
            smem_blk(lambda i, st: (0, i)),
            smem_blk(lambda i, st: (0, jnp.minimum(i + 1, n_steps - 1))),
            row(D_MODEL), row(PLE_DIM), row(LANES), full(w_pg_bf), full(w_pp_bf), full(g), full(b),
            pl.BlockSpec(memory_space=pl.ANY),
        ],
        out_specs=row(D_MODEL),
        scratch_shapes=[pltpu.VMEM((2, 2 * t, D_MODEL), F32), pltpu.SemaphoreType.DMA((2,))],
    )
    return pl.pallas_call(
        _combine_kernel,
        grid_spec=grid_spec,
        out_shape=jax.ShapeDtypeStruct((n, D_MODEL), F32),
        compiler_params=pltpu.CompilerParams(
            dimension_semantics=("arbitrary",), vmem_limit_bytes=VMEM_LIMIT),
        name="combine",
    )(row_start, er, er, x1, p2d, rw, w_pg_bf, w_pp_bf, g, b, yb)


def _layer(x, p, w_in, w_conv, w_out, ln1_g, ln1_b, w_router_g, b_router_g, w_router_e,
           b_router_e, w_gate, w_up, w_down, w_ple_gate, w_ple_proj, ln2_g, ln2_b):
    b, s, d = x.shape
    n = b * s
    r = DISPATCH_BLOCK

    yc, q, k, vt, kmean = _proj_conv(x, w_in.astype(BF16), w_conv)
    ya = _moba(q, k, vt, kmean.reshape(b, s // MOBA_BLOCK, D_ATTN))

    w_r = jnp.zeros((d, LANES), F32).at[:, :N_EXPERTS].set(w_router_e)
    w_r = w_r.at[:, GROUP_LANE0:GROUP_LANE0 + N_GROUPS].set(w_router_g)
    b_r = jnp.zeros((1, LANES), F32).at[0, :N_EXPERTS].set(b_router_e)
    b_r = b_r.at[0, GROUP_LANE0:GROUP_LANE0 + N_GROUPS].set(b_router_g)
    x1, er, rw, cnt = _mix_route(
        yc.reshape(n, D_CONV), ya.reshape(n, D_ATTN), x.reshape(n, d), w_out.astype(BF16),
        ln1_g.reshape(1, d), ln1_b.reshape(1, d), w_r, b_r)

    counts = cnt[0, :N_EXPERTS].astype(jnp.int32)
    padded = (counts + r - 1) // r * r
    row_end = jnp.cumsum(padded)
    row_start = row_end - padded
    n_blk = -(-(2 * n) // r) + N_EXPERTS
    n_rows = n_blk * r
    zero_from = jnp.concatenate([row_start + counts, row_end[-1:]])
    zero_to = jnp.concatenate([row_end, jnp.full((1,), n_rows, jnp.int32)])
    blk_row0 = jnp.arange(n_blk, dtype=jnp.int32) * r
    blk_expert = jnp.minimum(
        jnp.sum((row_end[None, :] <= blk_row0[:, None]).astype(jnp.int32), axis=1), N_EXPERTS - 1)
    n_used = row_end[-1:] // r

    buf = _dispatch(zero_from, zero_to, row_start, er, x1, n_rows)
    yb = _experts(blk_expert, n_used, buf, w_gate, w_up, w_down)
    out = _combine(row_start, er, x1, p.reshape(n, PLE_DIM), rw, w_ple_gate.astype(BF16),
                   w_ple_proj.astype(BF16), ln2_g.reshape(1, d), ln2_b.reshape(1, d), yb)
    return out.reshape(b, s, d)


def kernel(x, p, w_in, w_conv, w_out, ln1_g, ln1_b, w_router_g, b_router_g, w_router_e,
           b_router_e, w_gate, w_up, w_down, w_ple_gate, w_ple_proj, ln2_g, ln2_b):
    for i in range(DEPTH):
        x = _layer(x, p[i], w_in[i], w_conv[i], w_out[i], ln1_g[i], ln1_b[i], w_router_g[i],
                   b_router_g[i], w_router_e[i], b_router_e[i], w_gate[i], w_up[i], w_down[i],
                   w_ple_gate[i], w_ple_proj[i], ln2_g[i], ln2_b[i])
    return x
```

```python
import functools

import jax
import jax.numpy as jnp
from jax import lax
from jax.experimental import pallas as pl
from jax.experimental.pallas import tpu as pltpu

D_MODEL = 1024
D_CONV = 512
N_HEADS = 8
HEAD_DIM = 64
D_ATTN = N_HEADS * HEAD_DIM
D_IN = 3 * D_CONV + 3 * D_ATTN
MOBA_BLOCK = 256
MOBA_TOPK = 3
N_GROUPS = 4
EXPERTS_PER_GROUP = 8
N_EXPERTS = N_GROUPS * EXPERTS_PER_GROUP
D_EXPERT = 512
DISPATCH_BLOCK = 256
PLE_DIM = 256
LN_EPS = 1e-5
DEPTH = 1
DEEPNORM_ALPHA = (2 * DEPTH) ** 0.25

LANES = 128
PROJ_ROWS = 512
TOKEN_TILE = 256
PLE_SLABS = 4
KV_CHUNK_SHIFT = 2
KV_CHUNK_BLOCKS = 1 << KV_CHUNK_SHIFT
MOBA_HEADS_PER_STEP = 4
BF16_SUBLANES = 16
ACC_ROWS = HEAD_DIM + BF16_SUBLANES
Q_SCALE = HEAD_DIM ** -0.5 * 1.4426950408889634
VMEM_LIMIT = 48 * 1024 * 1024

F32 = jnp.float32
BF16 = jnp.bfloat16
NEG_INF = float("-inf")


def _dot(a, b):
    return jnp.dot(a, b, preferred_element_type=F32)


def _dot_nt(a, b):
    return lax.dot_general(a, b, (((1,), (1,)), ((), ())), preferred_element_type=F32)


def _layer_norm(h, g, b):
    mu = jnp.mean(h, axis=-1, keepdims=True)
    d = h - mu
    var = jnp.mean(d * d, axis=-1, keepdims=True)
    return d * lax.rsqrt(var + LN_EPS) * g + b


def _proj_conv_kernel(x_ref, w_ref, wc_ref, yc_ref, q_ref, k_ref, vt_ref, km_ref, ubuf):
    s = pl.program_id(1)
    rows = x_ref.shape[0]
    xb = x_ref[...].astype(BF16)

    def sect(i):
        return _dot(xb, w_ref[:, i * D_CONV:(i + 1) * D_CONV])

    @pl.when(s == 0)
    def _():
        ubuf[0:8, :] = jnp.zeros((8, D_CONV), F32)

    u = sect(1) * sect(2)
    ubuf[8:8 + rows, :] = u
    wc = wc_ref[...]
    conv = wc[0:1, :] * ubuf[6:6 + rows, :] + wc[1:2, :] * ubuf[7:7 + rows, :] + wc[2:3, :] * u
    yc_ref[...] = (sect(0) * conv).astype(BF16)
    ubuf[0:8, :] = ubuf[rows:rows + 8, :]

    q_ref[...] = (sect(3) * Q_SCALE).astype(BF16)
    k = sect(4)
    k_ref[...] = k.astype(BF16)
    for i in range(rows // MOBA_BLOCK):
        km_ref[i] = jnp.mean(k[i * MOBA_BLOCK:(i + 1) * MOBA_BLOCK, :], axis=0, keepdims=True)
    vt_ref[...] = sect(5).T.astype(BF16)


def _proj_conv(x, w_in_bf, w_conv):
    b, s, _ = x.shape
    nb = s // MOBA_BLOCK
    grid = (b, s // PROJ_ROWS)
    seq_spec = pl.BlockSpec((None, PROJ_ROWS, D_CONV), lambda i, j: (i, j, 0))
    return pl.pallas_call(
        _proj_conv_kernel,
        grid=grid,
        in_specs=[
            pl.BlockSpec((None, PROJ_ROWS, D_MODEL), lambda i, j: (i, j, 0)),
            pl.BlockSpec((D_MODEL, D_IN), lambda i, j: (0, 0)),
            pl.BlockSpec((3, D_CONV), lambda i, j: (0, 0)),
        ],
        out_specs=[
            seq_spec, seq_spec, seq_spec,
            pl.BlockSpec((None, D_ATTN, PROJ_ROWS), lambda i, j: (i, 0, j)),
            pl.BlockSpec((None, PROJ_ROWS // MOBA_BLOCK, 1, D_ATTN), lambda i, j: (i, j, 0, 0)),
        ],
        out_shape=[
            jax.ShapeDtypeStruct((b, s, D_CONV), BF16),
            jax.ShapeDtypeStruct((b, s, D_ATTN), BF16),
            jax.ShapeDtypeStruct((b, s, D_ATTN), BF16),
            jax.ShapeDtypeStruct((b, D_ATTN, s), BF16),
            jax.ShapeDtypeStruct((b, nb, 1, D_ATTN), F32),
        ],
        scratch_shapes=[pltpu.VMEM((PROJ_ROWS + 8, D_CONV), F32)],
        compiler_params=pltpu.CompilerParams(
            dimension_semantics=("arbitrary", "arbitrary"), vmem_limit_bytes=VMEM_LIMIT),
        name="proj_conv",
    )(x, w_in_bf, w_conv)


def _moba_kernel(q_ref, k_ref, vt_ref, km_ref, o_ref, bias_ref, sd_ref, s_ref, mt_ref, m_ref,
                 acc_ref):
    j = pl.program_id(2)
    blk = MOBA_BLOCK
    nb = km_ref.shape[0]
    n_heads = vt_ref.shape[0] // HEAD_DIM
    heads = range(n_heads)
    n_chunks = lax.shift_right_logical(j + (KV_CHUNK_BLOCKS - 1), KV_CHUNK_SHIFT)

    def pair_lanes(h):
        return slice((h // 2) * LANES, (h // 2 + 1) * LANES)

    row = lax.broadcasted_iota(jnp.int32, (LANES, blk), 0)
    qt_pairs = [q_ref[:, pair_lanes(2 * pr)].astype(F32).T for pr in range(n_heads // 2)]
    qt_heads = [jnp.where((row >= HEAD_DIM) == bool(h % 2), qt_pairs[h // 2], 0.0).astype(BF16)
                for h in heads]

    bidx = lax.broadcasted_iota(jnp.int32, (nb, blk), 0)
    for h in heads:
        km = km_ref[:, pair_lanes(h)]
        km_hi = km.astype(BF16)
        km_lo = (km - km_hi.astype(F32)).astype(BF16)
        g = _dot(km_hi, qt_heads[h]) + _dot(km_lo, qt_heads[h])
        g = jnp.where(bidx < j, g, NEG_INF)
        sel = jnp.zeros((nb, blk), jnp.bool_)
        for _ in range(MOBA_TOPK):
            mx = jnp.max(g, axis=0, keepdims=True)
            cand = jnp.where((g == mx) & (mx > NEG_INF), bidx, nb)
            pick = bidx == jnp.min(cand, axis=0, keepdims=True)
            sel = sel | pick
            g = jnp.where(pick, NEG_INF, g)
        bias_ref[h] = jnp.where(sel, 0.0, NEG_INF)
        m_ref[h] = jnp.full((1, blk), NEG_INF, F32)
        acc_ref[h] = jnp.zeros(acc_ref.shape[1:], F32)

    def k_block(n, h):
        return k_ref[pl.ds(pl.multiple_of(n * blk, blk), blk), pair_lanes(h)]

    def vt_block(n, h):
        return vt_ref[h * HEAD_DIM:(h + 1) * HEAD_DIM, pl.ds(pl.multiple_of(n * blk, blk), blk)]

    def stage_block(c, u, slot):
        n = c * KV_CHUNK_BLOCKS + u
        mts = []
        for h in heads:
            sc = _dot(k_block(n, h), qt_heads[h]) + bias_ref[h, pl.ds(n, 1), :]
            s_ref[slot, h, u * blk:(u + 1) * blk, :] = sc
            mts.append(jnp.max(sc, axis=0, keepdims=True))
        return mts

    def pv(vt_h, sc, m_new):
        p = jnp.exp2(sc - m_new).astype(BF16)
        lhs = jnp.concatenate([vt_h, jnp.ones((ACC_ROWS - HEAD_DIM, vt_h.shape[1]), BF16)], axis=0)
        return _dot(lhs, p)

    def rescale(h, mt):
        m_old = m_ref[h]
        m_new = jnp.maximum(m_old, mt)
        m_ref[h] = m_new
        return m_new, jnp.exp2(m_old - m_new)

    def step(consume, stage):
        if consume is not None:
            cc, cs = consume
            scale = [rescale(h, mt_ref[1 + cs, h]) for h in heads]
            parts = [None] * n_heads
        mts = []
        for u in range(KV_CHUNK_BLOCKS):
            if stage is not None:
                mts.append(stage_block(stage[0], u, stage[1]))
            if consume is not None:
                for h in heads:
                    r = pv(vt_block(cc * KV_CHUNK_BLOCKS + u, h),
                           s_ref[cs, h, u * blk:(u + 1) * blk, :], scale[h][0])
                    parts[h] = r if parts[h] is None else parts[h] + r
        for h in heads:
            if consume is not None:
                acc_ref[h] = scale[h][1] * acc_ref[h] + parts[h]
            if stage is not None:
                mt_ref[1 + stage[1], h] = functools.reduce(jnp.maximum, [m[h] for m in mts])

    kpos = lax.broadcasted_iota(jnp.int32, (blk, blk), 0)
    qpos = lax.broadcasted_iota(jnp.int32, (blk, blk), 1)
    causal = kpos <= qpos
    for h in heads:
        sc = jnp.where(causal, _dot(k_block(j, h), qt_heads[h]), NEG_INF)
        sd_ref[h] = sc
        mt_ref[0, h] = jnp.max(sc, axis=0, keepdims=True)
    step(None, (0, 0))
    for h in heads:
        m_new, alpha = rescale(h, mt_ref[0, h])
        acc_ref[h] = alpha * acc_ref[h] + pv(vt_block(j, h), sd_ref[h], m_new)

    steps = jnp.maximum(n_chunks - 1, 0)

    def body(i, carry):
        c = 2 * i
        step((c, 0), (c + 1, 1))
        step((c + 1, 1), (c + 2, 0))
        return carry

    lax.fori_loop(0, lax.shift_right_logical(steps, 1), body, 0)

    @pl.when((n_chunks > 0) & (steps % 2 == 1))
    def _():
        step((n_chunks - 2, 0), (n_chunks - 1, 1))
        step((n_chunks - 1, 1), None)

    @pl.when((n_chunks > 0) & (steps % 2 == 0))
    def _():
        step((n_chunks - 1, 0), None)

    outs = [acc_ref[h, 0:HEAD_DIM, :] / acc_ref[h, HEAD_DIM:HEAD_DIM + 1, :] for h in heads]
    o_ref[...] = jnp.concatenate(outs, axis=0).T.astype(BF16)


def _moba(q, k, vt, kmean):
    b, s, _ = q.shape
    nb = s // MOBA_BLOCK
    w = MOBA_HEADS_PER_STEP * HEAD_DIM
    hs = MOBA_HEADS_PER_STEP
    return pl.pallas_call(
        _moba_kernel,
        grid=(b, D_ATTN // w, nb),
        in_specs=[
            pl.BlockSpec((None, MOBA_BLOCK, w), lambda i, p, j: (i, j, p)),
            pl.BlockSpec((None, s, w), lambda i, p, j: (i, 0, p)),
            pl.BlockSpec((None, w, s), lambda i, p, j: (i, p, 0)),
            pl.BlockSpec((None, nb, w), lambda i, p, j: (i, 0, p)),
        ],
        out_specs=pl.BlockSpec((None, MOBA_BLOCK, w), lambda i, p, j: (i, j, p)),
        out_shape=jax.ShapeDtypeStruct((b, s, D_ATTN), BF16),
        scratch_shapes=[
            pltpu.VMEM((hs, nb, MOBA_BLOCK), F32),
            pltpu.VMEM((hs, MOBA_BLOCK, MOBA_BLOCK), F32),
            pltpu.VMEM((2, hs, KV_CHUNK_BLOCKS * MOBA_BLOCK, MOBA_BLOCK), F32),
            pltpu.VMEM((3, hs, 1, MOBA_BLOCK), F32),
            pltpu.VMEM((hs, 1, MOBA_BLOCK), F32),
            pltpu.VMEM((hs, ACC_ROWS, MOBA_BLOCK), F32),
        ],
        compiler_params=pltpu.CompilerParams(
            dimension_semantics=("arbitrary", "arbitrary", "arbitrary"),
            vmem_limit_bytes=VMEM_LIMIT),
        name="moba",
    )(q, k, vt, kmean)


GROUP_LANE0 = N_EXPERTS


def _mix_route_kernel(yc_ref, ya_ref, x_ref, wo_ref, g_ref, b_ref, wr_ref, br_ref,
                      x1_ref, er_ref, rw_ref, cnt_ref, base_ref):
    i = pl.program_id(0)
    t = x_ref.shape[0]

    @pl.when(i == 0)
    def _():
        base_ref[...] = jnp.zeros_like(base_ref)

    mix = _dot(yc_ref[...], wo_ref[0:D_CONV, :]) + _dot(ya_ref[...], wo_ref[D_CONV:, :])
    x1 = _layer_norm(DEEPNORM_ALPHA * x_ref[...] + mix, g_ref[...], b_ref[...])
    x1_ref[...] = x1

    x_hi = x1.astype(BF16)
    x_lo = (x1 - x_hi.astype(F32)).astype(BF16)
    wr = wr_ref[...]
    w_hi = wr.astype(BF16)
    w_lo = (wr - w_hi.astype(F32)).astype(BF16)
    logits = _dot(x_hi, w_hi) + _dot(x_lo, w_hi) + _dot(x_hi, w_lo) + br_ref[...]

    lane = lax.broadcasted_iota(jnp.int32, (t, LANES), 1).astype(F32)
    big = float(LANES)

    lg = jnp.where((lane >= GROUP_LANE0) & (lane < GROUP_LANE0 + N_GROUPS), logits, NEG_INF)
    gmax = jnp.max(lg, axis=-1, keepdims=True)
    g_p = 1.0 / jnp.sum(jnp.exp(lg - gmax), axis=-1, keepdims=True)
    g_idx = jnp.min(jnp.where(lg == gmax, lane, big), axis=-1, keepdims=True) - GROUP_LANE0

    lo = g_idx * EXPERTS_PER_GROUP
    le = jnp.where((lane >= lo) & (lane < lo + EXPERTS_PER_GROUP), logits, NEG_INF)
    m1 = jnp.max(le, axis=-1, keepdims=True)
    i1 = jnp.min(jnp.where(le == m1, lane, big), axis=-1, keepdims=True)
    le2 = jnp.where(lane == i1, NEG_INF, le)
    m2 = jnp.max(le2, axis=-1, keepdims=True)
    i2 = jnp.min(jnp.where(le2 == m2, lane, big), axis=-1, keepdims=True)
    p2 = jnp.exp(m2 - m1)
    den = 1.0 + p2
    w1 = g_p * (1.0 / den)
    w2 = g_p * (p2 / den)

    oh = (lane == i1) | (lane == i2)
    ohf = oh.astype(F32)
    r_i = lax.broadcasted_iota(jnp.int32, (t, t), 0)
    c_i = lax.broadcasted_iota(jnp.int32, (t, t), 1)
    tri = (r_i > c_i).astype(BF16)
    tot = base_ref[...] + _dot(tri, ohf.astype(BF16))
    r1 = jnp.sum(jnp.where(lane == i1, tot, 0.0), axis=-1, keepdims=True)
    r2 = jnp.sum(jnp.where(lane == i2, tot, 0.0), axis=-1, keepdims=True)
    base = base_ref[...] + jnp.sum(ohf, axis=0, keepdims=True)
    base_ref[...] = base
    cnt_ref[...] = jnp.broadcast_to(base, cnt_ref.shape)

    er = jnp.where(lane == 0, i1, jnp.where(lane == 1, i2, jnp.where(lane == 2, r1, r2)))
    er_ref[...] = er.T[0:8, :].astype(jnp.int32)
    rw_ref[...] = jnp.where(lane == 0, w1, w2)


def _mix_route(yc, ya, x2d, w_out_bf, g, b, w_r, b_r):
    n = x2d.shape[0]
    t = TOKEN_TILE
    row = lambda w: pl.BlockSpec((t, w), lambda i: (i, 0))
    full = lambda a: pl.BlockSpec(a.shape, lambda i: (0, 0))
    return pl.pallas_call(
        _mix_route_kernel,
        grid=(n // t,),
        in_specs=[row(D_CONV), row(D_ATTN), row(D_MODEL), full(w_out_bf), full(g), full(b),
                  full(w_r), full(b_r)],
        out_specs=[row(D_MODEL), pl.BlockSpec((8, t), lambda i: (0, i)), row(LANES),
                   pl.BlockSpec((8, LANES), lambda i: (0, 0))],
        out_shape=[
            jax.ShapeDtypeStruct((n, D_MODEL), F32),
            jax.ShapeDtypeStruct((8, n), jnp.int32),
            jax.ShapeDtypeStruct((n, LANES), F32),
            jax.ShapeDtypeStruct((8, LANES), F32),
        ],
        scratch_shapes=[pltpu.VMEM((1, LANES), F32)],
        compiler_params=pltpu.CompilerParams(
            dimension_semantics=("arbitrary",), vmem_limit_bytes=VMEM_LIMIT),
        name="mix_route",
    )(yc, ya, x2d, w_out_bf, g, b, w_r, b_r)


def _row_copies_start(rows, src_of, dst_of, sem):
    for r in rows:
        pltpu.make_async_copy(src_of(r), dst_of(r), sem).start(priority=r % 2)


def _row_copies_wait(src_rows, dst_rows, sem):
    pltpu.make_async_copy(src_rows, dst_rows, sem).wait()


def _dispatch_kernel(z0_ref, z1_ref, start_ref, er_ref, x1_ref, buf_hbm, zrow, sem, zsem):
    i = pl.program_id(0)
    t = x1_ref.shape[0]

    @pl.when(i == 0)
    def _():
        zrow[...] = jnp.zeros_like(zrow)

        def per_range(e, c):
            def zero_row(k, c2):
                pltpu.make_async_copy(zrow.at[pl.ds(0, 1)], buf_hbm.at[pl.ds(k, 1)], zsem).start()
                return c2
            lax.fori_loop(z0_ref[e], z1_ref[e], zero_row, 0)

            def wait_row(k, c2):
                pltpu.make_async_copy(zrow.at[pl.ds(0, 1)], buf_hbm.at[pl.ds(0, 1)], zsem).wait()
                return c2
            lax.fori_loop(z0_ref[e], z1_ref[e], wait_row, 0)
            return c
        lax.fori_loop(0, N_EXPERTS + 1, per_range, 0)

    for k in range(2):
        _row_copies_start(
            range(t), lambda q: x1_ref.at[pl.ds(q, 1)],
            lambda q, k=k: buf_hbm.at[pl.ds(start_ref[er_ref[k, q]] + er_ref[2 + k, q], 1)], sem)
    for k in range(2):
        _row_copies_wait(x1_ref, buf_hbm.at[pl.ds(0, t)], sem)


def _dispatch(zero_from, zero_to, row_start, er, x1, n_rows):
    n = x1.shape[0]
    t = TOKEN_TILE
    grid_spec = pltpu.PrefetchScalarGridSpec(
        num_scalar_prefetch=3,
        grid=(n // t,),
        in_specs=[
            pl.BlockSpec((8, t), lambda i, *_: (0, i), memory_space=pltpu.SMEM),
            pl.BlockSpec((t, D_MODEL), lambda i, *_: (i, 0)),
        ],
        out_specs=pl.BlockSpec(memory_space=pl.ANY),
        scratch_shapes=[pltpu.VMEM((8, D_MODEL), F32), pltpu.SemaphoreType.DMA, pltpu.SemaphoreType.DMA],
    )
    return pl.pallas_call(
        _dispatch_kernel,
        grid_spec=grid_spec,
        out_shape=jax.ShapeDtypeStruct((n_rows, D_MODEL), F32),
        compiler_params=pltpu.CompilerParams(
            dimension_semantics=("arbitrary",), vmem_limit_bytes=VMEM_LIMIT),
        name="dispatch",
    )(zero_from, zero_to, row_start, er, x1)


def _experts_kernel(be_ref, first_ref, nxt_ref, par_ref, nu_ref, xb_ref, wg_hbm, wu_hbm, wd_hbm,
                    yb_ref, wg_buf, wu_buf, wd_buf, sem):
    i = pl.program_id(0)
    used = i < nu_ref[0]
    slot = par_ref[i]
    weights = ((wg_hbm, wg_buf), (wu_hbm, wu_buf), (wd_hbm, wd_buf))

    def fetch(e, s):
        for k, (w_hbm, w_buf) in enumerate(weights):
            pltpu.make_async_copy(w_hbm.at[e], w_buf.at[s], sem.at[s, k]).start()

    def fetch_wait(s):
        for k, (w_hbm, w_buf) in enumerate(weights):
            pltpu.make_async_copy(w_hbm.at[0], w_buf.at[s], sem.at[s, k]).wait()

    @pl.when(i == 0)
    def _():
        fetch(be_ref[0], slot)

    @pl.when(used & (first_ref[i] == 1))
    def _():
        fetch_wait(slot)

        @pl.when(nxt_ref[i] >= 0)
        def _():
            fetch(nxt_ref[i], 1 - slot)

    @pl.when(used)
    def _():
        xb = xb_ref[...].astype(BF16)
        gate = _dot(xb, wg_buf[slot].astype(BF16))
        up = _dot(xb, wu_buf[slot].astype(BF16))
        hid = (jax.nn.silu(gate) * up).astype(BF16)
        yb_ref[...] = _dot(hid, wd_buf[slot].astype(BF16))

    @pl.when(jnp.logical_not(used))
    def _():
        yb_ref[...] = jnp.zeros_like(yb_ref)


def _experts(blk_expert, blk_first, blk_next, blk_slot, n_used, buf, w_gate, w_up, w_down):
    n_blk = blk_expert.shape[0]
    r = DISPATCH_BLOCK
    grid_spec = pltpu.PrefetchScalarGridSpec(
        num_scalar_prefetch=5,
        grid=(n_blk,),
        in_specs=[
            pl.BlockSpec((r, D_MODEL), lambda i, *_: (i, 0)),
            pl.BlockSpec(memory_space=pl.ANY),
            pl.BlockSpec(memory_space=pl.ANY),
            pl.BlockSpec(memory_space=pl.ANY),
        ],
        out_specs=pl.BlockSpec((r, D_MODEL), lambda i, *_: (i, 0)),
        scratch_shapes=[
            pltpu.VMEM((2,) + w_gate.shape[1:], w_gate.dtype),
            pltpu.VMEM((2,) + w_up.shape[1:], w_up.dtype),
            pltpu.VMEM((2,) + w_down.shape[1:], w_down.dtype),
            pltpu.SemaphoreType.DMA((2, 3)),
        ],
    )
    return pl.pallas_call(
        _experts_kernel,
        grid_spec=grid_spec,
        out_shape=jax.ShapeDtypeStruct(buf.shape, F32),
        compiler_params=pltpu.CompilerParams(
            dimension_semantics=("arbitrary",), vmem_limit_bytes=VMEM_LIMIT),
        name="experts",
    )(blk_expert, blk_first, blk_next, blk_slot, n_used, buf, w_gate, w_up, w_down)


def _combine_kernel(start_ref, cur_ref, nxt_ref, x1_ref, p_ref, rw_ref, wpg_ref, wpp_ref, g_ref,
                    b_ref, yb_hbm, o_ref, gbuf, sem):
    i = pl.program_id(0)
    n_steps = pl.num_programs(0)
    t = x1_ref.shape[0]
    slot = i % 2

    def gather(idx_ref, s, rows):
        for k in range(2):
            _row_copies_start(
                rows, lambda q, k=k: yb_hbm.at[pl.ds(start_ref[idx_ref[k, q]] + idx_ref[2 + k, q], 1)],
                lambda q, k=k: gbuf.at[s, pl.ds(k * t + q, 1)], sem.at[s])

    @pl.when(i == 0)
    def _():
        gather(cur_ref, 0, range(t))

    x1 = x1_ref[...]
    x1b = x1.astype(BF16)
    pb = p_ref[...].astype(BF16)
    d = x1.shape[1]
    slabs = []
    for c in range(PLE_SLABS):
        gather(nxt_ref, 1 - slot, range(c * t // PLE_SLABS, (c + 1) * t // PLE_SLABS))
        cols = slice(c * d // PLE_SLABS, (c + 1) * d // PLE_SLABS)
        slabs.append(jax.nn.sigmoid(_dot(x1b, wpg_ref[:, cols])) * _dot(pb, wpp_ref[:, cols]))
    ple = jnp.concatenate(slabs, axis=1)

    _row_copies_wait(yb_hbm.at[pl.ds(0, 2 * t)], gbuf.at[slot], sem.at[slot])
    rw = rw_ref[...]
    ffn = gbuf[slot, 0:t, :] * rw[:, 0:1] + gbuf[slot, t:2 * t, :] * rw[:, 1:2]
    o_ref[...] = _layer_norm(DEEPNORM_ALPHA * x1 + ffn + ple, g_ref[...], b_ref[...])

    @pl.when(i == n_steps - 1)
    def _():
        _row_copies_wait(yb_hbm.at[pl.ds(0, 2 * t)], gbuf.at[1 - slot], sem.at[1 - slot])


def _combine(row_start, er, x1, p2d, rw, w_pg_bf, w_pp_bf, g, b, yb):
    n = x1.shape[0]
    t = TOKEN_TILE
    n_steps = n // t
    smem_blk = lambda f: pl.BlockSpec((8, t), f, memory_space=pltpu.SMEM)
    row = lambda w: pl.BlockSpec((t, w), lambda i, st: (i, 0))
    full = lambda a: pl.BlockSpec(a.shape, lambda i, st: (0, 0))
    grid_spec = pltpu.PrefetchScalarGridSpec(
        num_scalar_prefetch=1,
        grid=(n_steps,),
        in_specs=[
            smem_blk(lambda i, st: (0, i)),
            smem_blk(lambda i, st: (0, jnp.minimum(i + 1, n_steps - 1))),
            row(D_MODEL), row(PLE_DIM), row(LANES), full(w_pg_bf), full(w_pp_bf), full(g), full(b),
            pl.BlockSpec(memory_space=pl.ANY),
        ],
        out_specs=row(D_MODEL),
        scratch_shapes=[pltpu.VMEM((2, 2 * t, D_MODEL), F32), pltpu.SemaphoreType.DMA((2,))],
    )
    return pl.pallas_call(
        _combine_kernel,
        grid_spec=grid_spec,
        out_shape=jax.ShapeDtypeStruct((n, D_MODEL), F32),
        compiler_params=pltpu.CompilerParams(
            dimension_semantics=("arbitrary",), vmem_limit_bytes=VMEM_LIMIT),
        name="combine",
    )(row_start, er, er, x1, p2d, rw, w_pg_bf, w_pp_bf, g, b, yb)


def _layer(x, p, w_in, w_conv, w_out, ln1_g, ln1_b, w_router_g, b_router_g, w_router_e,
           b_router_e, w_gate, w_up, w_down, w_ple_gate, w_ple_proj, ln2_g, ln2_b):
    b, s, d = x.shape
    n = b * s
    r = DISPATCH_BLOCK

    yc, q, k, vt, kmean = _proj_conv(x, w_in.astype(BF16), w_conv)
    ya = _moba(q, k, vt, kmean.reshape(b, s // MOBA_BLOCK, D_ATTN))

    w_r = jnp.zeros((d, LANES), F32).at[:, :N_EXPERTS].set(w_router_e)
    w_r = w_r.at[:, GROUP_LANE0:GROUP_LANE0 + N_GROUPS].set(w_router_g)
    b_r = jnp.zeros((1, LANES), F32).at[0, :N_EXPERTS].set(b_router_e)
    b_r = b_r.at[0, GROUP_LANE0:GROUP_LANE0 + N_GROUPS].set(b_router_g)
    x1, er, rw, cnt = _mix_route(
        yc.reshape(n, D_CONV), ya.reshape(n, D_ATTN), x.reshape(n, d), w_out.astype(BF16),
        ln1_g.reshape(1, d), ln1_b.reshape(1, d), w_r, b_r)

    counts = cnt[0, :N_EXPERTS].astype(jnp.int32)
    padded = (counts + r - 1) // r * r
    row_end = jnp.cumsum(padded)
    row_start = row_end - padded
    n_blk = -(-(2 * n) // r) + N_EXPERTS
    n_rows = n_blk * r
    zero_from = jnp.concatenate([row_start + counts, row_end[-1:]])
    zero_to = jnp.concatenate([row_end, jnp.full((1,), n_rows, jnp.int32)])
    blk_row0 = jnp.arange(n_blk, dtype=jnp.int32) * r
    blk_expert = jnp.minimum(
        jnp.sum((row_end[None, :] <= blk_row0[:, None]).astype(jnp.int32), axis=1), N_EXPERTS - 1)
    n_used = row_end[-1:] // r
    eid = jnp.arange(N_EXPERTS, dtype=jnp.int32)
    nonempty = counts > 0
    later = jnp.where((eid[None, :] > eid[:, None]) & nonempty[None, :], eid[None, :], N_EXPERTS)
    next_expert = jnp.min(later, axis=1)
    next_expert = jnp.where(next_expert == N_EXPERTS, -1, next_expert)
    ordinal = jnp.cumsum(nonempty.astype(jnp.int32)) - 1
    blk_first = jnp.concatenate(
        [jnp.ones((1,), jnp.int32), (blk_expert[1:] != blk_expert[:-1]).astype(jnp.int32)])
    blk_next = next_expert[blk_expert]
    blk_slot = ordinal[blk_expert] % 2

    buf = _dispatch(zero_from, zero_to, row_start, er, x1, n_rows)
    yb = _experts(blk_expert, blk_first, blk_next, blk_slot, n_used, buf, w_gate, w_up, w_down)
    out = _combine(row_start, er, x1, p.reshape(n, PLE_DIM), rw, w_ple_gate.astype(BF16),
                   w_ple_proj.astype(BF16), ln2_g.reshape(1, d), ln2_b.reshape(1, d), yb)
    return out.reshape(b, s, d)


def kernel(x, p, w_in, w_conv, w_out, ln1_g, ln1_b, w_router_g, b_router_g, w_router_e,
           b_router_e, w_gate, w_up, w_down, w_ple_gate, w_ple_proj, ln2_g, ln2_b):
    for i in range(DEPTH):
        x = _layer(x, p[i], w_in[i], w_conv[i], w_out[i], ln1_g[i], ln1_b[i], w_router_g[i],
                   b_router_g[i], w_router_e[i], b_router_e[i], w_gate[i], w_up[i], w_down[i],
                   w_ple_gate[i], w_ple_proj[i], ln2_g[i], ln2_b[i])
    return x
```

```python
import functools

import jax
import jax.numpy as jnp
from jax import lax
from jax.experimental import pallas as pl
from jax.experimental.pallas import tpu as pltpu

D_MODEL = 1024
D_CONV = 512
N_HEADS = 8
HEAD_DIM = 64
D_ATTN = N_HEADS * HEAD_DIM
D_IN = 3 * D_CONV + 3 * D_ATTN
MOBA_BLOCK = 256
MOBA_TOPK = 3
N_GROUPS = 4
EXPERTS_PER_GROUP = 8
N_EXPERTS = N_GROUPS * EXPERTS_PER_GROUP
D_EXPERT = 512
DISPATCH_BLOCK = 256
ROW_GROUP = 8
PLE_DIM = 256
LN_EPS = 1e-5
DEPTH = 1
DEEPNORM_ALPHA = (2 * DEPTH) ** 0.25

LANES = 128
PROJ_ROWS = 512
TOKEN_TILE = 256
SORTED_TILE_ROWS = 2 * TOKEN_TILE + ROW_GROUP * N_EXPERTS
PLE_SLABS = 4
KV_CHUNK_SHIFT = 2
KV_CHUNK_BLOCKS = 1 << KV_CHUNK_SHIFT
MOBA_HEADS_PER_STEP = 4
BF16_SUBLANES = 16
ACC_ROWS = HEAD_DIM + BF16_SUBLANES
Q_SCALE = HEAD_DIM ** -0.5 * 1.4426950408889634
VMEM_LIMIT = 48 * 1024 * 1024

F32 = jnp.float32
BF16 = jnp.bfloat16
NEG_INF = float("-inf")


def _dot(a, b):
    return jnp.dot(a, b, preferred_element_type=F32)


def _dot_nt(a, b):
    return lax.dot_general(a, b, (((1,), (1,)), ((), ())), preferred_element_type=F32)


def _layer_norm(h, g, b):
    mu = jnp.mean(h, axis=-1, keepdims=True)
    d = h - mu
    var = jnp.mean(d * d, axis=-1, keepdims=True)
    return d * lax.rsqrt(var + LN_EPS) * g + b


def _proj_conv_kernel(x_ref, w_ref, wc_ref, yc_ref, q_ref, k_ref, vt_ref, km_ref, ubuf):
    s = pl.program_id(1)
    rows = x_ref.shape[0]
    xb = x_ref[...].astype(BF16)

    def sect(i):
        return _dot(xb, w_ref[:, i * D_CONV:(i + 1) * D_CONV])

    @pl.when(s == 0)
    def _():
        ubuf[0:8, :] = jnp.zeros((8, D_CONV), F32)

    u = sect(1) * sect(2)
    ubuf[8:8 + rows, :] = u
    wc = wc_ref[...]
    conv = wc[0:1, :] * ubuf[6:6 + rows, :] + wc[1:2, :] * ubuf[7:7 + rows, :] + wc[2:3, :] * u
    yc_ref[...] = (sect(0) * conv).astype(BF16)
    ubuf[0:8, :] = ubuf[rows:rows + 8, :]

    q_ref[...] = (sect(3) * Q_SCALE).astype(BF16)
    k = sect(4)
    k_ref[...] = k.astype(BF16)
    for i in range(rows // MOBA_BLOCK):
        km_ref[i] = jnp.mean(k[i * MOBA_BLOCK:(i + 1) * MOBA_BLOCK, :], axis=0, keepdims=True)
    vt_ref[...] = sect(5).T.astype(BF16)


def _proj_conv(x, w_in_bf, w_conv):
    b, s, _ = x.shape
    nb = s // MOBA_BLOCK
    grid = (b, s // PROJ_ROWS)
    seq_spec = pl.BlockSpec((None, PROJ_ROWS, D_CONV), lambda i, j: (i, j, 0))
    return pl.pallas_call(
        _proj_conv_kernel,
        grid=grid,
        in_specs=[
            pl.BlockSpec((None, PROJ_ROWS, D_MODEL), lambda i, j: (i, j, 0)),
            pl.BlockSpec((D_MODEL, D_IN), lambda i, j: (0, 0)),
            pl.BlockSpec((3, D_CONV), lambda i, j: (0, 0)),
        ],
        out_specs=[
            seq_spec, seq_spec, seq_spec,
            pl.BlockSpec((None, D_ATTN, PROJ_ROWS), lambda i, j: (i, 0, j)),
            pl.BlockSpec((None, PROJ_ROWS // MOBA_BLOCK, 1, D_ATTN), lambda i, j: (i, j, 0, 0)),
        ],
        out_shape=[
            jax.ShapeDtypeStruct((b, s, D_CONV), BF16),
            jax.ShapeDtypeStruct((b, s, D_ATTN), BF16),
            jax.ShapeDtypeStruct((b, s, D_ATTN), BF16),
            jax.ShapeDtypeStruct((b, D_ATTN, s), BF16),
            jax.ShapeDtypeStruct((b, nb, 1, D_ATTN), F32),
        ],
        scratch_shapes=[pltpu.VMEM((PROJ_ROWS + 8, D_CONV), F32)],
        compiler_params=pltpu.CompilerParams(
            dimension_semantics=("arbitrary", "arbitrary"), vmem_limit_bytes=VMEM_LIMIT),
        name="proj_conv",
    )(x, w_in_bf, w_conv)


def _moba_kernel(q_ref, k_ref, vt_ref, km_ref, o_ref, bias_ref, sd_ref, s_ref, mt_ref, m_ref,
                 acc_ref):
    j = pl.program_id(2)
    blk = MOBA_BLOCK
    nb = km_ref.shape[0]
    n_heads = vt_ref.shape[0] // HEAD_DIM
    heads = range(n_heads)
    n_chunks = lax.shift_right_logical(j + (KV_CHUNK_BLOCKS - 1), KV_CHUNK_SHIFT)

    def pair_lanes(h):
        return slice((h // 2) * LANES, (h // 2 + 1) * LANES)

    row = lax.broadcasted_iota(jnp.int32, (LANES, blk), 0)
    qt_pairs = [q_ref[:, pair_lanes(2 * pr)].astype(F32).T for pr in range(n_heads // 2)]
    qt_heads = [jnp.where((row >= HEAD_DIM) == bool(h % 2), qt_pairs[h // 2], 0.0).astype(BF16)
                for h in heads]

    bidx = lax.broadcasted_iota(jnp.int32, (nb, blk), 0)
    for h in heads:
        km = km_ref[:, pair_lanes(h)]
        km_hi = km.astype(BF16)
        km_lo = (km - km_hi.astype(F32)).astype(BF16)
        g = _dot(km_hi, qt_heads[h]) + _dot(km_lo, qt_heads[h])
        g = jnp.where(bidx < j, g, NEG_INF)
        sel = jnp.zeros((nb, blk), jnp.bool_)
        for _ in range(MOBA_TOPK):
            mx = jnp.max(g, axis=0, keepdims=True)
            cand = jnp.where((g == mx) & (mx > NEG_INF), bidx, nb)
            pick = bidx == jnp.min(cand, axis=0, keepdims=True)
            sel = sel | pick
            g = jnp.where(pick, NEG_INF, g)
        bias_ref[h] = jnp.where(sel, 0.0, NEG_INF)
        m_ref[h] = jnp.full((1, blk), NEG_INF, F32)
        acc_ref[h] = jnp.zeros(acc_ref.shape[1:], F32)

    def k_block(n, h):
        return k_ref[pl.ds(pl.multiple_of(n * blk, blk), blk), pair_lanes(h)]

    def vt_block(n, h):
        return vt_ref[h * HEAD_DIM:(h + 1) * HEAD_DIM, pl.ds(pl.multiple_of(n * blk, blk), blk)]

    def stage_block(c, u, slot):
        n = c * KV_CHUNK_BLOCKS + u
        mts = []
        for h in heads:
            sc = _dot(k_block(n, h), qt_heads[h]) + bias_ref[h, pl.ds(n, 1), :]
            s_ref[slot, h, u * blk:(u + 1) * blk, :] = sc
            mts.append(jnp.max(sc, axis=0, keepdims=True))
        return mts

    def pv(vt_h, sc, m_new):
        p = jnp.exp2(sc - m_new).astype(BF16)
        lhs = jnp.concatenate([vt_h, jnp.ones((ACC_ROWS - HEAD_DIM, vt_h.shape[1]), BF16)], axis=0)
        return _dot(lhs, p)

    def rescale(h, mt):
        m_old = m_ref[h]
        m_new = jnp.maximum(m_old, mt)
        m_ref[h] = m_new
        return m_new, jnp.exp2(m_old - m_new)

    def step(consume, stage):
        if consume is not None:
            cc, cs = consume
            scale = [rescale(h, mt_ref[1 + cs, h]) for h in heads]
            parts = [None] * n_heads
        mts = []
        for u in range(KV_CHUNK_BLOCKS):
            if stage is not None:
                mts.append(stage_block(stage[0], u, stage[1]))
            if consume is not None:
                for h in heads:
                    r = pv(vt_block(cc * KV_CHUNK_BLOCKS + u, h),
                           s_ref[cs, h, u * blk:(u + 1) * blk, :], scale[h][0])
                    parts[h] = r if parts[h] is None else parts[h] + r
        for h in heads:
            if consume is not None:
                acc_ref[h] = scale[h][1] * acc_ref[h] + parts[h]
            if stage is not None:
                mt_ref[1 + stage[1], h] = functools.reduce(jnp.maximum, [m[h] for m in mts])

    kpos = lax.broadcasted_iota(jnp.int32, (blk, blk), 0)
    qpos = lax.broadcasted_iota(jnp.int32, (blk, blk), 1)
    causal = kpos <= qpos
    for h in heads:
        sc = jnp.where(causal, _dot(k_block(j, h), qt_heads[h]), NEG_INF)
        sd_ref[h] = sc
        mt_ref[0, h] = jnp.max(sc, axis=0, keepdims=True)
    step(None, (0, 0))
    for h in heads:
        m_new, alpha = rescale(h, mt_ref[0, h])
        acc_ref[h] = alpha * acc_ref[h] + pv(vt_block(j, h), sd_ref[h], m_new)

    steps = jnp.maximum(n_chunks - 1, 0)

    def body(i, carry):
        c = 2 * i
        step((c, 0), (c + 1, 1))
        step((c + 1, 1), (c + 2, 0))
        return carry

    lax.fori_loop(0, lax.shift_right_logical(steps, 1), body, 0)

    @pl.when((n_chunks > 0) & (steps % 2 == 1))
    def _():
        step((n_chunks - 2, 0), (n_chunks - 1, 1))
        step((n_chunks - 1, 1), None)

    @pl.when((n_chunks > 0) & (steps % 2 == 0))
    def _():
        step((n_chunks - 1, 0), None)

    outs = [acc_ref[h, 0:HEAD_DIM, :] / acc_ref[h, HEAD_DIM:HEAD_DIM + 1, :] for h in heads]
    o_ref[...] = jnp.concatenate(outs, axis=0).T.astype(BF16)


def _moba(q, k, vt, kmean):
    b, s, _ = q.shape
    nb = s // MOBA_BLOCK
    w = MOBA_HEADS_PER_STEP * HEAD_DIM
    hs = MOBA_HEADS_PER_STEP
    return pl.pallas_call(
        _moba_kernel,
        grid=(b, D_ATTN // w, nb),
        in_specs=[
            pl.BlockSpec((None, MOBA_BLOCK, w), lambda i, p, j: (i, j, p)),
            pl.BlockSpec((None, s, w), lambda i, p, j: (i, 0, p)),
            pl.BlockSpec((None, w, s), lambda i, p, j: (i, p, 0)),
            pl.BlockSpec((None, nb, w), lambda i, p, j: (i, 0, p)),
        ],
        out_specs=pl.BlockSpec((None, MOBA_BLOCK, w), lambda i, p, j: (i, j, p)),
        out_shape=jax.ShapeDtypeStruct((b, s, D_ATTN), BF16),
        scratch_shapes=[
            pltpu.VMEM((hs, nb, MOBA_BLOCK), F32),
            pltpu.VMEM((hs, MOBA_BLOCK, MOBA_BLOCK), F32),
            pltpu.VMEM((2, hs, KV_CHUNK_BLOCKS * MOBA_BLOCK, MOBA_BLOCK), F32),
            pltpu.VMEM((3, hs, 1, MOBA_BLOCK), F32),
            pltpu.VMEM((hs, 1, MOBA_BLOCK), F32),
            pltpu.VMEM((hs, ACC_ROWS, MOBA_BLOCK), F32),
        ],
        compiler_params=pltpu.CompilerParams(
            dimension_semantics=("arbitrary", "arbitrary", "arbitrary"),
            vmem_limit_bytes=VMEM_LIMIT),
        name="moba",
    )(q, k, vt, kmean)


GROUP_LANE0 = N_EXPERTS


def _mix_route_kernel(yc_ref, ya_ref, x_ref, wo_ref, g_ref, b_ref, wr_ref, br_ref,
                      x1_ref, xs_ref, er_ref, rw_ref, cnt_ref):
    t = x_ref.shape[0]

    mix = _dot(yc_ref[...], wo_ref[0:D_CONV, :]) + _dot(ya_ref[...], wo_ref[D_CONV:, :])
    x1 = _layer_norm(DEEPNORM_ALPHA * x_ref[...] + mix, g_ref[...], b_ref[...])
    x1_ref[...] = x1

    x_hi = x1.astype(BF16)
    x_lo = (x1 - x_hi.astype(F32)).astype(BF16)
    wr = wr_ref[...]
    w_hi = wr.astype(BF16)
    w_lo = (wr - w_hi.astype(F32)).astype(BF16)
    logits = _dot(x_hi, w_hi) + _dot(x_lo, w_hi) + _dot(x_hi, w_lo) + br_ref[...]

    lane = lax.broadcasted_iota(jnp.int32, (t, LANES), 1).astype(F32)
    big = float(LANES)

    lg = jnp.where((lane >= GROUP_LANE0) & (lane < GROUP_LANE0 + N_GROUPS), logits, NEG_INF)
    gmax = jnp.max(lg, axis=-1, keepdims=True)
    g_p = 1.0 / jnp.sum(jnp.exp(lg - gmax), axis=-1, keepdims=True)
    g_idx = jnp.min(jnp.where(lg == gmax, lane, big), axis=-1, keepdims=True) - GROUP_LANE0

    lo = g_idx * EXPERTS_PER_GROUP
    le = jnp.where((lane >= lo) & (lane < lo + EXPERTS_PER_GROUP), logits, NEG_INF)
    m1 = jnp.max(le, axis=-1, keepdims=True)
    i1 = jnp.min(jnp.where(le == m1, lane, big), axis=-1, keepdims=True)
    le2 = jnp.where(lane == i1, NEG_INF, le)
    m2 = jnp.max(le2, axis=-1, keepdims=True)
    i2 = jnp.min(jnp.where(le2 == m2, lane, big), axis=-1, keepdims=True)
    p2 = jnp.exp(m2 - m1)
    den = 1.0 + p2
    w1 = g_p * (1.0 / den)
    w2 = g_p * (p2 / den)

    oh = (lane == i1) | (lane == i2)
    ohf = oh.astype(F32)
    r_i = lax.broadcasted_iota(jnp.int32, (t, t), 0)
    c_i = lax.broadcasted_iota(jnp.int32, (t, t), 1)
    before = _dot((r_i > c_i).astype(BF16), ohf.astype(BF16))
    r1 = jnp.sum(jnp.where(lane == i1, before, 0.0), axis=-1, keepdims=True)
    r2 = jnp.sum(jnp.where(lane == i2, before, 0.0), axis=-1, keepdims=True)

    cnt = jnp.sum(ohf, axis=0, keepdims=True)
    seg = jnp.floor((cnt + (ROW_GROUP - 1)) * (1.0 / ROW_GROUP)) * ROW_GROUP
    e_a = lax.broadcasted_iota(jnp.int32, (LANES, LANES), 0)
    e_b = lax.broadcasted_iota(jnp.int32, (LANES, LANES), 1)
    seg_start = _dot(jnp.broadcast_to(seg, (8, LANES)).astype(BF16), (e_a < e_b).astype(BF16))[0:1, :]
    pos1 = jnp.sum(jnp.where(lane == i1, seg_start, 0.0), axis=-1, keepdims=True) + r1
    pos2 = jnp.sum(jnp.where(lane == i2, seg_start, 0.0), axis=-1, keepdims=True) + r2
    cnt_ref[...] = jnp.broadcast_to(cnt, cnt_ref.shape)

    cols = jnp.where(lane == 0, i1, jnp.where(lane == 1, i2, jnp.where(lane == 2, r1, jnp.where(
        lane == 3, r2, jnp.where(lane == 4, pos1, pos2)))))
    rows = cols.T[0:8, :]
    er_ref[...] = rows.astype(jnp.int32)
    rw_ref[...] = jnp.where(lane == 0, w1, w2)

    srow = lax.broadcasted_iota(jnp.int32, (xs_ref.shape[0], t), 0).astype(F32)
    perm = ((srow == rows[4:5, :]) | (srow == rows[5:6, :])).astype(BF16)
    xs_ref[...] = _dot(perm, x_hi)


def _mix_route(yc, ya, x2d, w_out_bf, g, b, w_r, b_r):
    n = x2d.shape[0]
    t = TOKEN_TILE
    nt = n // t
    row = lambda w: pl.BlockSpec((t, w), lambda i: (i, 0))
    full = lambda a: pl.BlockSpec(a.shape, lambda i: (0, 0))
    return pl.pallas_call(
        _mix_route_kernel,
        grid=(nt,),
        in_specs=[row(D_CONV), row(D_ATTN), row(D_MODEL), full(w_out_bf), full(g), full(b),
                  full(w_r), full(b_r)],
        out_specs=[row(D_MODEL), pl.BlockSpec((SORTED_TILE_ROWS, D_MODEL), lambda i: (i, 0)),
                   pl.BlockSpec((8, t), lambda i: (0, i)), row(LANES),
                   pl.BlockSpec((8, LANES), lambda i: (i, 0))],
        out_shape=[
            jax.ShapeDtypeStruct((n, D_MODEL), F32),
            jax.ShapeDtypeStruct((nt * SORTED_TILE_ROWS, D_MODEL), F32),
            jax.ShapeDtypeStruct((8, n), jnp.int32),
            jax.ShapeDtypeStruct((n, LANES), F32),
            jax.ShapeDtypeStruct((nt * 8, LANES), F32),
        ],
        compiler_params=pltpu.CompilerParams(
            dimension_semantics=("arbitrary",), vmem_limit_bytes=VMEM_LIMIT),
        name="mix_route",
    )(yc, ya, x2d, w_out_bf, g, b, w_r, b_r)


def _row_copies_start(rows, src_of, dst_of, sem):
    for r in rows:
        pltpu.make_async_copy(src_of(r), dst_of(r), sem).start()


def _row_copies_wait(src_rows, dst_rows, sem):
    pltpu.make_async_copy(src_rows, dst_rows, sem).wait()


def _experts_kernel(be_ref, first_ref, nxt_ref, par_ref, nu_ref, cur_ref, nxg_ref, xs_hbm,
                    wg_hbm, wu_hbm, wd_hbm, yb_ref, xbuf, wg_buf, wu_buf, wd_buf, xsem, wsem):
    i = pl.program_id(0)
    used = i < nu_ref[0]
    slot = par_ref[i]
    xslot = i % 2
    g = ROW_GROUP
    weights = ((wg_hbm, wg_buf), (wu_hbm, wu_buf), (wd_hbm, wd_buf))

    def fetch(e, s):
        for k, (w_hbm, w_buf) in enumerate(weights):
            pltpu.make_async_copy(w_hbm.at[e], w_buf.at[s], wsem.at[s, k]).start()

    def fetch_wait(s):
        for k, (w_hbm, w_buf) in enumerate(weights):
            pltpu.make_async_copy(w_hbm.at[0], w_buf.at[s], wsem.at[s, k]).wait()

    def gather_rows(grp_ref, s):
        _row_copies_start(
            range(DISPATCH_BLOCK // g),
            lambda q: xs_hbm.at[pl.ds(pl.multiple_of(grp_ref[0, q], g), g)],
            lambda q: xbuf.at[s, pl.ds(q * g, g)], xsem.at[s])

    @pl.when(i == 0)
    def _():
        fetch(be_ref[0], slot)
        gather_rows(cur_ref, 0)

    gather_rows(nxg_ref, 1 - xslot)

    @pl.when(used & (first_ref[i] == 1))
    def _():
        fetch_wait(slot)

        @pl.when(nxt_ref[i] >= 0)
        def _():
            fetch(nxt_ref[i], 1 - slot)

    _row_copies_wait(xs_hbm.at[pl.ds(0, DISPATCH_BLOCK)], xbuf.at[xslot], xsem.at[xslot])

    @pl.when(used)
    def _():
        xb = xbuf[xslot].astype(BF16)
        gate = _dot(xb, wg_buf[slot].astype(BF16))
        up = _dot(xb, wu_buf[slot].astype(BF16))
        hid = (jax.nn.silu(gate) * up).astype(BF16)
        yb_ref[...] = _dot(hid, wd_buf[slot].astype(BF16))

    @pl.when(jnp.logical_not(used))
    def _():
        yb_ref[...] = jnp.zeros_like(yb_ref)

    @pl.when(i == pl.num_programs(0) - 1)
    def _():
        _row_copies_wait(xs_hbm.at[pl.ds(0, DISPATCH_BLOCK)], xbuf.at[1 - xslot], xsem.at[1 - xslot])


def _experts(blk_expert, blk_first, blk_next, blk_slot, n_used, grp_row, xs, w_gate, w_up, w_down):
    n_blk = blk_expert.shape[0]
    r = DISPATCH_BLOCK
    gpb = r // ROW_GROUP
    grp3 = grp_row.reshape(n_blk, 1, gpb)
    smem_blk = lambda f: pl.BlockSpec((None, 1, gpb), f, memory_space=pltpu.SMEM)
    grid_spec = pltpu.PrefetchScalarGridSpec(
        num_scalar_prefetch=5,
        grid=(n_blk,),
        in_specs=[
            smem_blk(lambda i, *_: (i, 0, 0)),
            smem_blk(lambda i, *_: (jnp.minimum(i + 1, n_blk - 1), 0, 0)),
            pl.BlockSpec(memory_space=pl.ANY),
            pl.BlockSpec(memory_space=pl.ANY),
            pl.BlockSpec(memory_space=pl.ANY),
            pl.BlockSpec(memory_space=pl.ANY),
        ],
        out_specs=pl.BlockSpec((r, D_MODEL), lambda i, *_: (i, 0)),
        scratch_shapes=[
            pltpu.VMEM((2, r, D_MODEL), F32),
            pltpu.VMEM((2,) + w_gate.shape[1:], w_gate.dtype),
            pltpu.VMEM((2,) + w_up.shape[1:], w_up.dtype),
            pltpu.VMEM((2,) + w_down.shape[1:], w_down.dtype),
            pltpu.SemaphoreType.DMA((2,)),
            pltpu.SemaphoreType.DMA((2, 3)),
        ],
    )
    return pl.pallas_call(
        _experts_kernel,
        grid_spec=grid_spec,
        out_shape=jax.ShapeDtypeStruct((n_blk * r, D_MODEL), F32),
        compiler_params=pltpu.CompilerParams(
            dimension_semantics=("arbitrary",), vmem_limit_bytes=VMEM_LIMIT),
        name="experts",
    )(blk_expert, blk_first, blk_next, blk_slot, n_used, grp3, grp3, xs, w_gate, w_up, w_down)


def _combine_kernel(seg_ref, cur_ref, nxt_ref, x1_ref, p_ref, rw_ref, wpg_ref, wpp_ref, g_ref,
                    b_ref, yb_hbm, o_ref, gbuf, sem):
    i = pl.program_id(0)
    n_steps = pl.num_programs(0)
    t = x1_ref.shape[0]
    slot = i % 2

    def gather(idx_ref, tile, s, rows):
        for k in range(2):
            _row_copies_start(
                rows,
                lambda q, k=k: yb_hbm.at[
                    pl.ds(seg_ref[tile * N_EXPERTS + idx_ref[k, q]] + idx_ref[2 + k, q], 1)],
                lambda q, k=k: gbuf.at[s, pl.ds(k * t + q, 1)], sem.at[s])

    @pl.when(i == 0)
    def _():
        gather(cur_ref, 0, 0, range(t))

    x1 = x1_ref[...]
    x1b = x1.astype(BF16)
    pb = p_ref[...].astype(BF16)
    d = x1.shape[1]
    slabs = []
    for c in range(PLE_SLABS):
        gather(nxt_ref, jnp.minimum(i + 1, n_steps - 1), 1 - slot,
               range(c * t // PLE_SLABS, (c + 1) * t // PLE_SLABS))
        cols = slice(c * d // PLE_SLABS, (c + 1) * d // PLE_SLABS)
        slabs.append(jax.nn.sigmoid(_dot(x1b, wpg_ref[:, cols])) * _dot(pb, wpp_ref[:, cols]))
    ple = jnp.concatenate(slabs, axis=1)

    _row_copies_wait(yb_hbm.at[pl.ds(0, 2 * t)], gbuf.at[slot], sem.at[slot])
    rw = rw_ref[...]
    ffn = gbuf[slot, 0:t, :] * rw[:, 0:1] + gbuf[slot, t:2 * t, :] * rw[:, 1:2]
    o_ref[...] = _layer_norm(DEEPNORM_ALPHA * x1 + ffn + ple, g_ref[...], b_ref[...])

    @pl.when(i == n_steps - 1)
    def _():
        _row_copies_wait(yb_hbm.at[pl.ds(0, 2 * t)], gbuf.at[1 - slot], sem.at[1 - slot])


def _combine(seg_row, er, x1, p2d, rw, w_pg_bf, w_pp_bf, g, b, yb):
    n = x1.shape[0]
    t = TOKEN_TILE
    n_steps = n // t
    smem_blk = lambda f: pl.BlockSpec((8, t), f, memory_space=pltpu.SMEM)
    row = lambda w: pl.BlockSpec((t, w), lambda i, st: (i, 0))
    full = lambda a: pl.BlockSpec(a.shape, lambda i, st: (0, 0))
    grid_spec = pltpu.PrefetchScalarGridSpec(
        num_scalar_prefetch=1,
        grid=(n_steps,),
        in_specs=[
            smem_blk(lambda i, st: (0, i)),
            smem_blk(lambda i, st: (0, jnp.minimum(i + 1, n_steps - 1))),
            row(D_MODEL), row(PLE_DIM), row(LANES), full(w_pg_bf), full(w_pp_bf), full(g), full(b),
            pl.BlockSpec(memory_space=pl.ANY),
        ],
        out_specs=row(D_MODEL),
        scratch_shapes=[pltpu.VMEM((2, 2 * t, D_MODEL), F32), pltpu.SemaphoreType.DMA((2,))],
    )
    return pl.pallas_call(
        _combine_kernel,
        grid_spec=grid_spec,
        out_shape=jax.ShapeDtypeStruct((n, D_MODEL), F32),
        compiler_params=pltpu.CompilerParams(
            dimension_semantics=("arbitrary",), vmem_limit_bytes=VMEM_LIMIT),
        name="combine",
    )(seg_row, er, er, x1, p2d, rw, w_pg_bf, w_pp_bf, g, b, yb)


def _layer(x, p, w_in, w_conv, w_out, ln1_g, ln1_b, w_router_g, b_router_g, w_router_e,
           b_router_e, w_gate, w_up, w_down, w_ple_gate, w_ple_proj, ln2_g, ln2_b):
    b, s, d = x.shape
    n = b * s
    r = DISPATCH_BLOCK

    yc, q, k, vt, kmean = _proj_conv(x, w_in.astype(BF16), w_conv)
    ya = _moba(q, k, vt, kmean.reshape(b, s // MOBA_BLOCK, D_ATTN))

    w_r = jnp.zeros((d, LANES), F32).at[:, :N_EXPERTS].set(w_router_e)
    w_r = w_r.at[:, GROUP_LANE0:GROUP_LANE0 + N_GROUPS].set(w_router_g)
    b_r = jnp.zeros((1, LANES), F32).at[0, :N_EXPERTS].set(b_router_e)
    b_r = b_r.at[0, GROUP_LANE0:GROUP_LANE0 + N_GROUPS].set(b_router_g)
    x1, xs, er, rw, cnt = _mix_route(
        yc.reshape(n, D_CONV), ya.reshape(n, D_ATTN), x.reshape(n, d), w_out.astype(BF16),
        ln1_g.reshape(1, d), ln1_b.reshape(1, d), w_r, b_r)

    nt = n // TOKEN_TILE
    g = ROW_GROUP
    seg = (cnt.reshape(nt, 8, LANES)[:, 0, :N_EXPERTS].astype(jnp.int32) + g - 1) // g * g
    rows_e = jnp.sum(seg, axis=0)
    padded = (rows_e + r - 1) // r * r
    row_end = jnp.cumsum(padded)
    row_start = row_end - padded
    seg_dst = row_start[None, :] + jnp.cumsum(seg, axis=0) - seg
    seg_src = (jnp.arange(nt, dtype=jnp.int32) * SORTED_TILE_ROWS)[:, None] + jnp.cumsum(seg, axis=1) - seg
    n_blk = -(-(2 * n + (g - 1) * N_EXPERTS * nt) // r) + N_EXPERTS
    grp_first = jnp.arange(n_blk * (r // g), dtype=jnp.int32) * g
    dst_f, src_f, len_f = seg_dst.T.reshape(-1), seg_src.T.reshape(-1), seg.T.reshape(-1)
    inside = (grp_first[:, None] >= dst_f[None, :]) & (grp_first[:, None] < (dst_f + len_f)[None, :])
    grp_row = jnp.sum(jnp.where(inside, src_f[None, :] + grp_first[:, None] - dst_f[None, :], 0), axis=1)
    grp_row = jnp.where(jnp.any(inside, axis=1), grp_row, SORTED_TILE_ROWS - g)

    blk_row0 = jnp.arange(n_blk, dtype=jnp.int32) * r
    blk_expert = jnp.minimum(
        jnp.sum((row_end[None, :] <= blk_row0[:, None]).astype(jnp.int32), axis=1), N_EXPERTS - 1)
    n_used = row_end[-1:] // r
    eid = jnp.arange(N_EXPERTS, dtype=jnp.int32)
    nonempty = rows_e > 0
    later = jnp.where((eid[None, :] > eid[:, None]) & nonempty[None, :], eid[None, :], N_EXPERTS)
    next_expert = jnp.min(later, axis=1)
    next_expert = jnp.where(next_expert == N_EXPERTS, -1, next_expert)
    ordinal = jnp.cumsum(nonempty.astype(jnp.int32)) - 1
    blk_first = jnp.concatenate(
        [jnp.ones((1,), jnp.int32), (blk_expert[1:] != blk_expert[:-1]).astype(jnp.int32)])
    blk_onehot = (blk_expert[:, None] == eid[None, :]).astype(jnp.int32)
    blk_next = jnp.sum(blk_onehot * next_expert[None, :], axis=1)
    blk_slot = jnp.sum(blk_onehot * ordinal[None, :], axis=1) % 2

    yb = _experts(blk_expert, blk_first, blk_next, blk_slot, n_used, grp_row, xs, w_gate, w_up, w_down)
    out = _combine(seg_dst.reshape(-1), er, x1, p.reshape(n, PLE_DIM), rw, w_ple_gate.astype(BF16),
                   w_ple_proj.astype(BF16), ln2_g.reshape(1, d), ln2_b.reshape(1, d), yb)
    return out.reshape(b, s, d)


def kernel(x, p, w_in, w_conv, w_out, ln1_g, ln1_b, w_router_g, b_router_g, w_router_e,
           b_router_e, w_gate, w_up, w_down, w_ple_gate, w_ple_proj, ln2_g, ln2_b):
    for i in range(DEPTH):
        x = _layer(x, p[i], w_in[i], w_conv[i], w_out[i], ln1_g[i], ln1_b[i], w_router_g[i],
                   b_router_g[i], w_router_e[i], b_router_e[i], w_gate[i], w_up[i], w_down[i],
                   w_ple_gate[i], w_ple_proj[i], ln2_g[i], ln2_b[i])
    return x
```

```python
import functools

import jax
import jax.numpy as jnp
from jax import lax
from jax.experimental import pallas as pl
from jax.experimental.pallas import tpu as pltpu

D_MODEL = 1024
D_CONV = 512
N_HEADS = 8
HEAD_DIM = 64
D_ATTN = N_HEADS * HEAD_DIM
D_IN = 3 * D_CONV + 3 * D_ATTN
MOBA_BLOCK = 256
MOBA_TOPK = 3
N_GROUPS = 4
EXPERTS_PER_GROUP = 8
N_EXPERTS = N_GROUPS * EXPERTS_PER_GROUP
D_EXPERT = 512
DISPATCH_BLOCK = 256
ROW_GROUP = 8
PLE_DIM = 256
LN_EPS = 1e-5
DEPTH = 1
DEEPNORM_ALPHA = (2 * DEPTH) ** 0.25

LANES = 128
PROJ_ROWS = 512
TOKEN_TILE = 256
SORTED_TILE_ROWS = 2 * TOKEN_TILE + ROW_GROUP * N_EXPERTS
SORTED_ROW_WIDTH = D_MODEL + LANES
KV_CHUNK_SHIFT = 2
KV_CHUNK_BLOCKS = 1 << KV_CHUNK_SHIFT
MOBA_HEADS_PER_STEP = 4
BF16_SUBLANES = 16
ACC_ROWS = HEAD_DIM + BF16_SUBLANES
Q_SCALE = HEAD_DIM ** -0.5 * 1.4426950408889634
VMEM_LIMIT = 48 * 1024 * 1024

F32 = jnp.float32
BF16 = jnp.bfloat16
NEG_INF = float("-inf")


def _dot(a, b):
    return jnp.dot(a, b, preferred_element_type=F32)


def _dot_nt(a, b):
    return lax.dot_general(a, b, (((1,), (1,)), ((), ())), preferred_element_type=F32)


def _layer_norm(h, g, b):
    mu = jnp.mean(h, axis=-1, keepdims=True)
    d = h - mu
    var = jnp.mean(d * d, axis=-1, keepdims=True)
    return d * lax.rsqrt(var + LN_EPS) * g + b


def _proj_conv_kernel(x_ref, w_ref, wc_ref, yc_ref, q_ref, k_ref, vt_ref, km_ref, ubuf):
    s = pl.program_id(1)
    rows = x_ref.shape[0]
    xb = x_ref[...].astype(BF16)

    def sect(i):
        return _dot(xb, w_ref[:, i * D_CONV:(i + 1) * D_CONV])

    @pl.when(s == 0)
    def _():
        ubuf[0:8, :] = jnp.zeros((8, D_CONV), F32)

    u = sect(1) * sect(2)
    ubuf[8:8 + rows, :] = u
    wc = wc_ref[...]
    conv = wc[0:1, :] * ubuf[6:6 + rows, :] + wc[1:2, :] * ubuf[7:7 + rows, :] + wc[2:3, :] * u
    yc_ref[...] = (sect(0) * conv).astype(BF16)
    ubuf[0:8, :] = ubuf[rows:rows + 8, :]

    q_ref[...] = (sect(3) * Q_SCALE).astype(BF16)
    k = sect(4)
    k_ref[...] = k.astype(BF16)
    for i in range(rows // MOBA_BLOCK):
        km_ref[i] = jnp.mean(k[i * MOBA_BLOCK:(i + 1) * MOBA_BLOCK, :], axis=0, keepdims=True)
    vt_ref[...] = sect(5).T.astype(BF16)


def _proj_conv(x, w_in_bf, w_conv):
    b, s, _ = x.shape
    nb = s // MOBA_BLOCK
    grid = (b, s // PROJ_ROWS)
    seq_spec = pl.BlockSpec((None, PROJ_ROWS, D_CONV), lambda i, j: (i, j, 0))
    return pl.pallas_call(
        _proj_conv_kernel,
        grid=grid,
        in_specs=[
            pl.BlockSpec((None, PROJ_ROWS, D_MODEL), lambda i, j: (i, j, 0)),
            pl.BlockSpec((D_MODEL, D_IN), lambda i, j: (0, 0)),
            pl.BlockSpec((3, D_CONV), lambda i, j: (0, 0)),
        ],
        out_specs=[
            seq_spec, seq_spec, seq_spec,
            pl.BlockSpec((None, D_ATTN, PROJ_ROWS), lambda i, j: (i, 0, j)),
            pl.BlockSpec((None, PROJ_ROWS // MOBA_BLOCK, 1, D_ATTN), lambda i, j: (i, j, 0, 0)),
        ],
        out_shape=[
            jax.ShapeDtypeStruct((b, s, D_CONV), BF16),
            jax.ShapeDtypeStruct((b, s, D_ATTN), BF16),
            jax.ShapeDtypeStruct((b, s, D_ATTN), BF16),
            jax.ShapeDtypeStruct((b, D_ATTN, s), BF16),
            jax.ShapeDtypeStruct((b, nb, 1, D_ATTN), F32),
        ],
        scratch_shapes=[pltpu.VMEM((PROJ_ROWS + 8, D_CONV), F32)],
        compiler_params=pltpu.CompilerParams(
            dimension_semantics=("arbitrary", "arbitrary"), vmem_limit_bytes=VMEM_LIMIT),
        name="proj_conv",
    )(x, w_in_bf, w_conv)


def _moba_kernel(q_ref, k_ref, vt_ref, km_ref, o_ref, bias_ref, sd_ref, s_ref, mt_ref, m_ref,
                 acc_ref):
    j = pl.program_id(2)
    blk = MOBA_BLOCK
    nb = km_ref.shape[0]
    n_heads = vt_ref.shape[0] // HEAD_DIM
    heads = range(n_heads)
    n_chunks = lax.shift_right_logical(j + (KV_CHUNK_BLOCKS - 1), KV_CHUNK_SHIFT)

    def pair_lanes(h):
        return slice((h // 2) * LANES, (h // 2 + 1) * LANES)

    row = lax.broadcasted_iota(jnp.int32, (LANES, blk), 0)
    qt_pairs = [q_ref[:, pair_lanes(2 * pr)].astype(F32).T for pr in range(n_heads // 2)]
    qt_heads = [jnp.where((row >= HEAD_DIM) == bool(h % 2), qt_pairs[h // 2], 0.0).astype(BF16)
                for h in heads]

    bidx = lax.broadcasted_iota(jnp.int32, (nb, blk), 0)
    for h in heads:
        km = km_ref[:, pair_lanes(h)]
        km_hi = km.astype(BF16)
        km_lo = (km - km_hi.astype(F32)).astype(BF16)
        g = _dot(km_hi, qt_heads[h]) + _dot(km_lo, qt_heads[h])
        g = jnp.where(bidx < j, g, NEG_INF)
        sel = jnp.zeros((nb, blk), jnp.bool_)
        for _ in range(MOBA_TOPK):
            mx = jnp.max(g, axis=0, keepdims=True)
            cand = jnp.where((g == mx) & (mx > NEG_INF), bidx, nb)
            pick = bidx == jnp.min(cand, axis=0, keepdims=True)
            sel = sel | pick
            g = jnp.where(pick, NEG_INF, g)
        bias_ref[h] = jnp.where(sel, 0.0, NEG_INF)
        m_ref[h] = jnp.full((1, blk), NEG_INF, F32)
        acc_ref[h] = jnp.zeros(acc_ref.shape[1:], F32)

    def k_block(n, h):
        return k_ref[pl.ds(pl.multiple_of(n * blk, blk), blk), pair_lanes(h)]

    def vt_block(n, h):
        return vt_ref[h * HEAD_DIM:(h + 1) * HEAD_DIM, pl.ds(pl.multiple_of(n * blk, blk), blk)]

    def stage_block(c, u, slot):
        n = c * KV_CHUNK_BLOCKS + u
        mts = []
        for h in heads:
            sc = _dot(k_block(n, h), qt_heads[h]) + bias_ref[h, pl.ds(n, 1), :]
            s_ref[slot, h, u * blk:(u + 1) * blk, :] = sc
            mts.append(jnp.max(sc, axis=0, keepdims=True))
        return mts

    def pv(vt_h, sc, m_new):
        p = jnp.exp2(sc - m_new).astype(BF16)
        lhs = jnp.concatenate([vt_h, jnp.ones((ACC_ROWS - HEAD_DIM, vt_h.shape[1]), BF16)], axis=0)
        return _dot(lhs, p)

    def rescale(h, mt):
        m_old = m_ref[h]
        m_new = jnp.maximum(m_old, mt)
        m_ref[h] = m_new
        return m_new, jnp.exp2(m_old - m_new)

    def step(consume, stage):
        if consume is not None:
            cc, cs = consume
            scale = [rescale(h, mt_ref[1 + cs, h]) for h in heads]
            parts = [None] * n_heads
        mts = []
        for u in range(KV_CHUNK_BLOCKS):
            if stage is not None:
                mts.append(stage_block(stage[0], u, stage[1]))
            if consume is not None:
                for h in heads:
                    r = pv(vt_block(cc * KV_CHUNK_BLOCKS + u, h),
                           s_ref[cs, h, u * blk:(u + 1) * blk, :], scale[h][0])
                    parts[h] = r if parts[h] is None else parts[h] + r
        for h in heads:
            if consume is not None:
                acc_ref[h] = scale[h][1] * acc_ref[h] + parts[h]
            if stage is not None:
                mt_ref[1 + stage[1], h] = functools.reduce(jnp.maximum, [m[h] for m in mts])

    kpos = lax.broadcasted_iota(jnp.int32, (blk, blk), 0)
    qpos = lax.broadcasted_iota(jnp.int32, (blk, blk), 1)
    causal = kpos <= qpos
    for h in heads:
        sc = jnp.where(causal, _dot(k_block(j, h), qt_heads[h]), NEG_INF)
        sd_ref[h] = sc
        mt_ref[0, h] = jnp.max(sc, axis=0, keepdims=True)
    step(None, (0, 0))
    for h in heads:
        m_new, alpha = rescale(h, mt_ref[0, h])
        acc_ref[h] = alpha * acc_ref[h] + pv(vt_block(j, h), sd_ref[h], m_new)

    steps = jnp.maximum(n_chunks - 1, 0)

    def body(i, carry):
        c = 2 * i
        step((c, 0), (c + 1, 1))
        step((c + 1, 1), (c + 2, 0))
        return carry

    lax.fori_loop(0, lax.shift_right_logical(steps, 1), body, 0)

    @pl.when((n_chunks > 0) & (steps % 2 == 1))
    def _():
        step((n_chunks - 2, 0), (n_chunks - 1, 1))
        step((n_chunks - 1, 1), None)

    @pl.when((n_chunks > 0) & (steps % 2 == 0))
    def _():
        step((n_chunks - 1, 0), None)

    outs = [acc_ref[h, 0:HEAD_DIM, :] / acc_ref[h, HEAD_DIM:HEAD_DIM + 1, :] for h in heads]
    o_ref[...] = jnp.concatenate(outs, axis=0).T.astype(BF16)


def _moba(q, k, vt, kmean):
    b, s, _ = q.shape
    nb = s // MOBA_BLOCK
    w = MOBA_HEADS_PER_STEP * HEAD_DIM
    hs = MOBA_HEADS_PER_STEP
    return pl.pallas_call(
        _moba_kernel,
        grid=(b, D_ATTN // w, nb),
        in_specs=[
            pl.BlockSpec((None, MOBA_BLOCK, w), lambda i, p, j: (i, j, p)),
            pl.BlockSpec((None, s, w), lambda i, p, j: (i, 0, p)),
            pl.BlockSpec((None, w, s), lambda i, p, j: (i, p, 0)),
            pl.BlockSpec((None, nb, w), lambda i, p, j: (i, 0, p)),
        ],
        out_specs=pl.BlockSpec((None, MOBA_BLOCK, w), lambda i, p, j: (i, j, p)),
        out_shape=jax.ShapeDtypeStruct((b, s, D_ATTN), BF16),
        scratch_shapes=[
            pltpu.VMEM((hs, nb, MOBA_BLOCK), F32),
            pltpu.VMEM((hs, MOBA_BLOCK, MOBA_BLOCK), F32),
            pltpu.VMEM((2, hs, KV_CHUNK_BLOCKS * MOBA_BLOCK, MOBA_BLOCK), F32),
            pltpu.VMEM((3, hs, 1, MOBA_BLOCK), F32),
            pltpu.VMEM((hs, 1, MOBA_BLOCK), F32),
            pltpu.VMEM((hs, ACC_ROWS, MOBA_BLOCK), F32),
        ],
        compiler_params=pltpu.CompilerParams(
            dimension_semantics=("arbitrary", "arbitrary", "arbitrary"),
            vmem_limit_bytes=VMEM_LIMIT),
        name="moba",
    )(q, k, vt, kmean)


GROUP_LANE0 = N_EXPERTS


def _split3(col, lane):
    hi = col.astype(BF16).astype(F32)
    mid = (col - hi).astype(BF16).astype(F32)
    lo = col - hi - mid
    return jnp.where(lane == 0, hi, jnp.where(lane == 1, mid, jnp.where(lane == 2, lo, 0.0))).astype(BF16)


def _mix_route_kernel(yc_ref, ya_ref, x_ref, wo_ref, g_ref, b_ref, wr_ref, br_ref,
                      x1_ref, xs_ref, pos_ref, cnt_ref):
    t = x_ref.shape[0]

    mix = _dot(yc_ref[...], wo_ref[0:D_CONV, :]) + _dot(ya_ref[...], wo_ref[D_CONV:, :])
    x1 = _layer_norm(DEEPNORM_ALPHA * x_ref[...] + mix, g_ref[...], b_ref[...])
    x1_ref[...] = x1

    x_hi = x1.astype(BF16)
    x_lo = (x1 - x_hi.astype(F32)).astype(BF16)
    wr = wr_ref[...]
    w_hi = wr.astype(BF16)
    w_lo = (wr - w_hi.astype(F32)).astype(BF16)
    logits = _dot(x_hi, w_hi) + _dot(x_lo, w_hi) + _dot(x_hi, w_lo) + br_ref[...]

    lane = lax.broadcasted_iota(jnp.int32, (t, LANES), 1).astype(F32)
    big = float(LANES)

    lg = jnp.where((lane >= GROUP_LANE0) & (lane < GROUP_LANE0 + N_GROUPS), logits, NEG_INF)
    gmax = jnp.max(lg, axis=-1, keepdims=True)
    g_p = 1.0 / jnp.sum(jnp.exp(lg - gmax), axis=-1, keepdims=True)
    g_idx = jnp.min(jnp.where(lg == gmax, lane, big), axis=-1, keepdims=True) - GROUP_LANE0

    lo = g_idx * EXPERTS_PER_GROUP
    le = jnp.where((lane >= lo) & (lane < lo + EXPERTS_PER_GROUP), logits, NEG_INF)
    m1 = jnp.max(le, axis=-1, keepdims=True)
    i1 = jnp.min(jnp.where(le == m1, lane, big), axis=-1, keepdims=True)
    le2 = jnp.where(lane == i1, NEG_INF, le)
    m2 = jnp.max(le2, axis=-1, keepdims=True)
    i2 = jnp.min(jnp.where(le2 == m2, lane, big), axis=-1, keepdims=True)
    p2 = jnp.exp(m2 - m1)
    den = 1.0 + p2
    w1 = g_p * (1.0 / den)
    w2 = g_p * (p2 / den)

    oh = (lane == i1) | (lane == i2)
    ohf = oh.astype(F32)
    r_i = lax.broadcasted_iota(jnp.int32, (t, t), 0)
    c_i = lax.broadcasted_iota(jnp.int32, (t, t), 1)
    before = _dot((r_i > c_i).astype(BF16), ohf.astype(BF16))
    r1 = jnp.sum(jnp.where(lane == i1, before, 0.0), axis=-1, keepdims=True)
    r2 = jnp.sum(jnp.where(lane == i2, before, 0.0), axis=-1, keepdims=True)

    cnt = jnp.sum(ohf, axis=0, keepdims=True)
    seg = jnp.floor((cnt + (ROW_GROUP - 1)) * (1.0 / ROW_GROUP)) * ROW_GROUP
    e_a = lax.broadcasted_iota(jnp.int32, (LANES, LANES), 0)
    e_b = lax.broadcasted_iota(jnp.int32, (LANES, LANES), 1)
    seg_start = _dot(jnp.broadcast_to(seg, (8, LANES)).astype(BF16), (e_a < e_b).astype(BF16))[0:1, :]
    pos1 = jnp.sum(jnp.where(lane == i1, seg_start, 0.0), axis=-1, keepdims=True) + r1
    pos2 = jnp.sum(jnp.where(lane == i2, seg_start, 0.0), axis=-1, keepdims=True) + r2
    cnt_ref[...] = jnp.broadcast_to(cnt, cnt_ref.shape)

    cols = jnp.where(lane == 0, i1, jnp.where(lane == 1, i2, jnp.where(lane == 2, r1, jnp.where(
        lane == 3, r2, jnp.where(lane == 4, pos1, pos2)))))
    rows = cols.T[0:8, :]
    pos_ref[...] = jnp.where(lane == 0, pos1, pos2)

    srow = lax.broadcasted_iota(jnp.int32, (xs_ref.shape[0], t), 0).astype(F32)
    perm1 = (srow == rows[4:5, :]).astype(BF16)
    perm2 = (srow == rows[5:6, :]).astype(BF16)
    xs_ref[:, 0:D_MODEL] = _dot(perm1 + perm2, x_hi)
    xs_ref[:, D_MODEL:] = (_dot(perm1, _split3(w1, lane)) + _dot(perm2, _split3(w2, lane)))


def _mix_route(yc, ya, x2d, w_out_bf, g, b, w_r, b_r):
    n = x2d.shape[0]
    t = TOKEN_TILE
    nt = n // t
    row = lambda w: pl.BlockSpec((t, w), lambda i: (i, 0))
    full = lambda a: pl.BlockSpec(a.shape, lambda i: (0, 0))
    return pl.pallas_call(
        _mix_route_kernel,
        grid=(nt,),
        in_specs=[row(D_CONV), row(D_ATTN), row(D_MODEL), full(w_out_bf), full(g), full(b),
                  full(w_r), full(b_r)],
        out_specs=[row(D_MODEL), pl.BlockSpec((SORTED_TILE_ROWS, SORTED_ROW_WIDTH), lambda i: (i, 0)),
                   row(LANES), pl.BlockSpec((8, LANES), lambda i: (i, 0))],
        out_shape=[
            jax.ShapeDtypeStruct((n, D_MODEL), F32),
            jax.ShapeDtypeStruct((nt * SORTED_TILE_ROWS, SORTED_ROW_WIDTH), F32),
            jax.ShapeDtypeStruct((n, LANES), F32),
            jax.ShapeDtypeStruct((nt * 8, LANES), F32),
        ],
        compiler_params=pltpu.CompilerParams(
            dimension_semantics=("arbitrary",), vmem_limit_bytes=VMEM_LIMIT),
        name="mix_route",
    )(yc, ya, x2d, w_out_bf, g, b, w_r, b_r)


def _row_copies_start(rows, src_of, dst_of, sem):
    for r in rows:
        pltpu.make_async_copy(src_of(r), dst_of(r), sem).start()


def _row_copies_wait(src_rows, dst_rows, sem):
    pltpu.make_async_copy(src_rows, dst_rows, sem).wait()


def _experts_kernel(be_ref, first_ref, nxt_ref, par_ref, nu_ref, cur_ref, nxg_ref, xs_hbm,
                    wg_hbm, wu_hbm, wd_hbm, yb_ref, xbuf, wg_buf, wu_buf, wd_buf, xsem, wsem):
    i = pl.program_id(0)
    used = i < nu_ref[0]
    slot = par_ref[i]
    xslot = i % 2
    g = ROW_GROUP
    weights = ((wg_hbm, wg_buf), (wu_hbm, wu_buf), (wd_hbm, wd_buf))

    def fetch(e, s):
        for k, (w_hbm, w_buf) in enumerate(weights):
            pltpu.make_async_copy(w_hbm.at[e], w_buf.at[s], wsem.at[s, k]).start()

    def fetch_wait(s):
        for k, (w_hbm, w_buf) in enumerate(weights):
            pltpu.make_async_copy(w_hbm.at[0], w_buf.at[s], wsem.at[s, k]).wait()

    def gather_rows(grp_ref, s):
        _row_copies_start(
            range(DISPATCH_BLOCK // g),
            lambda q: xs_hbm.at[pl.ds(pl.multiple_of(grp_ref[0, q], g), g)],
            lambda q: xbuf.at[s, pl.ds(q * g, g)], xsem.at[s])

    @pl.when(i == 0)
    def _():
        fetch(be_ref[0], slot)
        gather_rows(cur_ref, 0)

    @pl.when(i + 1 < nu_ref[0])
    def _():
        gather_rows(nxg_ref, 1 - xslot)

    @pl.when(used & (first_ref[i] == 1))
    def _():
        fetch_wait(slot)

        @pl.when(nxt_ref[i] >= 0)
        def _():
            fetch(nxt_ref[i], 1 - slot)

    @pl.when(used)
    def _():
        _row_copies_wait(xs_hbm.at[pl.ds(0, DISPATCH_BLOCK)], xbuf.at[xslot], xsem.at[xslot])
        xb = xbuf[xslot, :, 0:D_MODEL].astype(BF16)
        wrow = xbuf[xslot, :, D_MODEL:]
        weight = wrow[:, 0:1] + wrow[:, 1:2] + wrow[:, 2:3]
        gate = _dot(xb, wg_buf[slot].astype(BF16))
        up = _dot(xb, wu_buf[slot].astype(BF16))
        hid = (jax.nn.silu(gate) * up).astype(BF16)
        yb_ref[...] = _dot(hid, wd_buf[slot].astype(BF16)) * weight

    @pl.when(jnp.logical_not(used))
    def _():
        yb_ref[...] = jnp.zeros_like(yb_ref)


def _experts(blk_expert, blk_first, blk_next, blk_slot, n_used, grp_row, xs, w_gate, w_up, w_down):
    n_blk = blk_expert.shape[0]
    r = DISPATCH_BLOCK
    gpb = r // ROW_GROUP
    grp3 = grp_row.reshape(n_blk, 1, gpb)
    smem_blk = lambda f: pl.BlockSpec((None, 1, gpb), f, memory_space=pltpu.SMEM)
    grid_spec = pltpu.PrefetchScalarGridSpec(
        num_scalar_prefetch=5,
        grid=(n_blk,),
        in_specs=[
            smem_blk(lambda i, *_: (i, 0, 0)),
            smem_blk(lambda i, *_: (jnp.minimum(i + 1, n_blk - 1), 0, 0)),
            pl.BlockSpec(memory_space=pl.ANY),
            pl.BlockSpec(memory_space=pl.ANY),
            pl.BlockSpec(memory_space=pl.ANY),
            pl.BlockSpec(memory_space=pl.ANY),
        ],
        out_specs=pl.BlockSpec((r, D_MODEL), lambda i, *_: (i, 0)),
        scratch_shapes=[
            pltpu.VMEM((2, r, xs.shape[1]), F32),
            pltpu.VMEM((2,) + w_gate.shape[1:], w_gate.dtype),
            pltpu.VMEM((2,) + w_up.shape[1:], w_up.dtype),
            pltpu.VMEM((2,) + w_down.shape[1:], w_down.dtype),
            pltpu.SemaphoreType.DMA((2,)),
            pltpu.SemaphoreType.DMA((2, 3)),
        ],
    )
    return pl.pallas_call(
        _experts_kernel,
        grid_spec=grid_spec,
        out_shape=jax.ShapeDtypeStruct((n_blk * r, D_MODEL), F32),
        compiler_params=pltpu.CompilerParams(
            dimension_semantics=("arbitrary",), vmem_limit_bytes=VMEM_LIMIT),
        name="experts",
    )(blk_expert, blk_first, blk_next, blk_slot, n_used, grp3, grp3, xs, w_gate, w_up, w_down)


def _combine_kernel(cur_ref, nxt_ref, x1_ref, p_ref, pos_ref, wpg_ref, wpp_ref, g_ref, b_ref,
                    yb_hbm, o_ref, ybuf, sem):
    i = pl.program_id(0)
    n_steps = pl.num_programs(0)
    t = x1_ref.shape[0]
    slot = i % 2
    g = ROW_GROUP
    n_groups = ybuf.shape[1] // g

    def gather(grp_ref, s):
        _row_copies_start(
            range(n_groups),
            lambda q: yb_hbm.at[pl.ds(pl.multiple_of(grp_ref[0, q], g), g)],
            lambda q: ybuf.at[s, pl.ds(q * g, g)], sem.at[s])

    @pl.when(i == 0)
    def _():
        gather(cur_ref, 0)

    @pl.when(i + 1 < n_steps)
    def _():
        gather(nxt_ref, 1 - slot)

    x1 = x1_ref[...]
    ple = jax.nn.sigmoid(_dot(x1.astype(BF16), wpg_ref[...])) * _dot(p_ref[...].astype(BF16), wpp_ref[...])

    _row_copies_wait(yb_hbm.at[pl.ds(0, ybuf.shape[1])], ybuf.at[slot], sem.at[slot])
    pos = pos_ref[...]
    scol = lax.broadcasted_iota(jnp.int32, (t, ybuf.shape[1]), 1).astype(F32)
    pick = ((scol == pos[:, 0:1]) | (scol == pos[:, 1:2])).astype(BF16)
    ys = ybuf[slot]
    hi = ys.astype(BF16)
    rest = ys - hi.astype(F32)
    mid = rest.astype(BF16)
    lo = (rest - mid.astype(F32)).astype(BF16)
    ffn = _dot(pick, hi) + _dot(pick, mid) + _dot(pick, lo)
    o_ref[...] = _layer_norm(DEEPNORM_ALPHA * x1 + ffn + ple, g_ref[...], b_ref[...])


def _combine(tile_grp, x1, p2d, pos, w_pg_bf, w_pp_bf, g, b, yb):
    n = x1.shape[0]
    t = TOKEN_TILE
    n_steps = n // t
    gpt = SORTED_TILE_ROWS // ROW_GROUP
    grp3 = tile_grp.reshape(n_steps, 1, gpt)
    smem_blk = lambda f: pl.BlockSpec((None, 1, gpt), f, memory_space=pltpu.SMEM)
    row = lambda w: pl.BlockSpec((t, w), lambda i: (i, 0))
    full = lambda a: pl.BlockSpec(a.shape, lambda i: (0, 0))
    return pl.pallas_call(
        _combine_kernel,
        grid=(n_steps,),
        in_specs=[
            smem_blk(lambda i: (i, 0, 0)),
            smem_blk(lambda i: (jnp.minimum(i + 1, n_steps - 1), 0, 0)),
            row(D_MODEL), row(PLE_DIM), row(LANES), full(w_pg_bf), full(w_pp_bf), full(g), full(b),
            pl.BlockSpec(memory_space=pl.ANY),
        ],
        out_specs=row(D_MODEL),
        out_shape=jax.ShapeDtypeStruct((n, D_MODEL), F32),
        scratch_shapes=[pltpu.VMEM((2, SORTED_TILE_ROWS, D_MODEL), F32), pltpu.SemaphoreType.DMA((2,))],
        compiler_params=pltpu.CompilerParams(
            dimension_semantics=("arbitrary",), vmem_limit_bytes=VMEM_LIMIT),
        name="combine",
    )(grp3, grp3, x1, p2d, pos, w_pg_bf, w_pp_bf, g, b, yb)


def _layer(x, p, w_in, w_conv, w_out, ln1_g, ln1_b, w_router_g, b_router_g, w_router_e,
           b_router_e, w_gate, w_up, w_down, w_ple_gate, w_ple_proj, ln2_g, ln2_b):
    b, s, d = x.shape
    n = b * s
    r = DISPATCH_BLOCK

    yc, q, k, vt, kmean = _proj_conv(x, w_in.astype(BF16), w_conv)
    ya = _moba(q, k, vt, kmean.reshape(b, s // MOBA_BLOCK, D_ATTN))

    w_r = jnp.zeros((d, LANES), F32).at[:, :N_EXPERTS].set(w_router_e)
    w_r = w_r.at[:, GROUP_LANE0:GROUP_LANE0 + N_GROUPS].set(w_router_g)
    b_r = jnp.zeros((1, LANES), F32).at[0, :N_EXPERTS].set(b_router_e)
    b_r = b_r.at[0, GROUP_LANE0:GROUP_LANE0 + N_GROUPS].set(b_router_g)
    x1, xs, pos, cnt = _mix_route(
        yc.reshape(n, D_CONV), ya.reshape(n, D_ATTN), x.reshape(n, d), w_out.astype(BF16),
        ln1_g.reshape(1, d), ln1_b.reshape(1, d), w_r, b_r)

    nt = n // TOKEN_TILE
    g = ROW_GROUP
    seg = (cnt.reshape(nt, 8, LANES)[:, 0, :N_EXPERTS].astype(jnp.int32) + g - 1) // g * g
    rows_e = jnp.sum(seg, axis=0)
    padded = (rows_e + r - 1) // r * r
    row_end = jnp.cumsum(padded)
    row_start = row_end - padded
    seg_dst = row_start[None, :] + jnp.cumsum(seg, axis=0) - seg
    seg_off = jnp.cumsum(seg, axis=1) - seg
    seg_src = (jnp.arange(nt, dtype=jnp.int32) * SORTED_TILE_ROWS)[:, None] + seg_off
    n_blk = -(-(2 * n + (g - 1) * N_EXPERTS * nt) // r) + N_EXPERTS

    blk_row0 = jnp.arange(n_blk, dtype=jnp.int32) * r
    blk_expert = jnp.minimum(
        jnp.sum((row_end[None, :] <= blk_row0[:, None]).astype(jnp.int32), axis=1), N_EXPERTS - 1)
    n_used = row_end[-1:] // r
    eid = jnp.arange(N_EXPERTS, dtype=jnp.int32)
    blk_onehot = (blk_expert[:, None] == eid[None, :]).astype(jnp.int32)

    def of_block(tab):
        return jnp.sum(blk_onehot[:, :, None] * tab.T[None, :, :], axis=1)
    dst_b, src_b, len_b = of_block(seg_dst)[:, None, :], of_block(seg_src)[:, None, :], of_block(seg)[:, None, :]
    grp_first = (blk_row0[:, None] + jnp.arange(r // g, dtype=jnp.int32)[None, :] * g)[:, :, None]
    inside = (grp_first >= dst_b) & (grp_first < dst_b + len_b)
    grp_row = jnp.where(jnp.any(inside, axis=2),
                        jnp.sum(jnp.where(inside, src_b + grp_first - dst_b, 0), axis=2),
                        SORTED_TILE_ROWS - g)

    tile_first = (jnp.arange(SORTED_TILE_ROWS // g, dtype=jnp.int32) * g)[None, :, None]
    t_off, t_len, t_dst = seg_off[:, None, :], seg[:, None, :], seg_dst[:, None, :]
    within = (tile_first >= t_off) & (tile_first < t_off + t_len)
    tile_grp = jnp.sum(jnp.where(within, t_dst + tile_first - t_off, 0), axis=2)

    nonempty = rows_e > 0
    later = jnp.where((eid[None, :] > eid[:, None]) & nonempty[None, :], eid[None, :], N_EXPERTS)
    next_expert = jnp.min(later, axis=1)
    next_expert = jnp.where(next_expert == N_EXPERTS, -1, next_expert)
    ordinal = jnp.cumsum(nonempty.astype(jnp.int32)) - 1
    blk_first = jnp.concatenate(
        [jnp.ones((1,), jnp.int32), (blk_expert[1:] != blk_expert[:-1]).astype(jnp.int32)])
    blk_next = jnp.sum(blk_onehot * next_expert[None, :], axis=1)
    blk_slot = jnp.sum(blk_onehot * ordinal[None, :], axis=1) % 2

    yb = _experts(blk_expert, blk_first, blk_next, blk_slot, n_used, grp_row, xs, w_gate, w_up, w_down)
    out = _combine(tile_grp, x1, p.reshape(n, PLE_DIM), pos, w_ple_gate.astype(BF16),
                   w_ple_proj.astype(BF16), ln2_g.reshape(1, d), ln2_b.reshape(1, d), yb)
    return out.reshape(b, s, d)


def kernel(x, p, w_in, w_conv, w_out, ln1_g, ln1_b, w_router_g, b_router_g, w_router_e,
           b_router_e, w_gate, w_up, w_down, w_ple_gate, w_ple_proj, ln2_g, ln2_b):
    for i in range(DEPTH):
        x = _layer(x, p[i], w_in[i], w_conv[i], w_out[i], ln1_g[i], ln1_b[i], w_router_g[i],
                   b_router_g[i], w_router_e[i], b_router_e[i], w_gate[i], w_up[i], w_down[i],
                   w_ple_gate[i], w_ple_proj[i], ln2_g[i], ln2_b[i])
    return x
```

```python
import functools

import jax
import jax.numpy as jnp
from jax import lax
from jax.experimental import pallas as pl
from jax.experimental.pallas import tpu as pltpu

D_MODEL = 1024
D_CONV = 512
N_HEADS = 8
HEAD_DIM = 64
D_ATTN = N_HEADS * HEAD_DIM
D_IN = 3 * D_CONV + 3 * D_ATTN
MOBA_BLOCK = 256
MOBA_TOPK = 3
N_GROUPS = 4
EXPERTS_PER_GROUP = 8
N_EXPERTS = N_GROUPS * EXPERTS_PER_GROUP
D_EXPERT = 512
DISPATCH_BLOCK = 256
ROW_GROUP = 8
PLE_DIM = 256
LN_EPS = 1e-5
DEPTH = 1
DEEPNORM_ALPHA = (2 * DEPTH) ** 0.25

LANES = 128
PROJ_ROWS = 512
TOKEN_TILE = 256
SORTED_TILE_ROWS = 2 * TOKEN_TILE + ROW_GROUP * N_EXPERTS
UNSORT_CHUNK = 256
SORTED_ROW_WIDTH = D_MODEL + LANES
KV_CHUNK_SHIFT = 2
KV_CHUNK_BLOCKS = 1 << KV_CHUNK_SHIFT
MOBA_HEADS_PER_STEP = 4
BF16_SUBLANES = 16
ACC_ROWS = HEAD_DIM + BF16_SUBLANES
Q_SCALE = HEAD_DIM ** -0.5 * 1.4426950408889634
VMEM_LIMIT = 48 * 1024 * 1024

F32 = jnp.float32
BF16 = jnp.bfloat16
NEG_INF = float("-inf")


def _dot(a, b):
    return jnp.dot(a, b, preferred_element_type=F32)


def _dot_nt(a, b):
    return lax.dot_general(a, b, (((1,), (1,)), ((), ())), preferred_element_type=F32)


def _layer_norm(h, g, b):
    mu = jnp.mean(h, axis=-1, keepdims=True)
    d = h - mu
    var = jnp.mean(d * d, axis=-1, keepdims=True)
    return d * lax.rsqrt(var + LN_EPS) * g + b


def _proj_conv_kernel(x_ref, w_ref, wc_ref, yc_ref, q_ref, k_ref, vt_ref, km_ref, ubuf):
    s = pl.program_id(1)
    rows = x_ref.shape[0]
    xb = x_ref[...].astype(BF16)

    def sect(i):
        return _dot(xb, w_ref[:, i * D_CONV:(i + 1) * D_CONV])

    @pl.when(s == 0)
    def _():
        ubuf[0:8, :] = jnp.zeros((8, D_CONV), F32)

    u = sect(1) * sect(2)
    ubuf[8:8 + rows, :] = u
    wc = wc_ref[...]
    conv = wc[0:1, :] * ubuf[6:6 + rows, :] + wc[1:2, :] * ubuf[7:7 + rows, :] + wc[2:3, :] * u
    yc_ref[...] = (sect(0) * conv).astype(BF16)
    ubuf[0:8, :] = ubuf[rows:rows + 8, :]

    q_ref[...] = (sect(3) * Q_SCALE).astype(BF16)
    k = sect(4)
    k_ref[...] = k.astype(BF16)
    for i in range(rows // MOBA_BLOCK):
        km_ref[i] = jnp.mean(k[i * MOBA_BLOCK:(i + 1) * MOBA_BLOCK, :], axis=0, keepdims=True)
    vt_ref[...] = sect(5).T.astype(BF16)


def _proj_conv(x, w_in_bf, w_conv):
    b, s, _ = x.shape
    nb = s // MOBA_BLOCK
    grid = (b, s // PROJ_ROWS)
    seq_spec = pl.BlockSpec((None, PROJ_ROWS, D_CONV), lambda i, j: (i, j, 0))
    return pl.pallas_call(
        _proj_conv_kernel,
        grid=grid,
        in_specs=[
            pl.BlockSpec((None, PROJ_ROWS, D_MODEL), lambda i, j: (i, j, 0)),
            pl.BlockSpec((D_MODEL, D_IN), lambda i, j: (0, 0)),
            pl.BlockSpec((3, D_CONV), lambda i, j: (0, 0)),
        ],
        out_specs=[
            seq_spec, seq_spec, seq_spec,
            pl.BlockSpec((None, D_ATTN, PROJ_ROWS), lambda i, j: (i, 0, j)),
            pl.BlockSpec((None, PROJ_ROWS // MOBA_BLOCK, 1, D_ATTN), lambda i, j: (i, j, 0, 0)),
        ],
        out_shape=[
            jax.ShapeDtypeStruct((b, s, D_CONV), BF16),
            jax.ShapeDtypeStruct((b, s, D_ATTN), BF16),
            jax.ShapeDtypeStruct((b, s, D_ATTN), BF16),
            jax.ShapeDtypeStruct((b, D_ATTN, s), BF16),
            jax.ShapeDtypeStruct((b, nb, 1, D_ATTN), F32),
        ],
        scratch_shapes=[pltpu.VMEM((PROJ_ROWS + 8, D_CONV), F32)],
        compiler_params=pltpu.CompilerParams(
            dimension_semantics=("arbitrary", "arbitrary"), vmem_limit_bytes=VMEM_LIMIT),
        name="proj_conv",
    )(x, w_in_bf, w_conv)


def _moba_kernel(q_ref, k_ref, vt_ref, km_ref, o_ref, bias_ref, sd_ref, s_ref, mt_ref, m_ref,
                 acc_ref):
    j = pl.program_id(2)
    blk = MOBA_BLOCK
    nb = km_ref.shape[0]
    n_heads = vt_ref.shape[0] // HEAD_DIM
    heads = range(n_heads)
    n_chunks = lax.shift_right_logical(j + (KV_CHUNK_BLOCKS - 1), KV_CHUNK_SHIFT)

    def pair_lanes(h):
        return slice((h // 2) * LANES, (h // 2 + 1) * LANES)

    row = lax.broadcasted_iota(jnp.int32, (LANES, blk), 0)
    qt_pairs = [q_ref[:, pair_lanes(2 * pr)].astype(F32).T for pr in range(n_heads // 2)]
    qt_heads = [jnp.where((row >= HEAD_DIM) == bool(h % 2), qt_pairs[h // 2], 0.0).astype(BF16)
                for h in heads]

    bidx = lax.broadcasted_iota(jnp.int32, (nb, blk), 0)
    for h in heads:
        km = km_ref[:, pair_lanes(h)]
        km_hi = km.astype(BF16)
        km_lo = (km - km_hi.astype(F32)).astype(BF16)
        g = _dot(km_hi, qt_heads[h]) + _dot(km_lo, qt_heads[h])
        g = jnp.where(bidx < j, g, NEG_INF)
        sel = jnp.zeros((nb, blk), jnp.bool_)
        for _ in range(MOBA_TOPK):
            mx = jnp.max(g, axis=0, keepdims=True)
            cand = jnp.where((g == mx) & (mx > NEG_INF), bidx, nb)
            pick = bidx == jnp.min(cand, axis=0, keepdims=True)
            sel = sel | pick
            g = jnp.where(pick, NEG_INF, g)
        bias_ref[h] = jnp.where(sel, 0.0, NEG_INF)
        m_ref[h] = jnp.full((1, blk), NEG_INF, F32)
        acc_ref[h] = jnp.zeros(acc_ref.shape[1:], F32)

    def k_block(n, h):
        return k_ref[pl.ds(pl.multiple_of(n * blk, blk), blk), pair_lanes(h)]

    def vt_block(n, h):
        return vt_ref[h * HEAD_DIM:(h + 1) * HEAD_DIM, pl.ds(pl.multiple_of(n * blk, blk), blk)]

    def stage_block(c, u, slot):
        n = c * KV_CHUNK_BLOCKS + u
        mts = []
        for h in heads:
            sc = _dot(k_block(n, h), qt_heads[h]) + bias_ref[h, pl.ds(n, 1), :]
            s_ref[slot, h, u * blk:(u + 1) * blk, :] = sc
            mts.append(jnp.max(sc, axis=0, keepdims=True))
        return mts

    def pv(vt_h, sc, m_new):
        p = jnp.exp2(sc - m_new).astype(BF16)
        lhs = jnp.concatenate([vt_h, jnp.ones((ACC_ROWS - HEAD_DIM, vt_h.shape[1]), BF16)], axis=0)
        return _dot(lhs, p)

    def rescale(h, mt):
        m_old = m_ref[h]
        m_new = jnp.maximum(m_old, mt)
        m_ref[h] = m_new
        return m_new, jnp.exp2(m_old - m_new)

    def step(consume, stage):
        if consume is not None:
            cc, cs = consume
            scale = [rescale(h, mt_ref[1 + cs, h]) for h in heads]
            parts = [None] * n_heads
        mts = []
        for u in range(KV_CHUNK_BLOCKS):
            if stage is not None:
                mts.append(stage_block(stage[0], u, stage[1]))
            if consume is not None:
                for h in heads:
                    r = pv(vt_block(cc * KV_CHUNK_BLOCKS + u, h),
                           s_ref[cs, h, u * blk:(u + 1) * blk, :], scale[h][0])
                    parts[h] = r if parts[h] is None else parts[h] + r
        for h in heads:
            if consume is not None:
                acc_ref[h] = scale[h][1] * acc_ref[h] + parts[h]
            if stage is not None:
                mt_ref[1 + stage[1], h] = functools.reduce(jnp.maximum, [m[h] for m in mts])

    kpos = lax.broadcasted_iota(jnp.int32, (blk, blk), 0)
    qpos = lax.broadcasted_iota(jnp.int32, (blk, blk), 1)
    causal = kpos <= qpos
    for h in heads:
        sc = jnp.where(causal, _dot(k_block(j, h), qt_heads[h]), NEG_INF)
        sd_ref[h] = sc
        mt_ref[0, h] = jnp.max(sc, axis=0, keepdims=True)
    step(None, (0, 0))
    for h in heads:
        m_new, alpha = rescale(h, mt_ref[0, h])
        acc_ref[h] = alpha * acc_ref[h] + pv(vt_block(j, h), sd_ref[h], m_new)

    steps = jnp.maximum(n_chunks - 1, 0)

    def body(i, carry):
        c = 2 * i
        step((c, 0), (c + 1, 1))
        step((c + 1, 1), (c + 2, 0))
        return carry

    lax.fori_loop(0, lax.shift_right_logical(steps, 1), body, 0)

    @pl.when((n_chunks > 0) & (steps % 2 == 1))
    def _():
        step((n_chunks - 2, 0), (n_chunks - 1, 1))
        step((n_chunks - 1, 1), None)

    @pl.when((n_chunks > 0) & (steps % 2 == 0))
    def _():
        step((n_chunks - 1, 0), None)

    outs = [acc_ref[h, 0:HEAD_DIM, :] / acc_ref[h, HEAD_DIM:HEAD_DIM + 1, :] for h in heads]
    o_ref[...] = jnp.concatenate(outs, axis=0).T.astype(BF16)


def _moba(q, k, vt, kmean):
    b, s, _ = q.shape
    nb = s // MOBA_BLOCK
    w = MOBA_HEADS_PER_STEP * HEAD_DIM
    hs = MOBA_HEADS_PER_STEP
    return pl.pallas_call(
        _moba_kernel,
        grid=(b, D_ATTN // w, nb),
        in_specs=[
            pl.BlockSpec((None, MOBA_BLOCK, w), lambda i, p, j: (i, j, p)),
            pl.BlockSpec((None, s, w), lambda i, p, j: (i, 0, p)),
            pl.BlockSpec((None, w, s), lambda i, p, j: (i, p, 0)),
            pl.BlockSpec((None, nb, w), lambda i, p, j: (i, 0, p)),
        ],
        out_specs=pl.BlockSpec((None, MOBA_BLOCK, w), lambda i, p, j: (i, j, p)),
        out_shape=jax.ShapeDtypeStruct((b, s, D_ATTN), BF16),
        scratch_shapes=[
            pltpu.VMEM((hs, nb, MOBA_BLOCK), F32),
            pltpu.VMEM((hs, MOBA_BLOCK, MOBA_BLOCK), F32),
            pltpu.VMEM((2, hs, KV_CHUNK_BLOCKS * MOBA_BLOCK, MOBA_BLOCK), F32),
            pltpu.VMEM((3, hs, 1, MOBA_BLOCK), F32),
            pltpu.VMEM((hs, 1, MOBA_BLOCK), F32),
            pltpu.VMEM((hs, ACC_ROWS, MOBA_BLOCK), F32),
        ],
        compiler_params=pltpu.CompilerParams(
            dimension_semantics=("arbitrary", "arbitrary", "arbitrary"),
            vmem_limit_bytes=VMEM_LIMIT),
        name="moba",
    )(q, k, vt, kmean)


GROUP_LANE0 = N_EXPERTS


def _split3(col, lane):
    hi = col.astype(BF16).astype(F32)
    mid = (col - hi).astype(BF16).astype(F32)
    lo = col - hi - mid
    return jnp.where(lane == 0, hi, jnp.where(lane == 1, mid, jnp.where(lane == 2, lo, 0.0))).astype(BF16)


def _mix_route_kernel(yc_ref, ya_ref, x_ref, wo_ref, g_ref, b_ref, wr_ref, br_ref,
                      x1_ref, xs_ref, pos_ref, cnt_ref):
    t = x_ref.shape[0]

    mix = _dot(yc_ref[...], wo_ref[0:D_CONV, :]) + _dot(ya_ref[...], wo_ref[D_CONV:, :])
    x1 = _layer_norm(DEEPNORM_ALPHA * x_ref[...] + mix, g_ref[...], b_ref[...])
    x1_ref[...] = x1

    x_hi = x1.astype(BF16)
    x_lo = (x1 - x_hi.astype(F32)).astype(BF16)
    wr = wr_ref[...]
    w_hi = wr.astype(BF16)
    w_lo = (wr - w_hi.astype(F32)).astype(BF16)
    logits = _dot(x_hi, w_hi) + _dot(x_lo, w_hi) + _dot(x_hi, w_lo) + br_ref[...]

    lane = lax.broadcasted_iota(jnp.int32, (t, LANES), 1).astype(F32)
    big = float(LANES)

    lg = jnp.where((lane >= GROUP_LANE0) & (lane < GROUP_LANE0 + N_GROUPS), logits, NEG_INF)
    gmax = jnp.max(lg, axis=-1, keepdims=True)
    g_p = 1.0 / jnp.sum(jnp.exp(lg - gmax), axis=-1, keepdims=True)
    g_idx = jnp.min(jnp.where(lg == gmax, lane, big), axis=-1, keepdims=True) - GROUP_LANE0

    lo = g_idx * EXPERTS_PER_GROUP
    le = jnp.where((lane >= lo) & (lane < lo + EXPERTS_PER_GROUP), logits, NEG_INF)
    m1 = jnp.max(le, axis=-1, keepdims=True)
    i1 = jnp.min(jnp.where(le == m1, lane, big), axis=-1, keepdims=True)
    le2 = jnp.where(lane == i1, NEG_INF, le)
    m2 = jnp.max(le2, axis=-1, keepdims=True)
    i2 = jnp.min(jnp.where(le2 == m2, lane, big), axis=-1, keepdims=True)
    p2 = jnp.exp(m2 - m1)
    den = 1.0 + p2
    w1 = g_p * (1.0 / den)
    w2 = g_p * (p2 / den)

    oh = (lane == i1) | (lane == i2)
    ohf = oh.astype(F32)
    r_i = lax.broadcasted_iota(jnp.int32, (t, t), 0)
    c_i = lax.broadcasted_iota(jnp.int32, (t, t), 1)
    before = _dot((r_i > c_i).astype(BF16), ohf.astype(BF16))
    r1 = jnp.sum(jnp.where(lane == i1, before, 0.0), axis=-1, keepdims=True)
    r2 = jnp.sum(jnp.where(lane == i2, before, 0.0), axis=-1, keepdims=True)

    cnt = jnp.sum(ohf, axis=0, keepdims=True)
    seg = jnp.floor((cnt + (ROW_GROUP - 1)) * (1.0 / ROW_GROUP)) * ROW_GROUP
    e_a = lax.broadcasted_iota(jnp.int32, (LANES, LANES), 0)
    e_b = lax.broadcasted_iota(jnp.int32, (LANES, LANES), 1)
    seg_start = _dot(jnp.broadcast_to(seg, (8, LANES)).astype(BF16), (e_a < e_b).astype(BF16))[0:1, :]
    pos1 = jnp.sum(jnp.where(lane == i1, seg_start, 0.0), axis=-1, keepdims=True) + r1
    pos2 = jnp.sum(jnp.where(lane == i2, seg_start, 0.0), axis=-1, keepdims=True) + r2
    cnt_ref[...] = jnp.broadcast_to(cnt, cnt_ref.shape)

    cols = jnp.where(lane == 0, i1, jnp.where(lane == 1, i2, jnp.where(lane == 2, r1, jnp.where(
        lane == 3, r2, jnp.where(lane == 4, pos1, pos2)))))
    rows = cols.T[0:8, :]
    pos_ref[...] = jnp.where(lane == 0, pos1, pos2)

    srow = lax.broadcasted_iota(jnp.int32, (xs_ref.shape[0], t), 0).astype(F32)
    perm1 = (srow == rows[4:5, :]).astype(BF16)
    perm2 = (srow == rows[5:6, :]).astype(BF16)
    xs_ref[:, 0:D_MODEL] = _dot(perm1 + perm2, x_hi)
    xs_ref[:, D_MODEL:] = (_dot(perm1, _split3(w1, lane)) + _dot(perm2, _split3(w2, lane)))


def _mix_route(yc, ya, x2d, w_out_bf, g, b, w_r, b_r):
    n = x2d.shape[0]
    t = TOKEN_TILE
    nt = n // t
    row = lambda w: pl.BlockSpec((t, w), lambda i: (i, 0))
    full = lambda a: pl.BlockSpec(a.shape, lambda i: (0, 0))
    return pl.pallas_call(
        _mix_route_kernel,
        grid=(nt,),
        in_specs=[row(D_CONV), row(D_ATTN), row(D_MODEL), full(w_out_bf), full(g), full(b),
                  full(w_r), full(b_r)],
        out_specs=[row(D_MODEL), pl.BlockSpec((SORTED_TILE_ROWS, SORTED_ROW_WIDTH), lambda i: (i, 0)),
                   row(LANES), pl.BlockSpec((8, LANES), lambda i: (i, 0))],
        out_shape=[
            jax.ShapeDtypeStruct((n, D_MODEL), F32),
            jax.ShapeDtypeStruct((nt * SORTED_TILE_ROWS, SORTED_ROW_WIDTH), F32),
            jax.ShapeDtypeStruct((n, LANES), F32),
            jax.ShapeDtypeStruct((nt * 8, LANES), F32),
        ],
        compiler_params=pltpu.CompilerParams(
            dimension_semantics=("arbitrary",), vmem_limit_bytes=VMEM_LIMIT),
        name="mix_route",
    )(yc, ya, x2d, w_out_bf, g, b, w_r, b_r)


def _row_copies_start(rows, src_of, dst_of, sem):
    for r in rows:
        pltpu.make_async_copy(src_of(r), dst_of(r), sem).start()


def _row_copies_wait(src_rows, dst_rows, sem):
    pltpu.make_async_copy(src_rows, dst_rows, sem).wait()


def _experts_kernel(be_ref, first_ref, nxt_ref, par_ref, nu_ref, cur_ref, nxg_ref, xs_hbm,
                    wg_hbm, wu_hbm, wd_hbm, yb_ref, xbuf, wg_buf, wu_buf, wd_buf, wg_bf, wu_bf, wd_bf,
                    xsem, wsem):
    i = pl.program_id(0)
    used = i < nu_ref[0]
    slot = par_ref[i]
    xslot = i % 2
    g = ROW_GROUP
    weights = ((wg_hbm, wg_buf), (wu_hbm, wu_buf), (wd_hbm, wd_buf))

    def fetch(e, s):
        for k, (w_hbm, w_buf) in enumerate(weights):
            pltpu.make_async_copy(w_hbm.at[e], w_buf.at[s], wsem.at[s, k]).start()

    def fetch_wait(s):
        for k, (w_hbm, w_buf) in enumerate(weights):
            pltpu.make_async_copy(w_hbm.at[0], w_buf.at[s], wsem.at[s, k]).wait()

    def gather_rows(grp_ref, s):
        _row_copies_start(
            range(DISPATCH_BLOCK // g),
            lambda q: xs_hbm.at[pl.ds(pl.multiple_of(grp_ref[0, q], g), g)],
            lambda q: xbuf.at[s, pl.ds(q * g, g)], xsem.at[s])

    @pl.when(i == 0)
    def _():
        fetch(be_ref[0], slot)
        gather_rows(cur_ref, 0)

    @pl.when(i + 1 < nu_ref[0])
    def _():
        gather_rows(nxg_ref, 1 - xslot)

    @pl.when(used & (first_ref[i] == 1))
    def _():
        fetch_wait(slot)

        @pl.when(nxt_ref[i] >= 0)
        def _():
            fetch(nxt_ref[i], 1 - slot)

        wg_bf[...] = wg_buf[slot].astype(BF16)
        wu_bf[...] = wu_buf[slot].astype(BF16)
        wd_bf[...] = wd_buf[slot].astype(BF16)

    @pl.when(used)
    def _():
        _row_copies_wait(xs_hbm.at[pl.ds(0, DISPATCH_BLOCK)], xbuf.at[xslot], xsem.at[xslot])
        xb = xbuf[xslot, :, 0:D_MODEL].astype(BF16)
        wrow = xbuf[xslot, :, D_MODEL:]
        weight = wrow[:, 0:1] + wrow[:, 1:2] + wrow[:, 2:3]
        gate = _dot(xb, wg_bf[...])
        up = _dot(xb, wu_bf[...])
        hid = (jax.nn.silu(gate) * up).astype(BF16)
        yb_ref[...] = _dot(hid, wd_bf[...]) * weight

    @pl.when(jnp.logical_not(used))
    def _():
        yb_ref[...] = jnp.zeros_like(yb_ref)


def _experts(blk_expert, blk_first, blk_next, blk_slot, n_used, grp_row, xs, w_gate, w_up, w_down):
    n_blk = blk_expert.shape[0]
    r = DISPATCH_BLOCK
    gpb = r // ROW_GROUP
    grp3 = grp_row.reshape(n_blk, 1, gpb)
    smem_blk = lambda f: pl.BlockSpec((None, 1, gpb), f, memory_space=pltpu.SMEM)
    grid_spec = pltpu.PrefetchScalarGridSpec(
        num_scalar_prefetch=5,
        grid=(n_blk,),
        in_specs=[
            smem_blk(lambda i, *_: (i, 0, 0)),
            smem_blk(lambda i, *_: (jnp.minimum(i + 1, n_blk - 1), 0, 0)),
            pl.BlockSpec(memory_space=pl.ANY),
            pl.BlockSpec(memory_space=pl.ANY),
            pl.BlockSpec(memory_space=pl.ANY),
            pl.BlockSpec(memory_space=pl.ANY),
        ],
        out_specs=pl.BlockSpec((r, D_MODEL), lambda i, *_: (i, 0)),
        scratch_shapes=[
            pltpu.VMEM((2, r, xs.shape[1]), F32),
            pltpu.VMEM((2,) + w_gate.shape[1:], w_gate.dtype),
            pltpu.VMEM((2,) + w_up.shape[1:], w_up.dtype),
            pltpu.VMEM((2,) + w_down.shape[1:], w_down.dtype),
            pltpu.VMEM(w_gate.shape[1:], BF16),
            pltpu.VMEM(w_up.shape[1:], BF16),
            pltpu.VMEM(w_down.shape[1:], BF16),
            pltpu.SemaphoreType.DMA((2,)),
            pltpu.SemaphoreType.DMA((2, 3)),
        ],
    )
    return pl.pallas_call(
        _experts_kernel,
        grid_spec=grid_spec,
        out_shape=jax.ShapeDtypeStruct((n_blk * r, D_MODEL), F32),
        compiler_params=pltpu.CompilerParams(
            dimension_semantics=("arbitrary",), vmem_limit_bytes=VMEM_LIMIT),
        name="experts",
    )(blk_expert, blk_first, blk_next, blk_slot, n_used, grp3, grp3, xs, w_gate, w_up, w_down)


def _combine_kernel(cur_ref, nxt_ref, x1_ref, p_ref, pos_ref, wpg_ref, wpp_ref, g_ref, b_ref,
                    yb_hbm, o_ref, ybuf, sem):
    i = pl.program_id(0)
    n_steps = pl.num_programs(0)
    t = x1_ref.shape[0]
    slot = i % 2
    g = ROW_GROUP
    n_groups = ybuf.shape[1] // g

    def gather(grp_ref, s):
        _row_copies_start(
            range(n_groups),
            lambda q: yb_hbm.at[pl.ds(pl.multiple_of(grp_ref[0, q], g), g)],
            lambda q: ybuf.at[s, pl.ds(q * g, g)], sem.at[s])

    @pl.when(i == 0)
    def _():
        gather(cur_ref, 0)

    @pl.when(i + 1 < n_steps)
    def _():
        gather(nxt_ref, 1 - slot)

    x1 = x1_ref[...]
    ple = jax.nn.sigmoid(_dot(x1.astype(BF16), wpg_ref[...])) * _dot(p_ref[...].astype(BF16), wpp_ref[...])

    _row_copies_wait(yb_hbm.at[pl.ds(0, ybuf.shape[1])], ybuf.at[slot], sem.at[slot])
    pos = pos_ref[...]
    scol = lax.broadcasted_iota(jnp.int32, (t, ybuf.shape[1]), 1).astype(F32)
    pick = ((scol == pos[:, 0:1]) | (scol == pos[:, 1:2])).astype(BF16)
    ffn = None
    for c in range(0, ybuf.shape[1], UNSORT_CHUNK):
        ys = ybuf[slot, c:c + UNSORT_CHUNK, :]
        hi = ys.astype(BF16)
        rest = ys - hi.astype(F32)
        mid = rest.astype(BF16)
        lo = (rest - mid.astype(F32)).astype(BF16)
        pc = pick[:, c:c + UNSORT_CHUNK]
        part = _dot(pc, hi) + _dot(pc, mid) + _dot(pc, lo)
        ffn = part if ffn is None else ffn + part
    o_ref[...] = _layer_norm(DEEPNORM_ALPHA * x1 + ffn + ple, g_ref[...], b_ref[...])


def _combine(tile_grp, x1, p2d, pos, w_pg_bf, w_pp_bf, g, b, yb):
    n = x1.shape[0]
    t = TOKEN_TILE
    n_steps = n // t
    gpt = SORTED_TILE_ROWS // ROW_GROUP
    grp3 = tile_grp.reshape(n_steps, 1, gpt)
    smem_blk = lambda f: pl.BlockSpec((None, 1, gpt), f, memory_space=pltpu.SMEM)
    row = lambda w: pl.BlockSpec((t, w), lambda i: (i, 0))
    full = lambda a: pl.BlockSpec(a.shape, lambda i: (0, 0))
    return pl.pallas_call(
        _combine_kernel,
        grid=(n_steps,),
        in_specs=[
            smem_blk(lambda i: (i, 0, 0)),
            smem_blk(lambda i: (jnp.minimum(i + 1, n_steps - 1), 0, 0)),
            row(D_MODEL), row(PLE_DIM), row(LANES), full(w_pg_bf), full(w_pp_bf), full(g), full(b),
            pl.BlockSpec(memory_space=pl.ANY),
        ],
        out_specs=row(D_MODEL),
        out_shape=jax.ShapeDtypeStruct((n, D_MODEL), F32),
        scratch_shapes=[pltpu.VMEM((2, SORTED_TILE_ROWS, D_MODEL), F32), pltpu.SemaphoreType.DMA((2,))],
        compiler_params=pltpu.CompilerParams(
            dimension_semantics=("arbitrary",), vmem_limit_bytes=VMEM_LIMIT),
        name="combine",
    )(grp3, grp3, x1, p2d, pos, w_pg_bf, w_pp_bf, g, b, yb)


def _layer(x, p, w_in, w_conv, w_out, ln1_g, ln1_b, w_router_g, b_router_g, w_router_e,
           b_router_e, w_gate, w_up, w_down, w_ple_gate, w_ple_proj, ln2_g, ln2_b):
    b, s, d = x.shape
    n = b * s
    r = DISPATCH_BLOCK

    yc, q, k, vt, kmean = _proj_conv(x, w_in.astype(BF16), w_conv)
    ya = _moba(q, k, vt, kmean.reshape(b, s // MOBA_BLOCK, D_ATTN))

    w_r = jnp.zeros((d, LANES), F32).at[:, :N_EXPERTS].set(w_router_e)
    w_r = w_r.at[:, GROUP_LANE0:GROUP_LANE0 + N_GROUPS].set(w_router_g)
    b_r = jnp.zeros((1, LANES), F32).at[0, :N_EXPERTS].set(b_router_e)
    b_r = b_r.at[0, GROUP_LANE0:GROUP_LANE0 + N_GROUPS].set(b_router_g)
    x1, xs, pos, cnt = _mix_route(
        yc.reshape(n, D_CONV), ya.reshape(n, D_ATTN), x.reshape(n, d), w_out.astype(BF16),
        ln1_g.reshape(1, d), ln1_b.reshape(1, d), w_r, b_r)

    nt = n // TOKEN_TILE
    g = ROW_GROUP
    seg = (cnt.reshape(nt, 8, LANES)[:, 0, :N_EXPERTS].astype(jnp.int32) + g - 1) // g * g
    rows_e = jnp.sum(seg, axis=0)
    padded = (rows_e + r - 1) // r * r
    row_end = jnp.cumsum(padded)
    row_start = row_end - padded
    seg_dst = row_start[None, :] + jnp.cumsum(seg, axis=0) - seg
    seg_off = jnp.cumsum(seg, axis=1) - seg
    seg_src = (jnp.arange(nt, dtype=jnp.int32) * SORTED_TILE_ROWS)[:, None] + seg_off
    n_blk = -(-(2 * n + (g - 1) * N_EXPERTS * nt) // r) + N_EXPERTS

    blk_row0 = jnp.arange(n_blk, dtype=jnp.int32) * r
    blk_expert = jnp.minimum(
        jnp.sum((row_end[None, :] <= blk_row0[:, None]).astype(jnp.int32), axis=1), N_EXPERTS - 1)
    n_used = row_end[-1:] // r
    eid = jnp.arange(N_EXPERTS, dtype=jnp.int32)
    blk_onehot = (blk_expert[:, None] == eid[None, :]).astype(jnp.int32)

    def of_block(tab):
        return jnp.sum(blk_onehot[:, :, None] * tab.T[None, :, :], axis=1)
    dst_b, src_b, len_b = of_block(seg_dst)[:, None, :], of_block(seg_src)[:, None, :], of_block(seg)[:, None, :]
    grp_first = (blk_row0[:, None] + jnp.arange(r // g, dtype=jnp.int32)[None, :] * g)[:, :, None]
    inside = (grp_first >= dst_b) & (grp_first < dst_b + len_b)
    grp_row = jnp.where(jnp.any(inside, axis=2),
                        jnp.sum(jnp.where(inside, src_b + grp_first - dst_b, 0), axis=2),
                        SORTED_TILE_ROWS - g)

    tile_first = (jnp.arange(SORTED_TILE_ROWS // g, dtype=jnp.int32) * g)[None, :, None]
    t_off, t_len, t_dst = seg_off[:, None, :], seg[:, None, :], seg_dst[:, None, :]
    within = (tile_first >= t_off) & (tile_first < t_off + t_len)
    tile_grp = jnp.sum(jnp.where(within, t_dst + tile_first - t_off, 0), axis=2)

    nonempty = rows_e > 0
    later = jnp.where((eid[None, :] > eid[:, None]) & nonempty[None, :], eid[None, :], N_EXPERTS)
    next_expert = jnp.min(later, axis=1)
    next_expert = jnp.where(next_expert == N_EXPERTS, -1, next_expert)
    ordinal = jnp.cumsum(nonempty.astype(jnp.int32)) - 1
    blk_first = jnp.concatenate(
        [jnp.ones((1,), jnp.int32), (blk_expert[1:] != blk_expert[:-1]).astype(jnp.int32)])
    blk_next = jnp.sum(blk_onehot * next_expert[None, :], axis=1)
    blk_slot = jnp.sum(blk_onehot * ordinal[None, :], axis=1) % 2

    yb = _experts(blk_expert, blk_first, blk_next, blk_slot, n_used, grp_row, xs, w_gate, w_up, w_down)
    out = _combine(tile_grp, x1, p.reshape(n, PLE_DIM), pos, w_ple_gate.astype(BF16),
                   w_ple_proj.astype(BF16), ln2_g.reshape(1, d), ln2_b.reshape(1, d), yb)
    return out.reshape(b, s, d)


def kernel(x, p, w_in, w_conv, w_out, ln1_g, ln1_b, w_router_g, b_router_g, w_router_e,
           b_router_e, w_gate, w_up, w_down, w_ple_gate, w_ple_proj, ln2_g, ln2_b):
    for i in range(DEPTH):
        x = _layer(x, p[i], w_in[i], w_conv[i], w_out[i], ln1_g[i], ln1_b[i], w_router_g[i],
                   b_router_g[i], w_router_e[i], b_router_e[i], w_gate[i], w_up[i], w_down[i],
                   w_ple_gate[i], w_ple_proj[i], ln2_g[i], ln2_b[i])
    return x
```

```python
import functools

import jax
import jax.numpy as jnp
from jax import lax
from jax.experimental import pallas as pl
from jax.experimental.pallas import tpu as pltpu

D_MODEL = 1024
D_CONV = 512
N_HEADS = 8
HEAD_DIM = 64
D_ATTN = N_HEADS * HEAD_DIM
D_IN = 3 * D_CONV + 3 * D_ATTN
MOBA_BLOCK = 256
MOBA_TOPK = 3
N_GROUPS = 4
EXPERTS_PER_GROUP = 8
N_EXPERTS = N_GROUPS * EXPERTS_PER_GROUP
D_EXPERT = 512
DISPATCH_BLOCK = 256
ROW_GROUP = 8
PLE_DIM = 256
LN_EPS = 1e-5
DEPTH = 1
DEEPNORM_ALPHA = (2 * DEPTH) ** 0.25

LANES = 128
PROJ_ROWS = 512
TOKEN_TILE = 256
SORTED_TILE_ROWS = 2 * TOKEN_TILE + ROW_GROUP * N_EXPERTS
UNSORT_CHUNK = 256
SORTED_ROW_WIDTH = D_MODEL + LANES
KV_CHUNK_SHIFT = 2
KV_CHUNK_BLOCKS = 1 << KV_CHUNK_SHIFT
MOBA_HEADS_PER_STEP = 4
BF16_SUBLANES = 16
ACC_ROWS = HEAD_DIM + BF16_SUBLANES
Q_SCALE = HEAD_DIM ** -0.5 * 1.4426950408889634
VMEM_LIMIT = 48 * 1024 * 1024

F32 = jnp.float32
BF16 = jnp.bfloat16
NEG_INF = float("-inf")


def _dot(a, b):
    return jnp.dot(a, b, preferred_element_type=F32)


def _dot_nt(a, b):
    return lax.dot_general(a, b, (((1,), (1,)), ((), ())), preferred_element_type=F32)


def _layer_norm(h, g, b):
    mu = jnp.mean(h, axis=-1, keepdims=True)
    d = h - mu
    var = jnp.mean(d * d, axis=-1, keepdims=True)
    return d * lax.rsqrt(var + LN_EPS) * g + b


def _proj_conv_kernel(x_ref, w_ref, wc_ref, yc_ref, q_ref, k_ref, vt_ref, km_ref, ubuf):
    s = pl.program_id(1)
    rows = x_ref.shape[0]
    xb = x_ref[...].astype(BF16)

    def sect(i):
        return _dot(xb, w_ref[:, i * D_CONV:(i + 1) * D_CONV])

    @pl.when(s == 0)
    def _():
        ubuf[0:8, :] = jnp.zeros((8, D_CONV), F32)

    u = sect(1) * sect(2)
    ubuf[8:8 + rows, :] = u
    wc = wc_ref[...]
    conv = wc[0:1, :] * ubuf[6:6 + rows, :] + wc[1:2, :] * ubuf[7:7 + rows, :] + wc[2:3, :] * u
    yc_ref[...] = (sect(0) * conv).astype(BF16)
    ubuf[0:8, :] = ubuf[rows:rows + 8, :]

    q_ref[...] = (sect(3) * Q_SCALE).astype(BF16)
    k = sect(4)
    k_ref[...] = k.astype(BF16)
    for i in range(rows // MOBA_BLOCK):
        km_ref[i] = jnp.mean(k[i * MOBA_BLOCK:(i + 1) * MOBA_BLOCK, :], axis=0, keepdims=True)
    vt = sect(5).T
    ones = jnp.ones((ACC_ROWS - HEAD_DIM, rows), F32)
    vt = jnp.concatenate(
        [piece for h in range(N_HEADS) for piece in (vt[h * HEAD_DIM:(h + 1) * HEAD_DIM, :], ones)],
        axis=0).astype(BF16)
    for i in range(rows // MOBA_BLOCK):
        vt_ref[i] = vt[:, i * MOBA_BLOCK:(i + 1) * MOBA_BLOCK]


def _proj_conv(x, w_in_bf, w_conv):
    b, s, _ = x.shape
    nb = s // MOBA_BLOCK
    grid = (b, s // PROJ_ROWS)
    seq_spec = pl.BlockSpec((None, PROJ_ROWS, D_CONV), lambda i, j: (i, j, 0))
    return pl.pallas_call(
        _proj_conv_kernel,
        grid=grid,
        in_specs=[
            pl.BlockSpec((None, PROJ_ROWS, D_MODEL), lambda i, j: (i, j, 0)),
            pl.BlockSpec((D_MODEL, D_IN), lambda i, j: (0, 0)),
            pl.BlockSpec((3, D_CONV), lambda i, j: (0, 0)),
        ],
        out_specs=[
            seq_spec, seq_spec, seq_spec,
            pl.BlockSpec((None, PROJ_ROWS // MOBA_BLOCK, N_HEADS * ACC_ROWS, MOBA_BLOCK),
                         lambda i, j: (i, j, 0, 0)),
            pl.BlockSpec((None, PROJ_ROWS // MOBA_BLOCK, 1, D_ATTN), lambda i, j: (i, j, 0, 0)),
        ],
        out_shape=[
            jax.ShapeDtypeStruct((b, s, D_CONV), BF16),
            jax.ShapeDtypeStruct((b, s, D_ATTN), BF16),
            jax.ShapeDtypeStruct((b, s, D_ATTN), BF16),
            jax.ShapeDtypeStruct((b, nb, N_HEADS * ACC_ROWS, MOBA_BLOCK), BF16),
            jax.ShapeDtypeStruct((b, nb, 1, D_ATTN), F32),
        ],
        scratch_shapes=[pltpu.VMEM((PROJ_ROWS + 8, D_CONV), F32)],
        compiler_params=pltpu.CompilerParams(
            dimension_semantics=("arbitrary", "arbitrary"), vmem_limit_bytes=VMEM_LIMIT),
        name="proj_conv",
    )(x, w_in_bf, w_conv)


def _moba_kernel(q_ref, k_ref, vt_ref, km_ref, o_ref, bias_ref, sd_ref, s_ref, mt_ref, m_ref,
                 acc_ref):
    j = pl.program_id(2)
    blk = MOBA_BLOCK
    nb = km_ref.shape[0]
    n_heads = vt_ref.shape[1] // ACC_ROWS
    heads = range(n_heads)
    n_chunks = lax.shift_right_logical(j + (KV_CHUNK_BLOCKS - 1), KV_CHUNK_SHIFT)

    def pair_lanes(h):
        return slice((h // 2) * LANES, (h // 2 + 1) * LANES)

    row = lax.broadcasted_iota(jnp.int32, (LANES, blk), 0)
    qt_pairs = [q_ref[:, pair_lanes(2 * pr)].astype(F32).T for pr in range(n_heads // 2)]
    qt_heads = [jnp.where((row >= HEAD_DIM) == bool(h % 2), qt_pairs[h // 2], 0.0).astype(BF16)
                for h in heads]

    bidx = lax.broadcasted_iota(jnp.int32, (nb, blk), 0)
    for h in heads:
        km = km_ref[:, pair_lanes(h)]
        km_hi = km.astype(BF16)
        km_lo = (km - km_hi.astype(F32)).astype(BF16)
        g = _dot(km_hi, qt_heads[h]) + _dot(km_lo, qt_heads[h])
        g = jnp.where(bidx < j, g, NEG_INF)
        sel = jnp.zeros((nb, blk), jnp.bool_)
        for _ in range(MOBA_TOPK):
            mx = jnp.max(g, axis=0, keepdims=True)
            cand = jnp.where((g == mx) & (mx > NEG_INF), bidx, nb)
            pick = bidx == jnp.min(cand, axis=0, keepdims=True)
            sel = sel | pick
            g = jnp.where(pick, NEG_INF, g)
        bias_ref[h] = jnp.where(sel, 0.0, NEG_INF)
        m_ref[h] = jnp.full((1, blk), NEG_INF, F32)
        acc_ref[h] = jnp.zeros(acc_ref.shape[1:], F32)

    def k_block(n, h):
        return k_ref[pl.ds(pl.multiple_of(n * blk, blk), blk), pair_lanes(h)]

    def vt_block(n, h):
        return vt_ref[n, h * ACC_ROWS:(h + 1) * ACC_ROWS, :]

    def stage_block(c, u, slot):
        n = c * KV_CHUNK_BLOCKS + u
        mts = []
        for h in heads:
            sc = _dot(k_block(n, h), qt_heads[h]) + bias_ref[h, pl.ds(n, 1), :]
            s_ref[slot, h, u * blk:(u + 1) * blk, :] = sc
            mts.append(jnp.max(sc, axis=0, keepdims=True))
        return mts

    def pv(vt_h, sc, m_new):
        return _dot(vt_h, jnp.exp2(sc - m_new).astype(BF16))

    def rescale(h, mt):
        m_old = m_ref[h]
        m_new = jnp.maximum(m_old, mt)
        m_ref[h] = m_new
        return m_new, jnp.exp2(m_old - m_new)

    def step(consume, stage):
        if consume is not None:
            cc, cs = consume
            scale = [rescale(h, mt_ref[1 + cs, h]) for h in heads]
            parts = [None] * n_heads
        mts = []
        for u in range(KV_CHUNK_BLOCKS):
            if stage is not None:
                mts.append(stage_block(stage[0], u, stage[1]))
            if consume is not None:
                for h in heads:
                    r = pv(vt_block(cc * KV_CHUNK_BLOCKS + u, h),
                           s_ref[cs, h, u * blk:(u + 1) * blk, :], scale[h][0])
                    parts[h] = r if parts[h] is None else parts[h] + r
        for h in heads:
            if consume is not None:
                acc_ref[h] = scale[h][1] * acc_ref[h] + parts[h]
            if stage is not None:
                mt_ref[1 + stage[1], h] = functools.reduce(jnp.maximum, [m[h] for m in mts])

    kpos = lax.broadcasted_iota(jnp.int32, (blk, blk), 0)
    qpos = lax.broadcasted_iota(jnp.int32, (blk, blk), 1)
    causal = kpos <= qpos
    for h in heads:
        sc = jnp.where(causal, _dot(k_block(j, h), qt_heads[h]), NEG_INF)
        sd_ref[h] = sc
        mt_ref[0, h] = jnp.max(sc, axis=0, keepdims=True)
    step(None, (0, 0))
    for h in heads:
        m_new, alpha = rescale(h, mt_ref[0, h])
        acc_ref[h] = alpha * acc_ref[h] + pv(vt_block(j, h), sd_ref[h], m_new)

    steps = jnp.maximum(n_chunks - 1, 0)

    def body(i, carry):
        c = 2 * i
        step((c, 0), (c + 1, 1))
        step((c + 1, 1), (c + 2, 0))
        return carry

    lax.fori_loop(0, lax.shift_right_logical(steps, 1), body, 0)

    @pl.when((n_chunks > 0) & (steps % 2 == 1))
    def _():
        step((n_chunks - 2, 0), (n_chunks - 1, 1))
        step((n_chunks - 1, 1), None)

    @pl.when((n_chunks > 0) & (steps % 2 == 0))
    def _():
        step((n_chunks - 1, 0), None)

    outs = [acc_ref[h, 0:HEAD_DIM, :] / acc_ref[h, HEAD_DIM:HEAD_DIM + 1, :] for h in heads]
    o_ref[...] = jnp.concatenate(outs, axis=0).T.astype(BF16)


def _moba(q, k, vt, kmean):
    b, s, _ = q.shape
    nb = s // MOBA_BLOCK
    w = MOBA_HEADS_PER_STEP * HEAD_DIM
    hs = MOBA_HEADS_PER_STEP
    return pl.pallas_call(
        _moba_kernel,
        grid=(b, D_ATTN // w, nb),
        in_specs=[
            pl.BlockSpec((None, MOBA_BLOCK, w), lambda i, p, j: (i, j, p)),
            pl.BlockSpec((None, s, w), lambda i, p, j: (i, 0, p)),
            pl.BlockSpec((None, nb, hs * ACC_ROWS, MOBA_BLOCK), lambda i, p, j: (i, 0, p, 0)),
            pl.BlockSpec((None, nb, w), lambda i, p, j: (i, 0, p)),
        ],
        out_specs=pl.BlockSpec((None, MOBA_BLOCK, w), lambda i, p, j: (i, j, p)),
        out_shape=jax.ShapeDtypeStruct((b, s, D_ATTN), BF16),
        scratch_shapes=[
            pltpu.VMEM((hs, nb, MOBA_BLOCK), F32),
            pltpu.VMEM((hs, MOBA_BLOCK, MOBA_BLOCK), F32),
            pltpu.VMEM((2, hs, KV_CHUNK_BLOCKS * MOBA_BLOCK, MOBA_BLOCK), F32),
            pltpu.VMEM((3, hs, 1, MOBA_BLOCK), F32),
            pltpu.VMEM((hs, 1, MOBA_BLOCK), F32),
            pltpu.VMEM((hs, ACC_ROWS, MOBA_BLOCK), F32),
        ],
        compiler_params=pltpu.CompilerParams(
            dimension_semantics=("arbitrary", "arbitrary", "arbitrary"),
            vmem_limit_bytes=VMEM_LIMIT),
        name="moba",
    )(q, k, vt, kmean)


GROUP_LANE0 = N_EXPERTS


def _split3(col, lane):
    hi = col.astype(BF16).astype(F32)
    mid = (col - hi).astype(BF16).astype(F32)
    lo = col - hi - mid
    return jnp.where(lane == 0, hi, jnp.where(lane == 1, mid, jnp.where(lane == 2, lo, 0.0))).astype(BF16)


def _mix_route_kernel(yc_ref, ya_ref, x_ref, wo_ref, g_ref, b_ref, wr_ref, br_ref,
                      x1_ref, xs_ref, pos_ref, cnt_ref):
    t = x_ref.shape[0]

    mix = _dot(yc_ref[...], wo_ref[0:D_CONV, :]) + _dot(ya_ref[...], wo_ref[D_CONV:, :])
    x1 = _layer_norm(DEEPNORM_ALPHA * x_ref[...] + mix, g_ref[...], b_ref[...])
    x1_ref[...] = x1

    x_hi = x1.astype(BF16)
    x_lo = (x1 - x_hi.astype(F32)).astype(BF16)
    wr = wr_ref[...]
    w_hi = wr.astype(BF16)
    w_lo = (wr - w_hi.astype(F32)).astype(BF16)
    logits = _dot(x_hi, w_hi) + _dot(x_lo, w_hi) + _dot(x_hi, w_lo) + br_ref[...]

    lane = lax.broadcasted_iota(jnp.int32, (t, LANES), 1).astype(F32)
    big = float(LANES)

    lg = jnp.where((lane >= GROUP_LANE0) & (lane < GROUP_LANE0 + N_GROUPS), logits, NEG_INF)
    gmax = jnp.max(lg, axis=-1, keepdims=True)
    g_p = 1.0 / jnp.sum(jnp.exp(lg - gmax), axis=-1, keepdims=True)
    g_idx = jnp.min(jnp.where(lg == gmax, lane, big), axis=-1, keepdims=True) - GROUP_LANE0

    lo = g_idx * EXPERTS_PER_GROUP
    le = jnp.where((lane >= lo) & (lane < lo + EXPERTS_PER_GROUP), logits, NEG_INF)
    m1 = jnp.max(le, axis=-1, keepdims=True)
    i1 = jnp.min(jnp.where(le == m1, lane, big), axis=-1, keepdims=True)
    le2 = jnp.where(lane == i1, NEG_INF, le)
    m2 = jnp.max(le2, axis=-1, keepdims=True)
    i2 = jnp.min(jnp.where(le2 == m2, lane, big), axis=-1, keepdims=True)
    p2 = jnp.exp(m2 - m1)
    den = 1.0 + p2
    w1 = g_p * (1.0 / den)
    w2 = g_p * (p2 / den)

    oh = (lane == i1) | (lane == i2)
    ohf = oh.astype(F32)
    r_i = lax.broadcasted_iota(jnp.int32, (t, t), 0)
    c_i = lax.broadcasted_iota(jnp.int32, (t, t), 1)
    before = _dot((r_i > c_i).astype(BF16), ohf.astype(BF16))
    r1 = jnp.sum(jnp.where(lane == i1, before, 0.0), axis=-1, keepdims=True)
    r2 = jnp.sum(jnp.where(lane == i2, before, 0.0), axis=-1, keepdims=True)

    cnt = jnp.sum(ohf, axis=0, keepdims=True)
    seg = jnp.floor((cnt + (ROW_GROUP - 1)) * (1.0 / ROW_GROUP)) * ROW_GROUP
    e_a = lax.broadcasted_iota(jnp.int32, (LANES, LANES), 0)
    e_b = lax.broadcasted_iota(jnp.int32, (LANES, LANES), 1)
    seg_start = _dot(jnp.broadcast_to(seg, (8, LANES)).astype(BF16), (e_a < e_b).astype(BF16))[0:1, :]
    pos1 = jnp.sum(jnp.where(lane == i1, seg_start, 0.0), axis=-1, keepdims=True) + r1
    pos2 = jnp.sum(jnp.where(lane == i2, seg_start, 0.0), axis=-1, keepdims=True) + r2
    cnt_ref[...] = jnp.broadcast_to(cnt, cnt_ref.shape)

    pos = jnp.where(lane == 0, pos1, pos2)
    pos_ref[...] = pos
    pos_rows = pos.T[0:8, :]

    srow = lax.broadcasted_iota(jnp.int32, (xs_ref.shape[0], t), 0).astype(F32)
    perm1 = (srow == pos_rows[0:1, :]).astype(BF16)
    perm2 = (srow == pos_rows[1:2, :]).astype(BF16)
    xs_ref[:, 0:D_MODEL] = _dot(perm1 + perm2, x_hi)
    xs_ref[:, D_MODEL:] = (_dot(perm1, _split3(w1, lane)) + _dot(perm2, _split3(w2, lane)))


def _mix_route(yc, ya, x2d, w_out_bf, g, b, w_r, b_r):
    n = x2d.shape[0]
    t = TOKEN_TILE
    nt = n // t
    row = lambda w: pl.BlockSpec((t, w), lambda i: (i, 0))
    full = lambda a: pl.BlockSpec(a.shape, lambda i: (0, 0))
    return pl.pallas_call(
        _mix_route_kernel,
        grid=(nt,),
        in_specs=[row(D_CONV), row(D_ATTN), row(D_MODEL), full(w_out_bf), full(g), full(b),
                  full(w_r), full(b_r)],
        out_specs=[row(D_MODEL), pl.BlockSpec((SORTED_TILE_ROWS, SORTED_ROW_WIDTH), lambda i: (i, 0)),
                   row(LANES), pl.BlockSpec((8, LANES), lambda i: (i, 0))],
        out_shape=[
            jax.ShapeDtypeStruct((n, D_MODEL), F32),
            jax.ShapeDtypeStruct((nt * SORTED_TILE_ROWS, SORTED_ROW_WIDTH), F32),
            jax.ShapeDtypeStruct((n, LANES), F32),
            jax.ShapeDtypeStruct((nt * 8, LANES), F32),
        ],
        compiler_params=pltpu.CompilerParams(
            dimension_semantics=("arbitrary",), vmem_limit_bytes=VMEM_LIMIT),
        name="mix_route",
    )(yc, ya, x2d, w_out_bf, g, b, w_r, b_r)


def _row_copies_start(rows, src_of, dst_of, sem):
    for r in rows:
        pltpu.make_async_copy(src_of(r), dst_of(r), sem).start()


def _row_copies_wait(src_rows, dst_rows, sem):
    pltpu.make_async_copy(src_rows, dst_rows, sem).wait()


def _experts_kernel(be_ref, first_ref, nxt_ref, par_ref, nu_ref, cur_ref, nxg_ref, xs_hbm,
                    wg_hbm, wu_hbm, wd_hbm, yb_ref, xbuf, wg_buf, wu_buf, wd_buf, wg_bf, wu_bf, wd_bf,
                    xsem, wsem):
    i = pl.program_id(0)
    used = i < nu_ref[0]
    slot = par_ref[i]
    xslot = i % 2
    g = ROW_GROUP
    weights = ((wg_hbm, wg_buf), (wu_hbm, wu_buf), (wd_hbm, wd_buf))

    def fetch(e, s):
        for k, (w_hbm, w_buf) in enumerate(weights):
            pltpu.make_async_copy(w_hbm.at[e], w_buf.at[s], wsem.at[s, k]).start()

    def fetch_wait(s):
        for k, (w_hbm, w_buf) in enumerate(weights):
            pltpu.make_async_copy(w_hbm.at[0], w_buf.at[s], wsem.at[s, k]).wait()

    def gather_rows(grp_ref, s):
        _row_copies_start(
            range(DISPATCH_BLOCK // g),
            lambda q: xs_hbm.at[pl.ds(pl.multiple_of(grp_ref[0, q], g), g)],
            lambda q: xbuf.at[s, pl.ds(q * g, g)], xsem.at[s])

    @pl.when(i == 0)
    def _():
        fetch(be_ref[0], slot)
        gather_rows(cur_ref, 0)

    @pl.when(i + 1 < nu_ref[0])
    def _():
        gather_rows(nxg_ref, 1 - xslot)

    @pl.when(used & (first_ref[i] == 1))
    def _():
        fetch_wait(slot)

        @pl.when(nxt_ref[i] >= 0)
        def _():
            fetch(nxt_ref[i], 1 - slot)

        wg_bf[...] = wg_buf[slot].astype(BF16)
        wu_bf[...] = wu_buf[slot].astype(BF16)
        wd_bf[...] = wd_buf[slot].astype(BF16)

    @pl.when(used)
    def _():
        _row_copies_wait(xs_hbm.at[pl.ds(0, DISPATCH_BLOCK)], xbuf.at[xslot], xsem.at[xslot])
        xb = xbuf[xslot, :, 0:D_MODEL].astype(BF16)
        wrow = xbuf[xslot, :, D_MODEL:]
        weight = wrow[:, 0:1] + wrow[:, 1:2] + wrow[:, 2:3]
        gate = _dot(xb, wg_bf[...])
        up = _dot(xb, wu_bf[...])
        hid = (jax.nn.silu(gate) * up).astype(BF16)
        yb_ref[...] = _dot(hid, wd_bf[...]) * weight

    @pl.when(jnp.logical_not(used))
    def _():
        yb_ref[...] = jnp.zeros_like(yb_ref)


def _experts(blk_expert, blk_first, blk_next, blk_slot, n_used, grp_row, xs, w_gate, w_up, w_down):
    n_blk = blk_expert.shape[0]
    r = DISPATCH_BLOCK
    gpb = r // ROW_GROUP
    grp3 = grp_row.reshape(n_blk, 1, gpb)
    smem_blk = lambda f: pl.BlockSpec((None, 1, gpb), f, memory_space=pltpu.SMEM)
    grid_spec = pltpu.PrefetchScalarGridSpec(
        num_scalar_prefetch=5,
        grid=(n_blk,),
        in_specs=[
            smem_blk(lambda i, *_: (i, 0, 0)),
            smem_blk(lambda i, *_: (jnp.minimum(i + 1, n_blk - 1), 0, 0)),
            pl.BlockSpec(memory_space=pl.ANY),
            pl.BlockSpec(memory_space=pl.ANY),
            pl.BlockSpec(memory_space=pl.ANY),
            pl.BlockSpec(memory_space=pl.ANY),
        ],
        out_specs=pl.BlockSpec((r, D_MODEL), lambda i, *_: (i, 0)),
        scratch_shapes=[
            pltpu.VMEM((2, r, xs.shape[1]), F32),
            pltpu.VMEM((2,) + w_gate.shape[1:], w_gate.dtype),
            pltpu.VMEM((2,) + w_up.shape[1:], w_up.dtype),
            pltpu.VMEM((2,) + w_down.shape[1:], w_down.dtype),
            pltpu.VMEM(w_gate.shape[1:], BF16),
            pltpu.VMEM(w_up.shape[1:], BF16),
            pltpu.VMEM(w_down.shape[1:], BF16),
            pltpu.SemaphoreType.DMA((2,)),
            pltpu.SemaphoreType.DMA((2, 3)),
        ],
    )
    return pl.pallas_call(
        _experts_kernel,
        grid_spec=grid_spec,
        out_shape=jax.ShapeDtypeStruct((n_blk * r, D_MODEL), F32),
        compiler_params=pltpu.CompilerParams(
            dimension_semantics=("arbitrary",), vmem_limit_bytes=VMEM_LIMIT),
        name="experts",
    )(blk_expert, blk_first, blk_next, blk_slot, n_used, grp3, grp3, xs, w_gate, w_up, w_down)


def _combine_kernel(cur_ref, nxt_ref, x1_ref, p_ref, pos_ref, wpg_ref, wpp_ref, g_ref, b_ref,
                    yb_hbm, o_ref, ybuf, sem):
    i = pl.program_id(0)
    n_steps = pl.num_programs(0)
    t = x1_ref.shape[0]
    slot = i % 2
    g = ROW_GROUP
    n_groups = ybuf.shape[1] // g

    def gather(grp_ref, s):
        _row_copies_start(
            range(n_groups),
            lambda q: yb_hbm.at[pl.ds(pl.multiple_of(grp_ref[0, q], g), g)],
            lambda q: ybuf.at[s, pl.ds(q * g, g)], sem.at[s])

    @pl.when(i == 0)
    def _():
        gather(cur_ref, 0)

    @pl.when(i + 1 < n_steps)
    def _():
        gather(nxt_ref, 1 - slot)

    x1 = x1_ref[...]
    ple = jax.nn.sigmoid(_dot(x1.astype(BF16), wpg_ref[...])) * _dot(p_ref[...].astype(BF16), wpp_ref[...])

    _row_copies_wait(yb_hbm.at[pl.ds(0, ybuf.shape[1])], ybuf.at[slot], sem.at[slot])
    pos = pos_ref[...]
    scol = lax.broadcasted_iota(jnp.int32, (t, ybuf.shape[1]), 1).astype(F32)
    pick = ((scol == pos[:, 0:1]) | (scol == pos[:, 1:2])).astype(BF16)
    ffn = None
    for c in range(0, ybuf.shape[1], UNSORT_CHUNK):
        ys = ybuf[slot, c:c + UNSORT_CHUNK, :]
        hi = ys.astype(BF16)
        lo = (ys - hi.astype(F32)).astype(BF16)
        pc = pick[:, c:c + UNSORT_CHUNK]
        part = _dot(pc, hi) + _dot(pc, lo)
        ffn = part if ffn is None else ffn + part
    o_ref[...] = _layer_norm(DEEPNORM_ALPHA * x1 + ffn + ple, g_ref[...], b_ref[...])


def _combine(tile_grp, x1, p2d, pos, w_pg_bf, w_pp_bf, g, b, yb):
    n = x1.shape[0]
    t = TOKEN_TILE
    n_steps = n // t
    gpt = SORTED_TILE_ROWS // ROW_GROUP
    grp3 = tile_grp.reshape(n_steps, 1, gpt)
    smem_blk = lambda f: pl.BlockSpec((None, 1, gpt), f, memory_space=pltpu.SMEM)
    row = lambda w: pl.BlockSpec((t, w), lambda i: (i, 0))
    full = lambda a: pl.BlockSpec(a.shape, lambda i: (0, 0))
    return pl.pallas_call(
        _combine_kernel,
        grid=(n_steps,),
        in_specs=[
            smem_blk(lambda i: (i, 0, 0)),
            smem_blk(lambda i: (jnp.minimum(i + 1, n_steps - 1), 0, 0)),
            row(D_MODEL), row(PLE_DIM), row(LANES), full(w_pg_bf), full(w_pp_bf), full(g), full(b),
            pl.BlockSpec(memory_space=pl.ANY),
        ],
        out_specs=row(D_MODEL),
        out_shape=jax.ShapeDtypeStruct((n, D_MODEL), F32),
        scratch_shapes=[pltpu.VMEM((2, SORTED_TILE_ROWS, D_MODEL), F32), pltpu.SemaphoreType.DMA((2,))],
        compiler_params=pltpu.CompilerParams(
            dimension_semantics=("arbitrary",), vmem_limit_bytes=VMEM_LIMIT),
        name="combine",
    )(grp3, grp3, x1, p2d, pos, w_pg_bf, w_pp_bf, g, b, yb)


def _layer(x, p, w_in, w_conv, w_out, ln1_g, ln1_b, w_router_g, b_router_g, w_router_e,
           b_router_e, w_gate, w_up, w_down, w_ple_gate, w_ple_proj, ln2_g, ln2_b):
    b, s, d = x.shape
    n = b * s
    r = DISPATCH_BLOCK

    yc, q, k, vt, kmean = _proj_conv(x, w_in.astype(BF16), w_conv)
    ya = _moba(q, k, vt, kmean.reshape(b, s // MOBA_BLOCK, D_ATTN))

    w_r = jnp.zeros((d, LANES), F32).at[:, :N_EXPERTS].set(w_router_e)
    w_r = w_r.at[:, GROUP_LANE0:GROUP_LANE0 + N_GROUPS].set(w_router_g)
    b_r = jnp.zeros((1, LANES), F32).at[0, :N_EXPERTS].set(b_router_e)
    b_r = b_r.at[0, GROUP_LANE0:GROUP_LANE0 + N_GROUPS].set(b_router_g)
    x1, xs, pos, cnt = _mix_route(
        yc.reshape(n, D_CONV), ya.reshape(n, D_ATTN), x.reshape(n, d), w_out.astype(BF16),
        ln1_g.reshape(1, d), ln1_b.reshape(1, d), w_r, b_r)

    nt = n // TOKEN_TILE
    g = ROW_GROUP
    seg = (cnt.reshape(nt, 8, LANES)[:, 0, :N_EXPERTS].astype(jnp.int32) + g - 1) // g * g
    rows_e = jnp.sum(seg, axis=0)
    padded = (rows_e + r - 1) // r * r
    row_end = jnp.cumsum(padded)
    row_start = row_end - padded
    seg_dst = row_start[None, :] + jnp.cumsum(seg, axis=0) - seg
    seg_off = jnp.cumsum(seg, axis=1) - seg
    seg_src = (jnp.arange(nt, dtype=jnp.int32) * SORTED_TILE_ROWS)[:, None] + seg_off
    n_blk = -(-(2 * n + (g - 1) * N_EXPERTS * nt) // r) + N_EXPERTS

    blk_row0 = jnp.arange(n_blk, dtype=jnp.int32) * r
    blk_expert = jnp.minimum(
        jnp.sum((row_end[None, :] <= blk_row0[:, None]).astype(jnp.int32), axis=1), N_EXPERTS - 1)
    n_used = row_end[-1:] // r
    eid = jnp.arange(N_EXPERTS, dtype=jnp.int32)
    blk_onehot = (blk_expert[:, None] == eid[None, :]).astype(jnp.int32)

    def of_block(tab):
        return jnp.sum(blk_onehot[:, :, None] * tab.T[None, :, :], axis=1)
    dst_b, src_b, len_b = of_block(seg_dst)[:, None, :], of_block(seg_src)[:, None, :], of_block(seg)[:, None, :]
    grp_first = (blk_row0[:, None] + jnp.arange(r // g, dtype=jnp.int32)[None, :] * g)[:, :, None]
    inside = (grp_first >= dst_b) & (grp_first < dst_b + len_b)
    grp_row = jnp.where(jnp.any(inside, axis=2),
                        jnp.sum(jnp.where(inside, src_b + grp_first - dst_b, 0), axis=2),
                        SORTED_TILE_ROWS - g)

    tile_first = (jnp.arange(SORTED_TILE_ROWS // g, dtype=jnp.int32) * g)[None, :, None]
    t_off, t_len, t_dst = seg_off[:, None, :], seg[:, None, :], seg_dst[:, None, :]
    within = (tile_first >= t_off) & (tile_first < t_off + t_len)
    tile_grp = jnp.sum(jnp.where(within, t_dst + tile_first - t_off, 0), axis=2)

    nonempty = rows_e > 0
    later = jnp.where((eid[None, :] > eid[:, None]) & nonempty[None, :], eid[None, :], N_EXPERTS)
    next_expert = jnp.min(later, axis=1)
    next_expert = jnp.where(next_expert == N_EXPERTS, -1, next_expert)
    ordinal = jnp.cumsum(nonempty.astype(jnp.int32)) - 1
    blk_first = jnp.concatenate(
        [jnp.ones((1,), jnp.int32), (blk_expert[1:] != blk_expert[:-1]).astype(jnp.int32)])
    blk_next = jnp.sum(blk_onehot * next_expert[None, :], axis=1)
    blk_slot = jnp.sum(blk_onehot * ordinal[None, :], axis=1) % 2

    yb = _experts(blk_expert, blk_first, blk_next, blk_slot, n_used, grp_row, xs, w_gate, w_up, w_down)
    out = _combine(tile_grp, x1, p.reshape(n, PLE_DIM), pos, w_ple_gate.astype(BF16),
                   w_ple_proj.astype(BF16), ln2_g.reshape(1, d), ln2_b.reshape(1, d), yb)
    return out.reshape(b, s, d)


def kernel(x, p, w_in, w_conv, w_out, ln1_g, ln1_b, w_router_g, b_router_g, w_router_e,
           b_router_e, w_gate, w_up, w_down, w_ple_gate, w_ple_proj, ln2_g, ln2_b):
    for i in range(DEPTH):
        x = _layer(x, p[i], w_in[i], w_conv[i], w_out[i], ln1_g[i], ln1_b[i], w_router_g[i],
                   b_router_g[i], w_router_e[i], b_router_e[i], w_gate[i], w_up[i], w_down[i],
                   w_ple_gate[i], w_ple_proj[i], ln2_g[i], ln2_b[i])
    return x
```

```python
import functools

import jax
import jax.numpy as jnp
from jax import lax
from jax.experimental import pallas as pl
from jax.experimental.pallas import tpu as pltpu

D_MODEL = 1024
D_CONV = 512
N_HEADS = 8
HEAD_DIM = 64
D_ATTN = N_HEADS * HEAD_DIM
D_IN = 3 * D_CONV + 3 * D_ATTN
MOBA_BLOCK = 256
MOBA_TOPK = 3
N_GROUPS = 4
EXPERTS_PER_GROUP = 8
N_EXPERTS = N_GROUPS * EXPERTS_PER_GROUP
D_EXPERT = 512
DISPATCH_BLOCK = 512
ROW_GROUP = 8
PLE_DIM = 256
LN_EPS = 1e-5
DEPTH = 1
DEEPNORM_ALPHA = (2 * DEPTH) ** 0.25

LANES = 128
PROJ_ROWS = 512
TOKEN_TILE = 256
SORTED_TILE_ROWS = 2 * TOKEN_TILE + ROW_GROUP * N_EXPERTS
UNSORT_CHUNK = 256
SORTED_ROW_WIDTH = D_MODEL + LANES
KV_CHUNK_SHIFT = 2
KV_CHUNK_BLOCKS = 1 << KV_CHUNK_SHIFT
MOBA_HEADS_PER_STEP = 4
BF16_SUBLANES = 16
ACC_ROWS = HEAD_DIM + BF16_SUBLANES
Q_SCALE = HEAD_DIM ** -0.5 * 1.4426950408889634
VMEM_LIMIT = 48 * 1024 * 1024

F32 = jnp.float32
BF16 = jnp.bfloat16
NEG_INF = float("-inf")


def _dot(a, b):
    return jnp.dot(a, b, preferred_element_type=F32)


def _dot_nt(a, b):
    return lax.dot_general(a, b, (((1,), (1,)), ((), ())), preferred_element_type=F32)


def _layer_norm(h, g, b):
    mu = jnp.mean(h, axis=-1, keepdims=True)
    d = h - mu
    var = jnp.mean(d * d, axis=-1, keepdims=True)
    return d * lax.rsqrt(var + LN_EPS) * g + b


def _proj_conv_kernel(x_ref, w_ref, wc_ref, yc_ref, q_ref, k_ref, vt_ref, km_ref, ubuf):
    s = pl.program_id(1)
    rows = x_ref.shape[0]
    xb = x_ref[...].astype(BF16)

    def sect(i):
        return _dot(xb, w_ref[:, i * D_CONV:(i + 1) * D_CONV])

    @pl.when(s == 0)
    def _():
        ubuf[0:8, :] = jnp.zeros((8, D_CONV), F32)

    u = sect(1) * sect(2)
    ubuf[8:8 + rows, :] = u
    wc = wc_ref[...]
    conv = wc[0:1, :] * ubuf[6:6 + rows, :] + wc[1:2, :] * ubuf[7:7 + rows, :] + wc[2:3, :] * u
    yc_ref[...] = (sect(0) * conv).astype(BF16)
    ubuf[0:8, :] = ubuf[rows:rows + 8, :]

    q_ref[...] = (sect(3) * Q_SCALE).astype(BF16)
    k = sect(4)
    k_ref[...] = k.astype(BF16)
    for i in range(rows // MOBA_BLOCK):
        km_ref[i] = jnp.mean(k[i * MOBA_BLOCK:(i + 1) * MOBA_BLOCK, :], axis=0, keepdims=True)
    vt = sect(5).T
    ones = jnp.ones((ACC_ROWS - HEAD_DIM, rows), F32)
    vt = jnp.concatenate(
        [piece for h in range(N_HEADS) for piece in (vt[h * HEAD_DIM:(h + 1) * HEAD_DIM, :], ones)],
        axis=0).astype(BF16)
    for i in range(rows // MOBA_BLOCK):
        vt_ref[i] = vt[:, i * MOBA_BLOCK:(i + 1) * MOBA_BLOCK]


def _proj_conv(x, w_in_bf, w_conv):
    b, s, _ = x.shape
    nb = s // MOBA_BLOCK
    grid = (b, s // PROJ_ROWS)
    seq_spec = pl.BlockSpec((None, PROJ_ROWS, D_CONV), lambda i, j: (i, j, 0))
    return pl.pallas_call(
        _proj_conv_kernel,
        grid=grid,
        in_specs=[
            pl.BlockSpec((None, PROJ_ROWS, D_MODEL), lambda i, j: (i, j, 0)),
            pl.BlockSpec((D_MODEL, D_IN), lambda i, j: (0, 0)),
            pl.BlockSpec((3, D_CONV), lambda i, j: (0, 0)),
        ],
        out_specs=[
            seq_spec, seq_spec, seq_spec,
            pl.BlockSpec((None, PROJ_ROWS // MOBA_BLOCK, N_HEADS * ACC_ROWS, MOBA_BLOCK),
                         lambda i, j: (i, j, 0, 0)),
            pl.BlockSpec((None, PROJ_ROWS // MOBA_BLOCK, 1, D_ATTN), lambda i, j: (i, j, 0, 0)),
        ],
        out_shape=[
            jax.ShapeDtypeStruct((b, s, D_CONV), BF16),
            jax.ShapeDtypeStruct((b, s, D_ATTN), BF16),
            jax.ShapeDtypeStruct((b, s, D_ATTN), BF16),
            jax.ShapeDtypeStruct((b, nb, N_HEADS * ACC_ROWS, MOBA_BLOCK), BF16),
            jax.ShapeDtypeStruct((b, nb, 1, D_ATTN), F32),
        ],
        scratch_shapes=[pltpu.VMEM((PROJ_ROWS + 8, D_CONV), F32)],
        compiler_params=pltpu.CompilerParams(
            dimension_semantics=("arbitrary", "arbitrary"), vmem_limit_bytes=VMEM_LIMIT),
        name="proj_conv",
    )(x, w_in_bf, w_conv)


def _moba_kernel(q_ref, k_ref, vt_ref, km_ref, o_ref, bias_ref, sd_ref, s_ref, mt_ref, m_ref,
                 acc_ref):
    j = pl.program_id(2)
    blk = MOBA_BLOCK
    nb = km_ref.shape[0]
    n_heads = vt_ref.shape[1] // ACC_ROWS
    heads = range(n_heads)
    n_chunks = lax.shift_right_logical(j + (KV_CHUNK_BLOCKS - 1), KV_CHUNK_SHIFT)

    def pair_lanes(h):
        return slice((h // 2) * LANES, (h // 2 + 1) * LANES)

    row = lax.broadcasted_iota(jnp.int32, (LANES, blk), 0)
    qt_pairs = [q_ref[:, pair_lanes(2 * pr)].astype(F32).T for pr in range(n_heads // 2)]
    qt_heads = [jnp.where((row >= HEAD_DIM) == bool(h % 2), qt_pairs[h // 2], 0.0).astype(BF16)
                for h in heads]

    bidx = lax.broadcasted_iota(jnp.int32, (nb, blk), 0)
    for h in heads:
        km = km_ref[:, pair_lanes(h)]
        km_hi = km.astype(BF16)
        km_lo = (km - km_hi.astype(F32)).astype(BF16)
        g = _dot(km_hi, qt_heads[h]) + _dot(km_lo, qt_heads[h])
        g = jnp.where(bidx < j, g, NEG_INF)
        sel = jnp.zeros((nb, blk), jnp.bool_)
        for _ in range(MOBA_TOPK):
            mx = jnp.max(g, axis=0, keepdims=True)
            cand = jnp.where((g == mx) & (mx > NEG_INF), bidx, nb)
            pick = bidx == jnp.min(cand, axis=0, keepdims=True)
            sel = sel | pick
            g = jnp.where(pick, NEG_INF, g)
        bias_ref[h] = jnp.where(sel, 0.0, NEG_INF)
        m_ref[h] = jnp.full((1, blk), NEG_INF, F32)
        acc_ref[h] = jnp.zeros(acc_ref.shape[1:], F32)

    def k_block(n, h):
        return k_ref[pl.ds(pl.multiple_of(n * blk, blk), blk), pair_lanes(h)]

    def vt_block(n, h):
        return vt_ref[n, h * ACC_ROWS:(h + 1) * ACC_ROWS, :]

    def stage_block(c, u, slot):
        n = c * KV_CHUNK_BLOCKS + u
        mts = []
        for h in heads:
            sc = _dot(k_block(n, h), qt_heads[h]) + bias_ref[h, pl.ds(n, 1), :]
            s_ref[slot, h, u * blk:(u + 1) * blk, :] = sc
            mts.append(jnp.max(sc, axis=0, keepdims=True))
        return mts

    def pv(vt_h, sc, m_new):
        return _dot(vt_h, jnp.exp2(sc - m_new).astype(BF16))

    def rescale(h, mt):
        m_old = m_ref[h]
        m_new = jnp.maximum(m_old, mt)
        m_ref[h] = m_new
        return m_new, jnp.exp2(m_old - m_new)

    def step(consume, stage):
        if consume is not None:
            cc, cs = consume
            scale = [rescale(h, mt_ref[1 + cs, h]) for h in heads]
            parts = [None] * n_heads
        mts = []
        for u in range(KV_CHUNK_BLOCKS):
            if stage is not None:
                mts.append(stage_block(stage[0], u, stage[1]))
            if consume is not None:
                for h in heads:
                    r = pv(vt_block(cc * KV_CHUNK_BLOCKS + u, h),
                           s_ref[cs, h, u * blk:(u + 1) * blk, :], scale[h][0])
                    parts[h] = r if parts[h] is None else parts[h] + r
        for h in heads:
            if consume is not None:
                acc_ref[h] = scale[h][1] * acc_ref[h] + parts[h]
            if stage is not None:
                mt_ref[1 + stage[1], h] = functools.reduce(jnp.maximum, [m[h] for m in mts])

    kpos = lax.broadcasted_iota(jnp.int32, (blk, blk), 0)
    qpos = lax.broadcasted_iota(jnp.int32, (blk, blk), 1)
    causal = kpos <= qpos
    for h in heads:
        sc = jnp.where(causal, _dot(k_block(j, h), qt_heads[h]), NEG_INF)
        sd_ref[h] = sc
        mt_ref[0, h] = jnp.max(sc, axis=0, keepdims=True)
    step(None, (0, 0))
    for h in heads:
        m_new, alpha = rescale(h, mt_ref[0, h])
        acc_ref[h] = alpha * acc_ref[h] + pv(vt_block(j, h), sd_ref[h], m_new)

    steps = jnp.maximum(n_chunks - 1, 0)

    def body(i, carry):
        c = 2 * i
        step((c, 0), (c + 1, 1))
        step((c + 1, 1), (c + 2, 0))
        return carry

    lax.fori_loop(0, lax.shift_right_logical(steps, 1), body, 0)

    @pl.when((n_chunks > 0) & (steps % 2 == 1))
    def _():
        step((n_chunks - 2, 0), (n_chunks - 1, 1))
        step((n_chunks - 1, 1), None)

    @pl.when((n_chunks > 0) & (steps % 2 == 0))
    def _():
        step((n_chunks - 1, 0), None)

    outs = [acc_ref[h, 0:HEAD_DIM, :] / acc_ref[h, HEAD_DIM:HEAD_DIM + 1, :] for h in heads]
    o_ref[...] = jnp.concatenate(outs, axis=0).T.astype(BF16)


def _moba(q, k, vt, kmean):
    b, s, _ = q.shape
    nb = s // MOBA_BLOCK
    w = MOBA_HEADS_PER_STEP * HEAD_DIM
    hs = MOBA_HEADS_PER_STEP
    return pl.pallas_call(
        _moba_kernel,
        grid=(b, D_ATTN // w, nb),
        in_specs=[
            pl.BlockSpec((None, MOBA_BLOCK, w), lambda i, p, j: (i, j, p)),
            pl.BlockSpec((None, s, w), lambda i, p, j: (i, 0, p)),
            pl.BlockSpec((None, nb, hs * ACC_ROWS, MOBA_BLOCK), lambda i, p, j: (i, 0, p, 0)),
            pl.BlockSpec((None, nb, w), lambda i, p, j: (i, 0, p)),
        ],
        out_specs=pl.BlockSpec((None, MOBA_BLOCK, w), lambda i, p, j: (i, j, p)),
        out_shape=jax.ShapeDtypeStruct((b, s, D_ATTN), BF16),
        scratch_shapes=[
            pltpu.VMEM((hs, nb, MOBA_BLOCK), F32),
            pltpu.VMEM((hs, MOBA_BLOCK, MOBA_BLOCK), F32),
            pltpu.VMEM((2, hs, KV_CHUNK_BLOCKS * MOBA_BLOCK, MOBA_BLOCK), F32),
            pltpu.VMEM((3, hs, 1, MOBA_BLOCK), F32),
            pltpu.VMEM((hs, 1, MOBA_BLOCK), F32),
            pltpu.VMEM((hs, ACC_ROWS, MOBA_BLOCK), F32),
        ],
        compiler_params=pltpu.CompilerParams(
            dimension_semantics=("arbitrary", "arbitrary", "arbitrary"),
            vmem_limit_bytes=VMEM_LIMIT),
        name="moba",
    )(q, k, vt, kmean)


GROUP_LANE0 = N_EXPERTS


def _split3(col, lane):
    hi = col.astype(BF16).astype(F32)
    mid = (col - hi).astype(BF16).astype(F32)
    lo = col - hi - mid
    return jnp.where(lane == 0, hi, jnp.where(lane == 1, mid, jnp.where(lane == 2, lo, 0.0))).astype(BF16)


def _mix_route_kernel(yc_ref, ya_ref, x_ref, wo_ref, g_ref, b_ref, wr_ref, br_ref,
                      x1_ref, xs_ref, pos_ref, cnt_ref):
    t = x_ref.shape[0]

    mix = _dot(yc_ref[...], wo_ref[0:D_CONV, :]) + _dot(ya_ref[...], wo_ref[D_CONV:, :])
    x1 = _layer_norm(DEEPNORM_ALPHA * x_ref[...] + mix, g_ref[...], b_ref[...])
    x1_ref[...] = x1

    x_hi = x1.astype(BF16)
    x_lo = (x1 - x_hi.astype(F32)).astype(BF16)
    wr = wr_ref[...]
    w_hi = wr.astype(BF16)
    w_lo = (wr - w_hi.astype(F32)).astype(BF16)
    logits = _dot(x_hi, w_hi) + _dot(x_lo, w_hi) + _dot(x_hi, w_lo) + br_ref[...]

    lane = lax.broadcasted_iota(jnp.int32, (t, LANES), 1).astype(F32)
    big = float(LANES)

    lg = jnp.where((lane >= GROUP_LANE0) & (lane < GROUP_LANE0 + N_GROUPS), logits, NEG_INF)
    gmax = jnp.max(lg, axis=-1, keepdims=True)
    g_p = 1.0 / jnp.sum(jnp.exp(lg - gmax), axis=-1, keepdims=True)
    g_idx = jnp.min(jnp.where(lg == gmax, lane, big), axis=-1, keepdims=True) - GROUP_LANE0

    lo = g_idx * EXPERTS_PER_GROUP
    le = jnp.where((lane >= lo) & (lane < lo + EXPERTS_PER_GROUP), logits, NEG_INF)
    m1 = jnp.max(le, axis=-1, keepdims=True)
    i1 = jnp.min(jnp.where(le == m1, lane, big), axis=-1, keepdims=True)
    le2 = jnp.where(lane == i1, NEG_INF, le)
    m2 = jnp.max(le2, axis=-1, keepdims=True)
    i2 = jnp.min(jnp.where(le2 == m2, lane, big), axis=-1, keepdims=True)
    p2 = jnp.exp(m2 - m1)
    den = 1.0 + p2
    w1 = g_p * (1.0 / den)
    w2 = g_p * (p2 / den)

    oh = (lane == i1) | (lane == i2)
    ohf = oh.astype(F32)
    r_i = lax.broadcasted_iota(jnp.int32, (t, t), 0)
    c_i = lax.broadcasted_iota(jnp.int32, (t, t), 1)
    before = _dot((r_i > c_i).astype(BF16), ohf.astype(BF16))
    r1 = jnp.sum(jnp.where(lane == i1, before, 0.0), axis=-1, keepdims=True)
    r2 = jnp.sum(jnp.where(lane == i2, before, 0.0), axis=-1, keepdims=True)

    cnt = jnp.sum(ohf, axis=0, keepdims=True)
    seg = jnp.floor((cnt + (ROW_GROUP - 1)) * (1.0 / ROW_GROUP)) * ROW_GROUP
    e_a = lax.broadcasted_iota(jnp.int32, (LANES, LANES), 0)
    e_b = lax.broadcasted_iota(jnp.int32, (LANES, LANES), 1)
    seg_start = _dot(jnp.broadcast_to(seg, (8, LANES)).astype(BF16), (e_a < e_b).astype(BF16))[0:1, :]
    pos1 = jnp.sum(jnp.where(lane == i1, seg_start, 0.0), axis=-1, keepdims=True) + r1
    pos2 = jnp.sum(jnp.where(lane == i2, seg_start, 0.0), axis=-1, keepdims=True) + r2
    cnt_ref[...] = jnp.broadcast_to(cnt, cnt_ref.shape)

    pos = jnp.where(lane == 0, pos1, pos2)
    pos_ref[...] = pos
    pos_rows = pos.T[0:8, :]

    srow = lax.broadcasted_iota(jnp.int32, (xs_ref.shape[0], t), 0).astype(F32)
    perm1 = (srow == pos_rows[0:1, :]).astype(BF16)
    perm2 = (srow == pos_rows[1:2, :]).astype(BF16)
    xs_ref[:, 0:D_MODEL] = _dot(perm1 + perm2, x_hi)
    xs_ref[:, D_MODEL:] = (_dot(perm1, _split3(w1, lane)) + _dot(perm2, _split3(w2, lane)))


def _mix_route(yc, ya, x2d, w_out_bf, g, b, w_r, b_r):
    n = x2d.shape[0]
    t = TOKEN_TILE
    nt = n // t
    row = lambda w: pl.BlockSpec((t, w), lambda i: (i, 0))
    full = lambda a: pl.BlockSpec(a.shape, lambda i: (0, 0))
    return pl.pallas_call(
        _mix_route_kernel,
        grid=(nt,),
        in_specs=[row(D_CONV), row(D_ATTN), row(D_MODEL), full(w_out_bf), full(g), full(b),
                  full(w_r), full(b_r)],
        out_specs=[row(D_MODEL), pl.BlockSpec((SORTED_TILE_ROWS, SORTED_ROW_WIDTH), lambda i: (i, 0)),
                   row(LANES), pl.BlockSpec((8, LANES), lambda i: (i, 0))],
        out_shape=[
            jax.ShapeDtypeStruct((n, D_MODEL), F32),
            jax.ShapeDtypeStruct((nt * SORTED_TILE_ROWS, SORTED_ROW_WIDTH), F32),
            jax.ShapeDtypeStruct((n, LANES), F32),
            jax.ShapeDtypeStruct((nt * 8, LANES), F32),
        ],
        compiler_params=pltpu.CompilerParams(
            dimension_semantics=("arbitrary",), vmem_limit_bytes=VMEM_LIMIT),
        name="mix_route",
    )(yc, ya, x2d, w_out_bf, g, b, w_r, b_r)


def _row_copies_start(rows, src_of, dst_of, sem):
    for r in rows:
        pltpu.make_async_copy(src_of(r), dst_of(r), sem).start()


def _row_copies_wait(src_rows, dst_rows, sem):
    pltpu.make_async_copy(src_rows, dst_rows, sem).wait()


def _experts_kernel(be_ref, first_ref, nxt_ref, par_ref, nu_ref, cur_ref, nxg_ref, xs_hbm,
                    wg_hbm, wu_hbm, wd_hbm, yb_ref, xbuf, wg_buf, wu_buf, wd_buf, wg_bf, wu_bf, wd_bf,
                    xsem, wsem):
    i = pl.program_id(0)
    used = i < nu_ref[0]
    slot = par_ref[i]
    xslot = i % 2
    g = ROW_GROUP
    weights = ((wg_hbm, wg_buf), (wu_hbm, wu_buf), (wd_hbm, wd_buf))

    def fetch(e, s):
        for k, (w_hbm, w_buf) in enumerate(weights):
            pltpu.make_async_copy(w_hbm.at[e], w_buf.at[s], wsem.at[s, k]).start()

    def fetch_wait(s):
        for k, (w_hbm, w_buf) in enumerate(weights):
            pltpu.make_async_copy(w_hbm.at[0], w_buf.at[s], wsem.at[s, k]).wait()

    def gather_rows(grp_ref, s):
        _row_copies_start(
            range(DISPATCH_BLOCK // g),
            lambda q: xs_hbm.at[pl.ds(pl.multiple_of(grp_ref[0, q], g), g)],
            lambda q: xbuf.at[s, pl.ds(q * g, g)], xsem.at[s])

    @pl.when(i == 0)
    def _():
        fetch(be_ref[0], slot)
        gather_rows(cur_ref, 0)

    @pl.when(i + 1 < nu_ref[0])
    def _():
        gather_rows(nxg_ref, 1 - xslot)

    @pl.when(used & (first_ref[i] == 1))
    def _():
        fetch_wait(slot)

        @pl.when(nxt_ref[i] >= 0)
        def _():
            fetch(nxt_ref[i], 1 - slot)

        wg_bf[...] = wg_buf[slot].astype(BF16)
        wu_bf[...] = wu_buf[slot].astype(BF16)
        wd_bf[...] = wd_buf[slot].astype(BF16)

    @pl.when(used)
    def _():
        _row_copies_wait(xs_hbm.at[pl.ds(0, DISPATCH_BLOCK)], xbuf.at[xslot], xsem.at[xslot])
        xb = xbuf[xslot, :, 0:D_MODEL].astype(BF16)
        wrow = xbuf[xslot, :, D_MODEL:]
        weight = wrow[:, 0:1] + wrow[:, 1:2] + wrow[:, 2:3]
        gate = _dot(xb, wg_bf[...])
        up = _dot(xb, wu_bf[...])
        hid = (jax.nn.silu(gate) * up).astype(BF16)
        yb_ref[...] = _dot(hid, wd_bf[...]) * weight

    @pl.when(jnp.logical_not(used))
    def _():
        yb_ref[...] = jnp.zeros_like(yb_ref)


def _experts(blk_expert, blk_first, blk_next, blk_slot, n_used, grp_row, xs, w_gate, w_up, w_down):
    n_blk = blk_expert.shape[0]
    r = DISPATCH_BLOCK
    gpb = r // ROW_GROUP
    grp3 = grp_row.reshape(n_blk, 1, gpb)
    smem_blk = lambda f: pl.BlockSpec((None, 1, gpb), f, memory_space=pltpu.SMEM)
    grid_spec = pltpu.PrefetchScalarGridSpec(
        num_scalar_prefetch=5,
        grid=(n_blk,),
        in_specs=[
            smem_blk(lambda i, *_: (i, 0, 0)),
            smem_blk(lambda i, *_: (jnp.minimum(i + 1, n_blk - 1), 0, 0)),
            pl.BlockSpec(memory_space=pl.ANY),
            pl.BlockSpec(memory_space=pl.ANY),
            pl.BlockSpec(memory_space=pl.ANY),
            pl.BlockSpec(memory_space=pl.ANY),
        ],
        out_specs=pl.BlockSpec((r, D_MODEL), lambda i, *_: (i, 0)),
        scratch_shapes=[
            pltpu.VMEM((2, r, xs.shape[1]), F32),
            pltpu.VMEM((2,) + w_gate.shape[1:], w_gate.dtype),
            pltpu.VMEM((2,) + w_up.shape[1:], w_up.dtype),
            pltpu.VMEM((2,) + w_down.shape[1:], w_down.dtype),
            pltpu.VMEM(w_gate.shape[1:], BF16),
            pltpu.VMEM(w_up.shape[1:], BF16),
            pltpu.VMEM(w_down.shape[1:], BF16),
            pltpu.SemaphoreType.DMA((2,)),
            pltpu.SemaphoreType.DMA((2, 3)),
        ],
    )
    return pl.pallas_call(
        _experts_kernel,
        grid_spec=grid_spec,
        out_shape=jax.ShapeDtypeStruct((n_blk * r, D_MODEL), F32),
        compiler_params=pltpu.CompilerParams(
            dimension_semantics=("arbitrary",), vmem_limit_bytes=VMEM_LIMIT),
        name="experts",
    )(blk_expert, blk_first, blk_next, blk_slot, n_used, grp3, grp3, xs, w_gate, w_up, w_down)


def _combine_kernel(cur_ref, nxt_ref, x1_ref, p_ref, pos_ref, wpg_ref, wpp_ref, g_ref, b_ref,
                    yb_hbm, o_ref, ybuf, sem):
    i = pl.program_id(0)
    n_steps = pl.num_programs(0)
    t = x1_ref.shape[0]
    slot = i % 2
    g = ROW_GROUP
    n_groups = ybuf.shape[1] // g

    def gather(grp_ref, s):
        _row_copies_start(
            range(n_groups),
            lambda q: yb_hbm.at[pl.ds(pl.multiple_of(grp_ref[0, q], g), g)],
            lambda q: ybuf.at[s, pl.ds(q * g, g)], sem.at[s])

    @pl.when(i == 0)
    def _():
        gather(cur_ref, 0)

    @pl.when(i + 1 < n_steps)
    def _():
        gather(nxt_ref, 1 - slot)

    x1 = x1_ref[...]
    ple = jax.nn.sigmoid(_dot(x1.astype(BF16), wpg_ref[...])) * _dot(p_ref[...].astype(BF16), wpp_ref[...])

    _row_copies_wait(yb_hbm.at[pl.ds(0, ybuf.shape[1])], ybuf.at[slot], sem.at[slot])
    pos = pos_ref[...]
    scol = lax.broadcasted_iota(jnp.int32, (t, ybuf.shape[1]), 1).astype(F32)
    pick = ((scol == pos[:, 0:1]) | (scol == pos[:, 1:2])).astype(BF16)
    ffn = None
    for c in range(0, ybuf.shape[1], UNSORT_CHUNK):
        ys = ybuf[slot, c:c + UNSORT_CHUNK, :]
        hi = ys.astype(BF16)
        lo = (ys - hi.astype(F32)).astype(BF16)
        pc = pick[:, c:c + UNSORT_CHUNK]
        part = _dot(pc, hi) + _dot(pc, lo)
        ffn = part if ffn is None else ffn + part
    o_ref[...] = _layer_norm(DEEPNORM_ALPHA * x1 + ffn + ple, g_ref[...], b_ref[...])


def _combine(tile_grp, x1, p2d, pos, w_pg_bf, w_pp_bf, g, b, yb):
    n = x1.shape[0]
    t = TOKEN_TILE
    n_steps = n // t
    gpt = SORTED_TILE_ROWS // ROW_GROUP
    grp3 = tile_grp.reshape(n_steps, 1, gpt)
    smem_blk = lambda f: pl.BlockSpec((None, 1, gpt), f, memory_space=pltpu.SMEM)
    row = lambda w: pl.BlockSpec((t, w), lambda i: (i, 0))
    full = lambda a: pl.BlockSpec(a.shape, lambda i: (0, 0))
    return pl.pallas_call(
        _combine_kernel,
        grid=(n_steps,),
        in_specs=[
            smem_blk(lambda i: (i, 0, 0)),
            smem_blk(lambda i: (jnp.minimum(i + 1, n_steps - 1), 0, 0)),
            row(D_MODEL), row(PLE_DIM), row(LANES), full(w_pg_bf), full(w_pp_bf), full(g), full(b),
            pl.BlockSpec(memory_space=pl.ANY),
        ],
        out_specs=row(D_MODEL),
        out_shape=jax.ShapeDtypeStruct((n, D_MODEL), F32),
        scratch_shapes=[pltpu.VMEM((2, SORTED_TILE_ROWS, D_MODEL), F32), pltpu.SemaphoreType.DMA((2,))],
        compiler_params=pltpu.CompilerParams(
            dimension_semantics=("arbitrary",), vmem_limit_bytes=VMEM_LIMIT),
        name="combine",
    )(grp3, grp3, x1, p2d, pos, w_pg_bf, w_pp_bf, g, b, yb)


def _layer(x, p, w_in, w_conv, w_out, ln1_g, ln1_b, w_router_g, b_router_g, w_router_e,
           b_router_e, w_gate, w_up, w_down, w_ple_gate, w_ple_proj, ln2_g, ln2_b):
    b, s, d = x.shape
    n = b * s
    r = DISPATCH_BLOCK

    yc, q, k, vt, kmean = _proj_conv(x, w_in.astype(BF16), w_conv)
    ya = _moba(q, k, vt, kmean.reshape(b, s // MOBA_BLOCK, D_ATTN))

    w_r = jnp.zeros((d, LANES), F32).at[:, :N_EXPERTS].set(w_router_e)
    w_r = w_r.at[:, GROUP_LANE0:GROUP_LANE0 + N_GROUPS].set(w_router_g)
    b_r = jnp.zeros((1, LANES), F32).at[0, :N_EXPERTS].set(b_router_e)
    b_r = b_r.at[0, GROUP_LANE0:GROUP_LANE0 + N_GROUPS].set(b_router_g)
    x1, xs, pos, cnt = _mix_route(
        yc.reshape(n, D_CONV), ya.reshape(n, D_ATTN), x.reshape(n, d), w_out.astype(BF16),
        ln1_g.reshape(1, d), ln1_b.reshape(1, d), w_r, b_r)

    nt = n // TOKEN_TILE
    g = ROW_GROUP
    seg = (cnt.reshape(nt, 8, LANES)[:, 0, :N_EXPERTS].astype(jnp.int32) + g - 1) // g * g
    rows_e = jnp.sum(seg, axis=0)
    padded = (rows_e + r - 1) // r * r
    row_end = jnp.cumsum(padded)
    row_start = row_end - padded
    seg_dst = row_start[None, :] + jnp.cumsum(seg, axis=0) - seg
    seg_off = jnp.cumsum(seg, axis=1) - seg
    seg_src = (jnp.arange(nt, dtype=jnp.int32) * SORTED_TILE_ROWS)[:, None] + seg_off
    n_blk = -(-(2 * n + (g - 1) * N_EXPERTS * nt) // r) + N_EXPERTS

    blk_row0 = jnp.arange(n_blk, dtype=jnp.int32) * r
    blk_expert = jnp.minimum(
        jnp.sum((row_end[None, :] <= blk_row0[:, None]).astype(jnp.int32), axis=1), N_EXPERTS - 1)
    n_used = row_end[-1:] // r
    eid = jnp.arange(N_EXPERTS, dtype=jnp.int32)
    blk_onehot = (blk_expert[:, None] == eid[None, :]).astype(jnp.int32)

    def of_block(tab):
        return jnp.sum(blk_onehot[:, :, None] * tab.T[None, :, :], axis=1)
    dst_b, src_b, len_b = of_block(seg_dst)[:, None, :], of_block(seg_src)[:, None, :], of_block(seg)[:, None, :]
    grp_first = (blk_row0[:, None] + jnp.arange(r // g, dtype=jnp.int32)[None, :] * g)[:, :, None]
    inside = (grp_first >= dst_b) & (grp_first < dst_b + len_b)
    grp_row = jnp.where(jnp.any(inside, axis=2),
                        jnp.sum(jnp.where(inside, src_b + grp_first - dst_b, 0), axis=2),
                        SORTED_TILE_ROWS - g)

    tile_first = (jnp.arange(SORTED_TILE_ROWS // g, dtype=jnp.int32) * g)[None, :, None]
    t_off, t_len, t_dst = seg_off[:, None, :], seg[:, None, :], seg_dst[:, None, :]
    within = (tile_first >= t_off) & (tile_first < t_off + t_len)
    tile_grp = jnp.sum(jnp.where(within, t_dst + tile_first - t_off, 0), axis=2)

    nonempty = rows_e > 0
    later = jnp.where((eid[None, :] > eid[:, None]) & nonempty[None, :], eid[None, :], N_EXPERTS)
    next_expert = jnp.min(later, axis=1)
    next_expert = jnp.where(next_expert == N_EXPERTS, -1, next_expert)
    ordinal = jnp.cumsum(nonempty.astype(jnp.int32)) - 1
    blk_first = jnp.concatenate(
        [jnp.ones((1,), jnp.int32), (blk_expert[1:] != blk_expert[:-1]).astype(jnp.int32)])
    blk_next = jnp.sum(blk_onehot * next_expert[None, :], axis=1)
    blk_slot = jnp.sum(blk_onehot * ordinal[None, :], axis=1) % 2

    yb = _experts(blk_expert, blk_first, blk_next, blk_slot, n_used, grp_row, xs, w_gate, w_up, w_down)
    out = _combine(tile_grp, x1, p.reshape(n, PLE_DIM), pos, w_ple_gate.astype(BF16),
                   w_ple_proj.astype(BF16), ln2_g.reshape(1, d), ln2_b.reshape(1, d), yb)
    return out.reshape(b, s, d)


def kernel(x, p, w_in, w_conv, w_out, ln1_g, ln1_b, w_router_g, b_router_g, w_router_e,
           b_router_e, w_gate, w_up, w_down, w_ple_gate, w_ple_proj, ln2_g, ln2_b):
    for i in range(DEPTH):
        x = _layer(x, p[i], w_in[i], w_conv[i], w_out[i], ln1_g[i], ln1_b[i], w_router_g[i],
                   b_router_g[i], w_router_e[i], b_router_e[i], w_gate[i], w_up[i], w_down[i],
                   w_ple_gate[i], w_ple_proj[i], ln2_g[i], ln2_b[i])
    return x
```

```python
import functools

import jax
import jax.numpy as jnp
from jax import lax
from jax.experimental import pallas as pl
from jax.experimental.pallas import tpu as pltpu

D_MODEL = 1024
D_CONV = 512
N_HEADS = 8
HEAD_DIM = 64
D_ATTN = N_HEADS * HEAD_DIM
D_IN = 3 * D_CONV + 3 * D_ATTN
MOBA_BLOCK = 256
MOBA_TOPK = 3
N_GROUPS = 4
EXPERTS_PER_GROUP = 8
N_EXPERTS = N_GROUPS * EXPERTS_PER_GROUP
DISPATCH_BLOCK = 256
ROW_GROUP = 8
PLE_DIM = 256
LN_EPS = 1e-5
DEPTH = 1
DEEPNORM_ALPHA = (2 * DEPTH) ** 0.25

LANES = 128
PROJ_ROWS = 512
TOKEN_TILE = 256
SORTED_TILE_ROWS = 2 * TOKEN_TILE + ROW_GROUP * N_EXPERTS
UNSORT_CHUNK = 256
SORTED_ROW_WIDTH = D_MODEL + LANES
KV_CHUNK_SHIFT = 2
KV_CHUNK_BLOCKS = 1 << KV_CHUNK_SHIFT
MOBA_HEADS_PER_STEP = 4
BF16_SUBLANES = 16
ACC_ROWS = HEAD_DIM + BF16_SUBLANES
Q_SCALE = HEAD_DIM ** -0.5 * 1.4426950408889634
VMEM_LIMIT = 48 * 1024 * 1024

F32 = jnp.float32
BF16 = jnp.bfloat16
NEG_INF = float("-inf")


def _dot(a, b):
    return jnp.dot(a, b, preferred_element_type=F32)


def _layer_norm(h, g, b):
    mu = jnp.mean(h, axis=-1, keepdims=True)
    d = h - mu
    var = jnp.mean(d * d, axis=-1, keepdims=True)
    return d * lax.rsqrt(var + LN_EPS) * g + b


def _proj_conv_kernel(x_ref, w_ref, wc_ref, yc_ref, q_ref, k_ref, vt_ref, km_ref, ubuf):
    s = pl.program_id(1)
    rows = x_ref.shape[0]
    xb = x_ref[...].astype(BF16)

    def sect(i):
        return _dot(xb, w_ref[:, i * D_CONV:(i + 1) * D_CONV])

    @pl.when(s == 0)
    def _():
        ubuf[0:8, :] = jnp.zeros((8, D_CONV), F32)

    u = sect(1) * sect(2)
    ubuf[8:8 + rows, :] = u
    wc = wc_ref[...]
    conv = wc[0:1, :] * ubuf[6:6 + rows, :] + wc[1:2, :] * ubuf[7:7 + rows, :] + wc[2:3, :] * u
    yc_ref[...] = (sect(0) * conv).astype(BF16)
    ubuf[0:8, :] = ubuf[rows:rows + 8, :]

    q_ref[...] = (sect(3) * Q_SCALE).astype(BF16)
    k = sect(4)
    k_ref[...] = k.astype(BF16)
    for i in range(rows // MOBA_BLOCK):
        km_ref[i] = jnp.mean(k[i * MOBA_BLOCK:(i + 1) * MOBA_BLOCK, :], axis=0, keepdims=True)
    vt = sect(5).T
    ones = jnp.ones((ACC_ROWS - HEAD_DIM, rows), F32)
    vt = jnp.concatenate(
        [piece for h in range(N_HEADS) for piece in (vt[h * HEAD_DIM:(h + 1) * HEAD_DIM, :], ones)],
        axis=0).astype(BF16)
    for i in range(rows // MOBA_BLOCK):
        vt_ref[i] = vt[:, i * MOBA_BLOCK:(i + 1) * MOBA_BLOCK]


def _proj_conv(x, w_in_bf, w_conv):
    b, s, _ = x.shape
    nb = s // MOBA_BLOCK
    grid = (b, s // PROJ_ROWS)
    seq_spec = pl.BlockSpec((None, PROJ_ROWS, D_CONV), lambda i, j: (i, j, 0))
    return pl.pallas_call(
        _proj_conv_kernel,
        grid=grid,
        in_specs=[
            pl.BlockSpec((None, PROJ_ROWS, D_MODEL), lambda i, j: (i, j, 0)),
            pl.BlockSpec((D_MODEL, D_IN), lambda i, j: (0, 0)),
            pl.BlockSpec((3, D_CONV), lambda i, j: (0, 0)),
        ],
        out_specs=[
            seq_spec, seq_spec, seq_spec,
            pl.BlockSpec((None, PROJ_ROWS // MOBA_BLOCK, N_HEADS * ACC_ROWS, MOBA_BLOCK),
                         lambda i, j: (i, j, 0, 0)),
            pl.BlockSpec((None, PROJ_ROWS // MOBA_BLOCK, 1, D_ATTN), lambda i, j: (i, j, 0, 0)),
        ],
        out_shape=[
            jax.ShapeDtypeStruct((b, s, D_CONV), BF16),
            jax.ShapeDtypeStruct((b, s, D_ATTN), BF16),
            jax.ShapeDtypeStruct((b, s, D_ATTN), BF16),
            jax.ShapeDtypeStruct((b, nb, N_HEADS * ACC_ROWS, MOBA_BLOCK), BF16),
            jax.ShapeDtypeStruct((b, nb, 1, D_ATTN), F32),
        ],
        scratch_shapes=[pltpu.VMEM((PROJ_ROWS + 8, D_CONV), F32)],
        compiler_params=pltpu.CompilerParams(
            dimension_semantics=("arbitrary", "arbitrary"), vmem_limit_bytes=VMEM_LIMIT),
        name="proj_conv",
    )(x, w_in_bf, w_conv)


def _moba_kernel(q_ref, k_ref, vt_ref, km_ref, o_ref, bias_ref, sd_ref, s_ref, mt_ref,
                 m_ref, acc_ref, qt_ref):
    j = pl.program_id(2)
    blk = MOBA_BLOCK
    nb = km_ref.shape[0]
    n_heads = vt_ref.shape[1] // ACC_ROWS
    heads = range(n_heads)
    heads_per_block = n_heads // KV_CHUNK_BLOCKS
    lead = j & (KV_CHUNK_BLOCKS - 1)
    n_chunks = lax.shift_right_logical(j, KV_CHUNK_SHIFT)
    lead0 = n_chunks * KV_CHUNK_BLOCKS

    def pair_lanes(h):
        return slice((h // 2) * LANES, (h // 2 + 1) * LANES)

    def prepare():
        row = lax.broadcasted_iota(jnp.int32, (LANES, blk), 0)
        qt_pairs = [q_ref[:, pair_lanes(2 * pr)].astype(F32).T for pr in range(n_heads // 2)]
        bidx = lax.broadcasted_iota(jnp.int32, (nb, blk), 0)
        for h in heads:
            qt = jnp.where((row >= HEAD_DIM) == bool(h % 2), qt_pairs[h // 2], 0.0).astype(BF16)
            qt_ref[h] = qt
            km = km_ref[:, pair_lanes(h)]
            km_hi = km.astype(BF16)
            km_lo = (km - km_hi.astype(F32)).astype(BF16)
            g = _dot(km_hi, qt) + _dot(km_lo, qt)
            g = jnp.where(bidx < j, g, NEG_INF)
            sel = jnp.zeros((nb, blk), jnp.bool_)
            for _ in range(MOBA_TOPK):
                mx = jnp.max(g, axis=0, keepdims=True)
                cand = jnp.where((g == mx) & (mx > NEG_INF), bidx, nb)
                pick = bidx == jnp.min(cand, axis=0, keepdims=True)
                sel = sel | pick
                g = jnp.where(pick, NEG_INF, g)
            bias_ref[h] = jnp.where(sel, 0.0, NEG_INF)

    def k_block(n, h):
        return k_ref[pl.ds(pl.multiple_of(n * blk, blk), blk), pair_lanes(h)]

    def vt_block(n, h):
        return vt_ref[n, h * ACC_ROWS:(h + 1) * ACC_ROWS, :]

    def stage_block(c, u, slot):
        n = c * KV_CHUNK_BLOCKS + u
        mts = []
        for h in heads:
            sc = _dot(k_block(n, h), qt_ref[h]) + bias_ref[h, pl.ds(n, 1), :]
            s_ref[slot, h, u * blk:(u + 1) * blk, :] = sc
            mts.append(jnp.max(sc, axis=0, keepdims=True))
        return mts

    def store_staged_max(slot, mts):
        for h in heads:
            mt_ref[slot, h] = functools.reduce(jnp.maximum, [m[h] for m in mts])

    def pv(vt_h, sc, m_new):
        return _dot(vt_h, jnp.exp2(sc - m_new).astype(BF16))

    def rescale(h, mt):
        m_old = m_ref[h]
        m_new = jnp.maximum(m_old, mt)
        m_ref[h] = m_new
        return m_new, jnp.exp2(m_old - m_new)

    def step(consume, stage):
        if consume is not None:
            cc, cs = consume
            scale = [rescale(h, mt_ref[cs, h]) for h in heads]
        mts = []
        for u in range(KV_CHUNK_BLOCKS):
            if stage is not None:
                mts.append(stage_block(stage[0], u, stage[1]))
            if consume is not None:
                for h in heads[u * heads_per_block:(u + 1) * heads_per_block]:
                    parts = [pv(vt_block(cc * KV_CHUNK_BLOCKS + v, h),
                                s_ref[cs, h, v * blk:(v + 1) * blk, :], scale[h][0])
                             for v in range(KV_CHUNK_BLOCKS)]
                    acc_ref[h] = scale[h][1] * acc_ref[h] + functools.reduce(jnp.add, parts)
        if stage is not None:
            store_staged_max(stage[1], mts)

    kpos = lax.broadcasted_iota(jnp.int32, (blk, blk), 0)
    qpos = lax.broadcasted_iota(jnp.int32, (blk, blk), 1)
    causal = kpos <= qpos

    def first_update(n_lead):
        prepare()
        m_first = []
        for h in heads:
            sc = jnp.where(causal, _dot(k_block(j, h), qt_ref[h]), NEG_INF)
            sd_ref[h] = sc
            m_first.append(jnp.max(sc, axis=0, keepdims=True))
        for u in range(n_lead):
            for h in heads:
                sc = _dot(k_block(lead0 + u, h), qt_ref[h]) + bias_ref[h, pl.ds(lead0 + u, 1), :]
                s_ref[1, h, u * blk:(u + 1) * blk, :] = sc
                m_first[h] = jnp.maximum(m_first[h], jnp.max(sc, axis=0, keepdims=True))
        parts = [None] * n_heads
        mts = []
        for u in range(KV_CHUNK_BLOCKS):
            mts.append(stage_block(0, u, 0))
            for h in heads:
                if u == 0:
                    parts[h] = pv(vt_block(j, h), sd_ref[h], m_first[h])
                elif u <= n_lead:
                    parts[h] = parts[h] + pv(vt_block(lead0 + u - 1, h),
                                             s_ref[1, h, (u - 1) * blk:u * blk, :], m_first[h])
        store_staged_max(0, mts)
        for h in heads:
            m_ref[h] = m_first[h]
            acc_ref[h] = parts[h]

    for n_lead in range(KV_CHUNK_BLOCKS):
        pl.when(lead == n_lead)(functools.partial(first_update, n_lead))

    steps = jnp.maximum(n_chunks - 1, 0)

    def body(i, carry):
        c = 2 * i
        step((c, 0), (c + 1, 1))
        step((c + 1, 1), (c + 2, 0))
        return carry

    lax.fori_loop(0, lax.shift_right_logical(steps, 1), body, 0)

    @pl.when((n_chunks > 0) & (steps % 2 == 1))
    def _():
        step((n_chunks - 2, 0), (n_chunks - 1, 1))
        step((n_chunks - 1, 1), None)

    @pl.when((n_chunks > 0) & (steps % 2 == 0))
    def _():
        step((n_chunks - 1, 0), None)

    outs = [acc_ref[h, 0:HEAD_DIM, :] / acc_ref[h, HEAD_DIM:HEAD_DIM + 1, :] for h in heads]
    o_ref[...] = jnp.concatenate(outs, axis=0).T.astype(BF16)


def _moba(q, k, vt, kmean):
    b, s, _ = q.shape
    nb = s // MOBA_BLOCK
    w = MOBA_HEADS_PER_STEP * HEAD_DIM
    hs = MOBA_HEADS_PER_STEP
    return pl.pallas_call(
        _moba_kernel,
        grid=(b, D_ATTN // w, nb),
        in_specs=[
            pl.BlockSpec((None, MOBA_BLOCK, w), lambda i, p, j: (i, j, p)),
            pl.BlockSpec((None, s, w), lambda i, p, j: (i, 0, p)),
            pl.BlockSpec((None, nb, hs * ACC_ROWS, MOBA_BLOCK), lambda i, p, j: (i, 0, p, 0)),
            pl.BlockSpec((None, nb, w), lambda i, p, j: (i, 0, p)),
        ],
        out_specs=pl.BlockSpec((None, MOBA_BLOCK, w), lambda i, p, j: (i, j, p)),
        out_shape=jax.ShapeDtypeStruct((b, s, D_ATTN), BF16),
        scratch_shapes=[
            pltpu.VMEM((hs, nb, MOBA_BLOCK), F32),
            pltpu.VMEM((hs, MOBA_BLOCK, MOBA_BLOCK), F32),
            pltpu.VMEM((2, hs, KV_CHUNK_BLOCKS * MOBA_BLOCK, MOBA_BLOCK), F32),
            pltpu.VMEM((3, hs, 1, MOBA_BLOCK), F32),
            pltpu.VMEM((hs, 1, MOBA_BLOCK), F32),
            pltpu.VMEM((hs, ACC_ROWS, MOBA_BLOCK), F32),
            pltpu.VMEM((hs, LANES, MOBA_BLOCK), BF16),
        ],
        compiler_params=pltpu.CompilerParams(
            dimension_semantics=("arbitrary", "arbitrary", "arbitrary"),
            vmem_limit_bytes=VMEM_LIMIT),
        name="moba",
    )(q, k, vt, kmean)


GROUP_LANE0 = N_EXPERTS


def _split3(col, lane):
    hi = col.astype(BF16).astype(F32)
    mid = (col - hi).astype(BF16).astype(F32)
    lo = col - hi - mid
    return jnp.where(lane == 0, hi, jnp.where(lane == 1, mid, jnp.where(lane == 2, lo, 0.0))).astype(BF16)


def _mix_route_kernel(yc_ref, ya_ref, x_ref, wo_ref, g_ref, b_ref, wr_ref, br_ref,
                      x1_ref, xs_ref, pos_ref, cnt_ref):
    t = x_ref.shape[0]

    mix = _dot(yc_ref[...], wo_ref[0:D_CONV, :]) + _dot(ya_ref[...], wo_ref[D_CONV:, :])
    x1 = _layer_norm(DEEPNORM_ALPHA * x_ref[...] + mix, g_ref[...], b_ref[...])
    x1_ref[...] = x1

    x_hi = x1.astype(BF16)
    x_lo = (x1 - x_hi.astype(F32)).astype(BF16)
    wr = wr_ref[...]
    w_hi = wr.astype(BF16)
    w_lo = (wr - w_hi.astype(F32)).astype(BF16)
    logits = _dot(x_hi, w_hi) + _dot(x_lo, w_hi) + _dot(x_hi, w_lo) + br_ref[...]

    lane = lax.broadcasted_iota(jnp.int32, (t, LANES), 1).astype(F32)
    big = float(LANES)

    lg = jnp.where((lane >= GROUP_LANE0) & (lane < GROUP_LANE0 + N_GROUPS), logits, NEG_INF)
    gmax = jnp.max(lg, axis=-1, keepdims=True)
    g_p = 1.0 / jnp.sum(jnp.exp(lg - gmax), axis=-1, keepdims=True)
    g_idx = jnp.min(jnp.where(lg == gmax, lane, big), axis=-1, keepdims=True) - GROUP_LANE0

    lo = g_idx * EXPERTS_PER_GROUP
    le = jnp.where((lane >= lo) & (lane < lo + EXPERTS_PER_GROUP), logits, NEG_INF)
    m1 = jnp.max(le, axis=-1, keepdims=True)
    i1 = jnp.min(jnp.where(le == m1, lane, big), axis=-1, keepdims=True)
    le2 = jnp.where(lane == i1, NEG_INF, le)
    m2 = jnp.max(le2, axis=-1, keepdims=True)
    i2 = jnp.min(jnp.where(le2 == m2, lane, big), axis=-1, keepdims=True)
    p2 = jnp.exp(m2 - m1)
    den = 1.0 + p2
    w1 = g_p * (1.0 / den)
    w2 = g_p * (p2 / den)

    oh = (lane == i1) | (lane == i2)
    ohf = oh.astype(F32)
    r_i = lax.broadcasted_iota(jnp.int32, (t, t), 0)
    c_i = lax.broadcasted_iota(jnp.int32, (t, t), 1)
    before = _dot((r_i > c_i).astype(BF16), ohf.astype(BF16))
    r1 = jnp.sum(jnp.where(lane == i1, before, 0.0), axis=-1, keepdims=True)
    r2 = jnp.sum(jnp.where(lane == i2, before, 0.0), axis=-1, keepdims=True)

    cnt = jnp.sum(ohf, axis=0, keepdims=True)
    seg = jnp.floor((cnt + (ROW_GROUP - 1)) * (1.0 / ROW_GROUP)) * ROW_GROUP
    e_a = lax.broadcasted_iota(jnp.int32, (LANES, LANES), 0)
    e_b = lax.broadcasted_iota(jnp.int32, (LANES, LANES), 1)
    seg_start = _dot(jnp.broadcast_to(seg, (8, LANES)).astype(BF16), (e_a < e_b).astype(BF16))[0:1, :]
    pos1 = jnp.sum(jnp.where(lane == i1, seg_start, 0.0), axis=-1, keepdims=True) + r1
    pos2 = jnp.sum(jnp.where(lane == i2, seg_start, 0.0), axis=-1, keepdims=True) + r2
    cnt_ref[...] = jnp.broadcast_to(cnt, cnt_ref.shape)

    pos = jnp.where(lane == 0, pos1, pos2)
    pos_ref[...] = pos
    pos_rows = pos.T[0:8, :]

    srow = lax.broadcasted_iota(jnp.int32, (xs_ref.shape[0], t), 0).astype(F32)
    perm1 = (srow == pos_rows[0:1, :]).astype(BF16)
    perm2 = (srow == pos_rows[1:2, :]).astype(BF16)
    xs_ref[:, 0:D_MODEL] = _dot(perm1 + perm2, x_hi)
    xs_ref[:, D_MODEL:] = (_dot(perm1, _split3(w1, lane)) + _dot(perm2, _split3(w2, lane)))


def _mix_route(yc, ya, x2d, w_out_bf, g, b, w_r, b_r):
    n = x2d.shape[0]
    t = TOKEN_TILE
    nt = n // t
    row = lambda w: pl.BlockSpec((t, w), lambda i: (i, 0))
    full = lambda a: pl.BlockSpec(a.shape, lambda i: (0, 0))
    return pl.pallas_call(
        _mix_route_kernel,
        grid=(nt,),
        in_specs=[row(D_CONV), row(D_ATTN), row(D_MODEL), full(w_out_bf), full(g), full(b),
                  full(w_r), full(b_r)],
        out_specs=[row(D_MODEL), pl.BlockSpec((SORTED_TILE_ROWS, SORTED_ROW_WIDTH), lambda i: (i, 0)),
                   row(LANES), pl.BlockSpec((8, LANES), lambda i: (i, 0))],
        out_shape=[
            jax.ShapeDtypeStruct((n, D_MODEL), F32),
            jax.ShapeDtypeStruct((nt * SORTED_TILE_ROWS, SORTED_ROW_WIDTH), F32),
            jax.ShapeDtypeStruct((n, LANES), F32),
            jax.ShapeDtypeStruct((nt * 8, LANES), F32),
        ],
        compiler_params=pltpu.CompilerParams(
            dimension_semantics=("arbitrary",), vmem_limit_bytes=VMEM_LIMIT),
        name="mix_route",
    )(yc, ya, x2d, w_out_bf, g, b, w_r, b_r)


def _row_copies_start(rows, src_of, dst_of, sem):
    for r in rows:
        pltpu.make_async_copy(src_of(r), dst_of(r), sem).start()


def _row_copies_wait(src_rows, dst_rows, sem):
    pltpu.make_async_copy(src_rows, dst_rows, sem).wait()


def _experts_kernel(be_ref, first_ref, nxt_ref, par_ref, nu_ref, cur_ref, nxg_ref, xs_hbm,
                    wg_hbm, wu_hbm, wd_hbm, yb_ref, xbuf, wg_buf, wu_buf, wd_buf, wg_bf, wu_bf, wd_bf,
                    xsem, wsem):
    i = pl.program_id(0)
    used = i < nu_ref[0]
    slot = par_ref[i]
    xslot = i % 2
    g = ROW_GROUP
    weights = ((wg_hbm, wg_buf), (wu_hbm, wu_buf), (wd_hbm, wd_buf))

    def fetch(e, s):
        for k, (w_hbm, w_buf) in enumerate(weights):
            pltpu.make_async_copy(w_hbm.at[e], w_buf.at[s], wsem.at[s, k]).start()

    def fetch_wait(s):
        for k, (w_hbm, w_buf) in enumerate(weights):
            pltpu.make_async_copy(w_hbm.at[0], w_buf.at[s], wsem.at[s, k]).wait()

    def gather_rows(grp_ref, s):
        _row_copies_start(
            range(DISPATCH_BLOCK // g),
            lambda q: xs_hbm.at[pl.ds(pl.multiple_of(grp_ref[0, q], g), g)],
            lambda q: xbuf.at[s, pl.ds(q * g, g)], xsem.at[s])

    @pl.when(i == 0)
    def _():
        fetch(be_ref[0], slot)
        gather_rows(cur_ref, 0)

    @pl.when(i + 1 < nu_ref[0])
    def _():
        gather_rows(nxg_ref, 1 - xslot)

    @pl.when(used & (first_ref[i] == 1))
    def _():
        fetch_wait(slot)

        @pl.when(nxt_ref[i] >= 0)
        def _():
            fetch(nxt_ref[i], 1 - slot)

        wg_bf[...] = wg_buf[slot].astype(BF16)
        wu_bf[...] = wu_buf[slot].astype(BF16)
        wd_bf[...] = wd_buf[slot].astype(BF16)

    @pl.when(used)
    def _():
        _row_copies_wait(xs_hbm.at[pl.ds(0, DISPATCH_BLOCK)], xbuf.at[xslot], xsem.at[xslot])
        xb = xbuf[xslot, :, 0:D_MODEL].astype(BF16)
        wrow = xbuf[xslot, :, D_MODEL:]
        weight = wrow[:, 0:1] + wrow[:, 1:2] + wrow[:, 2:3]
        gate = _dot(xb, wg_bf[...])
        up = _dot(xb, wu_bf[...])
        hid = (jax.nn.silu(gate) * up).astype(BF16)
        yb_ref[...] = _dot(hid, wd_bf[...]) * weight

    @pl.when(jnp.logical_not(used))
    def _():
        yb_ref[...] = jnp.zeros_like(yb_ref)


def _experts(blk_expert, blk_first, blk_next, blk_slot, n_used, grp_row, xs, w_gate, w_up, w_down):
    n_blk = blk_expert.shape[0]
    r = DISPATCH_BLOCK
    gpb = r // ROW_GROUP
    grp3 = grp_row.reshape(n_blk, 1, gpb)
    smem_blk = lambda f: pl.BlockSpec((None, 1, gpb), f, memory_space=pltpu.SMEM)
    grid_spec = pltpu.PrefetchScalarGridSpec(
        num_scalar_prefetch=5,
        grid=(n_blk,),
        in_specs=[
            smem_blk(lambda i, *_: (i, 0, 0)),
            smem_blk(lambda i, *_: (jnp.minimum(i + 1, n_blk - 1), 0, 0)),
            pl.BlockSpec(memory_space=pl.ANY),
            pl.BlockSpec(memory_space=pl.ANY),
            pl.BlockSpec(memory_space=pl.ANY),
            pl.BlockSpec(memory_space=pl.ANY),
        ],
        out_specs=pl.BlockSpec((r, D_MODEL), lambda i, *_: (i, 0)),
        scratch_shapes=[
            pltpu.VMEM((2, r, xs.shape[1]), F32),
            pltpu.VMEM((2,) + w_gate.shape[1:], w_gate.dtype),
            pltpu.VMEM((2,) + w_up.shape[1:], w_up.dtype),
            pltpu.VMEM((2,) + w_down.shape[1:], w_down.dtype),
            pltpu.VMEM(w_gate.shape[1:], BF16),
            pltpu.VMEM(w_up.shape[1:], BF16),
            pltpu.VMEM(w_down.shape[1:], BF16),
            pltpu.SemaphoreType.DMA((2,)),
            pltpu.SemaphoreType.DMA((2, 3)),
        ],
    )
    return pl.pallas_call(
        _experts_kernel,
        grid_spec=grid_spec,
        out_shape=jax.ShapeDtypeStruct((n_blk * r, D_MODEL), F32),
        compiler_params=pltpu.CompilerParams(
            dimension_semantics=("arbitrary",), vmem_limit_bytes=VMEM_LIMIT),
        name="experts",
    )(blk_expert, blk_first, blk_next, blk_slot, n_used, grp3, grp3, xs, w_gate, w_up, w_down)


def _combine_kernel(cur_ref, nxt_ref, x1_ref, p_ref, pos_ref, wpg_ref, wpp_ref, g_ref, b_ref,
                    yb_hbm, o_ref, ybuf, sem):
    i = pl.program_id(0)
    n_steps = pl.num_programs(0)
    t = x1_ref.shape[0]
    slot = i % 2
    g = ROW_GROUP
    n_groups = ybuf.shape[1] // g

    def gather(grp_ref, s):
        _row_copies_start(
            range(n_groups),
            lambda q: yb_hbm.at[pl.ds(pl.multiple_of(grp_ref[0, q], g), g)],
            lambda q: ybuf.at[s, pl.ds(q * g, g)], sem.at[s])

    @pl.when(i == 0)
    def _():
        gather(cur_ref, 0)

    @pl.when(i + 1 < n_steps)
    def _():
        gather(nxt_ref, 1 - slot)

    x1 = x1_ref[...]
    ple = jax.nn.sigmoid(_dot(x1.astype(BF16), wpg_ref[...])) * _dot(p_ref[...].astype(BF16), wpp_ref[...])

    _row_copies_wait(yb_hbm.at[pl.ds(0, ybuf.shape[1])], ybuf.at[slot], sem.at[slot])
    pos = pos_ref[...]
    scol = lax.broadcasted_iota(jnp.int32, (t, ybuf.shape[1]), 1).astype(F32)
    pick = ((scol == pos[:, 0:1]) | (scol == pos[:, 1:2])).astype(BF16)
    ffn = None
    for c in range(0, ybuf.shape[1], UNSORT_CHUNK):
        ys = ybuf[slot, c:c + UNSORT_CHUNK, :]
        hi = ys.astype(BF16)
        lo = (ys - hi.astype(F32)).astype(BF16)
        pc = pick[:, c:c + UNSORT_CHUNK]
        part = _dot(pc, hi) + _dot(pc, lo)
        ffn = part if ffn is None else ffn + part
    o_ref[...] = _layer_norm(DEEPNORM_ALPHA * x1 + ffn + ple, g_ref[...], b_ref[...])


def _combine(tile_grp, x1, p2d, pos, w_pg_bf, w_pp_bf, g, b, yb):
    n = x1.shape[0]
    t = TOKEN_TILE
    n_steps = n // t
    gpt = SORTED_TILE_ROWS // ROW_GROUP
    grp3 = tile_grp.reshape(n_steps, 1, gpt)
    smem_blk = lambda f: pl.BlockSpec((None, 1, gpt), f, memory_space=pltpu.SMEM)
    row = lambda w: pl.BlockSpec((t, w), lambda i: (i, 0))
    full = lambda a: pl.BlockSpec(a.shape, lambda i: (0, 0))
    return pl.pallas_call(
        _combine_kernel,
        grid=(n_steps,),
        in_specs=[
            smem_blk(lambda i: (i, 0, 0)),
            smem_blk(lambda i: (jnp.minimum(i + 1, n_steps - 1), 0, 0)),
            row(D_MODEL), row(PLE_DIM), row(LANES), full(w_pg_bf), full(w_pp_bf), full(g), full(b),
            pl.BlockSpec(memory_space=pl.ANY),
        ],
        out_specs=row(D_MODEL),
        out_shape=jax.ShapeDtypeStruct((n, D_MODEL), F32),
        scratch_shapes=[pltpu.VMEM((2, SORTED_TILE_ROWS, D_MODEL), F32), pltpu.SemaphoreType.DMA((2,))],
        compiler_params=pltpu.CompilerParams(
            dimension_semantics=("arbitrary",), vmem_limit_bytes=VMEM_LIMIT),
        name="combine",
    )(grp3, grp3, x1, p2d, pos, w_pg_bf, w_pp_bf, g, b, yb)


def _layer(x, p, w_in, w_conv, w_out, ln1_g, ln1_b, w_router_g, b_router_g, w_router_e,
           b_router_e, w_gate, w_up, w_down, w_ple_gate, w_ple_proj, ln2_g, ln2_b):
    b, s, d = x.shape
    n = b * s
    r = DISPATCH_BLOCK

    yc, q, k, vt, kmean = _proj_conv(x, w_in.astype(BF16), w_conv)
    ya = _moba(q, k, vt, kmean.reshape(b, s // MOBA_BLOCK, D_ATTN))

    w_r = jnp.zeros((d, LANES), F32).at[:, :N_EXPERTS].set(w_router_e)
    w_r = w_r.at[:, GROUP_LANE0:GROUP_LANE0 + N_GROUPS].set(w_router_g)
    b_r = jnp.zeros((1, LANES), F32).at[0, :N_EXPERTS].set(b_router_e)
    b_r = b_r.at[0, GROUP_LANE0:GROUP_LANE0 + N_GROUPS].set(b_router_g)
    x1, xs, pos, cnt = _mix_route(
        yc.reshape(n, D_CONV), ya.reshape(n, D_ATTN), x.reshape(n, d), w_out.astype(BF16),
        ln1_g.reshape(1, d), ln1_b.reshape(1, d), w_r, b_r)

    nt = n // TOKEN_TILE
    g = ROW_GROUP
    seg = (cnt.reshape(nt, 8, LANES)[:, 0, :N_EXPERTS].astype(jnp.int32) + g - 1) // g * g
    rows_e = jnp.sum(seg, axis=0)
    padded = (rows_e + r - 1) // r * r
    row_end = jnp.cumsum(padded)
    row_start = row_end - padded
    seg_dst = row_start[None, :] + jnp.cumsum(seg, axis=0) - seg
    seg_off = jnp.cumsum(seg, axis=1) - seg
    seg_src = (jnp.arange(nt, dtype=jnp.int32) * SORTED_TILE_ROWS)[:, None] + seg_off
    n_blk = -(-(2 * n + (g - 1) * N_EXPERTS * nt) // r) + N_EXPERTS

    blk_row0 = jnp.arange(n_blk, dtype=jnp.int32) * r
    blk_expert = jnp.minimum(
        jnp.sum((row_end[None, :] <= blk_row0[:, None]).astype(jnp.int32), axis=1), N_EXPERTS - 1)
    n_used = row_end[-1:] // r
    eid = jnp.arange(N_EXPERTS, dtype=jnp.int32)
    blk_onehot = (blk_expert[:, None] == eid[None, :]).astype(jnp.int32)

    def of_block(tab):
        return jnp.sum(blk_onehot[:, :, None] * tab.T[None, :, :], axis=1)
    dst_b, src_b, len_b = of_block(seg_dst)[:, None, :], of_block(seg_src)[:, None, :], of_block(seg)[:, None, :]
    grp_first = (blk_row0[:, None] + jnp.arange(r // g, dtype=jnp.int32)[None, :] * g)[:, :, None]
    inside = (grp_first >= dst_b) & (grp_first < dst_b + len_b)
    grp_row = jnp.where(jnp.any(inside, axis=2),
                        jnp.sum(jnp.where(inside, src_b + grp_first - dst_b, 0), axis=2),
                        SORTED_TILE_ROWS - g)

    tile_first = (jnp.arange(SORTED_TILE_ROWS // g, dtype=jnp.int32) * g)[None, :, None]
    t_off, t_len, t_dst = seg_off[:, None, :], seg[:, None, :], seg_dst[:, None, :]
    within = (tile_first >= t_off) & (tile_first < t_off + t_len)
    tile_grp = jnp.sum(jnp.where(within, t_dst + tile_first - t_off, 0), axis=2)

    nonempty = rows_e > 0
    later = jnp.where((eid[None, :] > eid[:, None]) & nonempty[None, :], eid[None, :], N_EXPERTS)
    next_expert = jnp.min(later, axis=1)
    next_expert = jnp.where(next_expert == N_EXPERTS, -1, next_expert)
    ordinal = jnp.cumsum(nonempty.astype(jnp.int32)) - 1
    blk_first = jnp.concatenate(
        [jnp.ones((1,), jnp.int32), (blk_expert[1:] != blk_expert[:-1]).astype(jnp.int32)])
    blk_next = jnp.sum(blk_onehot * next_expert[None, :], axis=1)
    blk_slot = jnp.sum(blk_onehot * ordinal[None, :], axis=1) % 2

    yb = _experts(blk_expert, blk_first, blk_next, blk_slot, n_used, grp_row, xs, w_gate, w_up, w_down)
    out = _combine(tile_grp, x1, p.reshape(n, PLE_DIM), pos, w_ple_gate.astype(BF16),
                   w_ple_proj.astype(BF16), ln2_g.reshape(1, d), ln2_b.reshape(1, d), yb)
    return out.reshape(b, s, d)


def kernel(x, p, w_in, w_conv, w_out, ln1_g, ln1_b, w_router_g, b_router_g, w_router_e,
           b_router_e, w_gate, w_up, w_down, w_ple_gate, w_ple_proj, ln2_g, ln2_b):
    for i in range(DEPTH):
        x = _layer(x, p[i], w_in[i], w_conv[i], w_out[i], ln1_g[i], ln1_b[i], w_router_g[i],
                   b_router_g[i], w_router_e[i], b_router_e[i], w_gate[i], w_up[i], w_down[i],
                   w_ple_gate[i], w_ple_proj[i], ln2_g[i], ln2_b[i])
    return x
```

```python
import functools

import jax
import jax.numpy as jnp
from jax import lax
from jax.experimental import pallas as pl
from jax.experimental.pallas import tpu as pltpu

D_MODEL = 1024
D_CONV = 512
N_HEADS = 8
HEAD_DIM = 64
D_ATTN = N_HEADS * HEAD_DIM
D_IN = 3 * D_CONV + 3 * D_ATTN
MOBA_BLOCK = 256
MOBA_TOPK = 3
N_GROUPS = 4
EXPERTS_PER_GROUP = 8
N_EXPERTS = N_GROUPS * EXPERTS_PER_GROUP
DISPATCH_BLOCK = 256
ROW_GROUP = 8
PLE_DIM = 256
LN_EPS = 1e-5
DEPTH = 1
DEEPNORM_ALPHA = (2 * DEPTH) ** 0.25

LANES = 128
PROJ_ROWS = 512
TOKEN_TILE = 256
SORTED_TILE_ROWS = 2 * TOKEN_TILE + ROW_GROUP * N_EXPERTS
UNSORT_CHUNK = 256
SORTED_ROW_WIDTH = D_MODEL + LANES
KV_CHUNK_SHIFT = 2
KV_CHUNK_BLOCKS = 1 << KV_CHUNK_SHIFT
MOBA_HEADS_PER_STEP = 4
assert MOBA_HEADS_PER_STEP % KV_CHUNK_BLOCKS == 0
BF16_SUBLANES = 16
ACC_ROWS = HEAD_DIM + BF16_SUBLANES
Q_SCALE = HEAD_DIM ** -0.5 * 1.4426950408889634
VMEM_LIMIT = 48 * 1024 * 1024

F32 = jnp.float32
BF16 = jnp.bfloat16
NEG_INF = float("-inf")


def _dot(a, b):
    return jnp.dot(a, b, preferred_element_type=F32)


def _layer_norm(h, g, b):
    mu = jnp.mean(h, axis=-1, keepdims=True)
    d = h - mu
    var = jnp.mean(d * d, axis=-1, keepdims=True)
    return d * lax.rsqrt(var + LN_EPS) * g + b


def _proj_conv_kernel(x_ref, w_ref, wc_ref, yc_ref, q_ref, k_ref, vt_ref, km_ref, ubuf):
    s = pl.program_id(1)
    rows = x_ref.shape[0]
    xb = x_ref[...].astype(BF16)

    def sect(i):
        return _dot(xb, w_ref[:, i * D_CONV:(i + 1) * D_CONV])

    @pl.when(s == 0)
    def _():
        ubuf[0:8, :] = jnp.zeros((8, D_CONV), F32)

    u = sect(1) * sect(2)
    ubuf[8:8 + rows, :] = u
    wc = wc_ref[...]
    conv = wc[0:1, :] * ubuf[6:6 + rows, :] + wc[1:2, :] * ubuf[7:7 + rows, :] + wc[2:3, :] * u
    yc_ref[...] = (sect(0) * conv).astype(BF16)
    ubuf[0:8, :] = ubuf[rows:rows + 8, :]

    q_ref[...] = (sect(3) * Q_SCALE).astype(BF16)
    k = sect(4)
    k_ref[...] = k.astype(BF16)
    for i in range(rows // MOBA_BLOCK):
        km_ref[i] = jnp.mean(k[i * MOBA_BLOCK:(i + 1) * MOBA_BLOCK, :], axis=0, keepdims=True)
    vt = sect(5).T
    ones = jnp.ones((ACC_ROWS - HEAD_DIM, rows), F32)
    vt = jnp.concatenate(
        [piece for h in range(N_HEADS) for piece in (vt[h * HEAD_DIM:(h + 1) * HEAD_DIM, :], ones)],
        axis=0).astype(BF16)
    for i in range(rows // MOBA_BLOCK):
        vt_ref[i] = vt[:, i * MOBA_BLOCK:(i + 1) * MOBA_BLOCK]


def _proj_conv(x, w_in_bf, w_conv):
    b, s, _ = x.shape
    nb = s // MOBA_BLOCK
    grid = (b, s // PROJ_ROWS)
    seq_spec = pl.BlockSpec((None, PROJ_ROWS, D_CONV), lambda i, j: (i, j, 0))
    return pl.pallas_call(
        _proj_conv_kernel,
        grid=grid,
        in_specs=[
            pl.BlockSpec((None, PROJ_ROWS, D_MODEL), lambda i, j: (i, j, 0)),
            pl.BlockSpec((D_MODEL, D_IN), lambda i, j: (0, 0)),
            pl.BlockSpec((3, D_CONV), lambda i, j: (0, 0)),
        ],
        out_specs=[
            seq_spec, seq_spec, seq_spec,
            pl.BlockSpec((None, PROJ_ROWS // MOBA_BLOCK, N_HEADS * ACC_ROWS, MOBA_BLOCK),
                         lambda i, j: (i, j, 0, 0)),
            pl.BlockSpec((None, PROJ_ROWS // MOBA_BLOCK, 1, D_ATTN), lambda i, j: (i, j, 0, 0)),
        ],
        out_shape=[
            jax.ShapeDtypeStruct((b, s, D_CONV), BF16),
            jax.ShapeDtypeStruct((b, s, D_ATTN), BF16),
            jax.ShapeDtypeStruct((b, s, D_ATTN), BF16),
            jax.ShapeDtypeStruct((b, nb, N_HEADS * ACC_ROWS, MOBA_BLOCK), BF16),
            jax.ShapeDtypeStruct((b, nb, 1, D_ATTN), F32),
        ],
        scratch_shapes=[pltpu.VMEM((PROJ_ROWS + 8, D_CONV), F32)],
        compiler_params=pltpu.CompilerParams(
            dimension_semantics=("arbitrary", "arbitrary"), vmem_limit_bytes=VMEM_LIMIT),
        name="proj_conv",
    )(x, w_in_bf, w_conv)


def _moba_kernel(q_ref, k_ref, vt_ref, km_ref, o_ref, bias_ref, sd_ref, s_ref, mt_ref,
                 m_ref, acc_ref, qt_ref):
    j = pl.program_id(2)
    blk = MOBA_BLOCK
    nb = km_ref.shape[0]
    n_heads = vt_ref.shape[1] // ACC_ROWS
    heads = range(n_heads)
    heads_per_block = n_heads // KV_CHUNK_BLOCKS
    lead = j & (KV_CHUNK_BLOCKS - 1)
    n_chunks = lax.shift_right_logical(j, KV_CHUNK_SHIFT)
    lead0 = n_chunks * KV_CHUNK_BLOCKS

    def pair_lanes(h):
        return slice((h // 2) * LANES, (h // 2 + 1) * LANES)

    def prepare():
        row = lax.broadcasted_iota(jnp.int32, (LANES, blk), 0)
        qt_pairs = [q_ref[:, pair_lanes(2 * pr)].astype(F32).T for pr in range(n_heads // 2)]
        bidx = lax.broadcasted_iota(jnp.int32, (nb, blk), 0)
        for h in heads:
            qt = jnp.where((row >= HEAD_DIM) == bool(h % 2), qt_pairs[h // 2], 0.0).astype(BF16)
            qt_ref[h] = qt
            km = km_ref[:, pair_lanes(h)]
            km_hi = km.astype(BF16)
            km_lo = (km - km_hi.astype(F32)).astype(BF16)
            g = _dot(km_hi, qt) + _dot(km_lo, qt)
            g = jnp.where(bidx < j, g, NEG_INF)
            sel = jnp.zeros((nb, blk), jnp.bool_)
            for _ in range(MOBA_TOPK):
                mx = jnp.max(g, axis=0, keepdims=True)
                cand = jnp.where((g == mx) & (mx > NEG_INF), bidx, nb)
                pick = bidx == jnp.min(cand, axis=0, keepdims=True)
                sel = sel | pick
                g = jnp.where(pick, NEG_INF, g)
            bias_ref[h] = jnp.where(sel, 0.0, NEG_INF)

    def k_block(n, h):
        return k_ref[pl.ds(pl.multiple_of(n * blk, blk), blk), pair_lanes(h)]

    def vt_block(n, h):
        return vt_ref[n, h * ACC_ROWS:(h + 1) * ACC_ROWS, :]

    def stage_head(c, u, slot, h):
        n = c * KV_CHUNK_BLOCKS + u
        sc = _dot(k_block(n, h), qt_ref[h]) + bias_ref[h, pl.ds(n, 1), :]
        s_ref[slot, h, u * blk:(u + 1) * blk, :] = sc
        return jnp.max(sc, axis=0, keepdims=True)

    def stage_block(c, u, slot):
        return [stage_head(c, u, slot, h) for h in heads]

    def store_staged_max(slot, mts):
        for h in heads:
            mt_ref[slot, h] = functools.reduce(jnp.maximum, [m[h] for m in mts])

    def pv(vt_h, sc, m_new):
        return _dot(vt_h, jnp.exp2(sc - m_new).astype(BF16))

    def rescale(h, mt):
        m_old = m_ref[h]
        m_new = jnp.maximum(m_old, mt)
        m_ref[h] = m_new
        return m_new, jnp.exp2(m_old - m_new)

    def step(consume, stage):
        if consume is not None:
            cc, cs = consume
            scale = [rescale(h, mt_ref[cs, h]) for h in heads]
        mts = []
        for u in range(KV_CHUNK_BLOCKS):
            staged = heads if stage is not None else []
            updated = heads[u * heads_per_block:(u + 1) * heads_per_block] if consume is not None else []
            products = [(h, v) for h in updated for v in range(KV_CHUNK_BLOCKS)]
            mt_u = []
            parts = {h: [] for h in updated}
            for i in range(max(len(staged), len(products))):
                if i < len(staged):
                    mt_u.append(stage_head(stage[0], u, stage[1], staged[i]))
                if i < len(products):
                    h, v = products[i]
                    parts[h].append(pv(vt_block(cc * KV_CHUNK_BLOCKS + v, h),
                                       s_ref[cs, h, v * blk:(v + 1) * blk, :], scale[h][0]))
            mts.append(mt_u)
            for h in updated:
                acc_ref[h] = scale[h][1] * acc_ref[h] + functools.reduce(jnp.add, parts[h])
        if stage is not None:
            store_staged_max(stage[1], mts)

    kpos = lax.broadcasted_iota(jnp.int32, (blk, blk), 0)
    qpos = lax.broadcasted_iota(jnp.int32, (blk, blk), 1)
    causal = kpos <= qpos

    def first_update(n_lead):
        prepare()
        m_first = []
        for h in heads:
            sc = jnp.where(causal, _dot(k_block(j, h), qt_ref[h]), NEG_INF)
            sd_ref[h] = sc
            m_first.append(jnp.max(sc, axis=0, keepdims=True))
        for u in range(n_lead):
            for h in heads:
                sc = _dot(k_block(lead0 + u, h), qt_ref[h]) + bias_ref[h, pl.ds(lead0 + u, 1), :]
                s_ref[1, h, u * blk:(u + 1) * blk, :] = sc
                m_first[h] = jnp.maximum(m_first[h], jnp.max(sc, axis=0, keepdims=True))
        mts = []
        for u in range(KV_CHUNK_BLOCKS):
            mts.append(stage_block(0, u, 0))
            for h in heads[u * heads_per_block:(u + 1) * heads_per_block]:
                parts = [pv(vt_block(j, h), sd_ref[h], m_first[h])]
                parts += [pv(vt_block(lead0 + v, h), s_ref[1, h, v * blk:(v + 1) * blk, :], m_first[h])
                          for v in range(n_lead)]
                m_ref[h] = m_first[h]
                acc_ref[h] = functools.reduce(jnp.add, parts)
        store_staged_max(0, mts)

    for n_lead in range(KV_CHUNK_BLOCKS):
        pl.when(lead == n_lead)(functools.partial(first_update, n_lead))

    steps = jnp.maximum(n_chunks - 1, 0)

    def body(i, carry):
        c = 2 * i
        step((c, 0), (c + 1, 1))
        step((c + 1, 1), (c + 2, 0))
        return carry

    lax.fori_loop(0, lax.shift_right_logical(steps, 1), body, 0)

    @pl.when((n_chunks > 0) & (steps % 2 == 1))
    def _():
        step((n_chunks - 2, 0), (n_chunks - 1, 1))
        step((n_chunks - 1, 1), None)

    @pl.when((n_chunks > 0) & (steps % 2 == 0))
    def _():
        step((n_chunks - 1, 0), None)

    outs = [acc_ref[h, 0:HEAD_DIM, :] / acc_ref[h, HEAD_DIM:HEAD_DIM + 1, :] for h in heads]
    o_ref[...] = jnp.concatenate(outs, axis=0).T.astype(BF16)


def _moba(q, k, vt, kmean):
    b, s, _ = q.shape
    nb = s // MOBA_BLOCK
    w = MOBA_HEADS_PER_STEP * HEAD_DIM
    hs = MOBA_HEADS_PER_STEP
    return pl.pallas_call(
        _moba_kernel,
        grid=(b, D_ATTN // w, nb),
        in_specs=[
            pl.BlockSpec((None, MOBA_BLOCK, w), lambda i, p, j: (i, j, p)),
            pl.BlockSpec((None, s, w), lambda i, p, j: (i, 0, p)),
            pl.BlockSpec((None, nb, hs * ACC_ROWS, MOBA_BLOCK), lambda i, p, j: (i, 0, p, 0)),
            pl.BlockSpec((None, nb, w), lambda i, p, j: (i, 0, p)),
        ],
        out_specs=pl.BlockSpec((None, MOBA_BLOCK, w), lambda i, p, j: (i, j, p)),
        out_shape=jax.ShapeDtypeStruct((b, s, D_ATTN), BF16),
        scratch_shapes=[
            pltpu.VMEM((hs, nb, MOBA_BLOCK), F32),
            pltpu.VMEM((hs, MOBA_BLOCK, MOBA_BLOCK), F32),
            pltpu.VMEM((2, hs, KV_CHUNK_BLOCKS * MOBA_BLOCK, MOBA_BLOCK), F32),
            pltpu.VMEM((3, hs, 1, MOBA_BLOCK), F32),
            pltpu.VMEM((hs, 1, MOBA_BLOCK), F32),
            pltpu.VMEM((hs, ACC_ROWS, MOBA_BLOCK), F32),
            pltpu.VMEM((hs, LANES, MOBA_BLOCK), BF16),
        ],
        compiler_params=pltpu.CompilerParams(
            dimension_semantics=("arbitrary", "arbitrary", "arbitrary"),
            vmem_limit_bytes=VMEM_LIMIT),
        name="moba",
    )(q, k, vt, kmean)


GROUP_LANE0 = N_EXPERTS


def _split3(col, lane):
    hi = col.astype(BF16).astype(F32)
    mid = (col - hi).astype(BF16).astype(F32)
    lo = col - hi - mid
    return jnp.where(lane == 0, hi, jnp.where(lane == 1, mid, jnp.where(lane == 2, lo, 0.0))).astype(BF16)


def _mix_route_kernel(yc_ref, ya_ref, x_ref, wo_ref, g_ref, b_ref, wr_ref, br_ref,
                      x1_ref, xs_ref, pos_ref, cnt_ref):
    t = x_ref.shape[0]

    mix = _dot(yc_ref[...], wo_ref[0:D_CONV, :]) + _dot(ya_ref[...], wo_ref[D_CONV:, :])
    x1 = _layer_norm(DEEPNORM_ALPHA * x_ref[...] + mix, g_ref[...], b_ref[...])
    x1_ref[...] = x1

    x_hi = x1.astype(BF16)
    x_lo = (x1 - x_hi.astype(F32)).astype(BF16)
    wr = wr_ref[...]
    w_hi = wr.astype(BF16)
    w_lo = (wr - w_hi.astype(F32)).astype(BF16)
    logits = _dot(x_hi, w_hi) + _dot(x_lo, w_hi) + _dot(x_hi, w_lo) + br_ref[...]

    lane = lax.broadcasted_iota(jnp.int32, (t, LANES), 1).astype(F32)
    big = float(LANES)

    lg = jnp.where((lane >= GROUP_LANE0) & (lane < GROUP_LANE0 + N_GROUPS), logits, NEG_INF)
    gmax = jnp.max(lg, axis=-1, keepdims=True)
    g_p = 1.0 / jnp.sum(jnp.exp(lg - gmax), axis=-1, keepdims=True)
    g_idx = jnp.min(jnp.where(lg == gmax, lane, big), axis=-1, keepdims=True) - GROUP_LANE0

    lo = g_idx * EXPERTS_PER_GROUP
    le = jnp.where((lane >= lo) & (lane < lo + EXPERTS_PER_GROUP), logits, NEG_INF)
    m1 = jnp.max(le, axis=-1, keepdims=True)
    i1 = jnp.min(jnp.where(le == m1, lane, big), axis=-1, keepdims=True)
    le2 = jnp.where(lane == i1, NEG_INF, le)
    m2 = jnp.max(le2, axis=-1, keepdims=True)
    i2 = jnp.min(jnp.where(le2 == m2, lane, big), axis=-1, keepdims=True)
    p2 = jnp.exp(m2 - m1)
    den = 1.0 + p2
    w1 = g_p * (1.0 / den)
    w2 = g_p * (p2 / den)

    oh = (lane == i1) | (lane == i2)
    ohf = oh.astype(F32)
    r_i = lax.broadcasted_iota(jnp.int32, (t, t), 0)
    c_i = lax.broadcasted_iota(jnp.int32, (t, t), 1)
    before = _dot((r_i > c_i).astype(BF16), ohf.astype(BF16))
    r1 = jnp.sum(jnp.where(lane == i1, before, 0.0), axis=-1, keepdims=True)
    r2 = jnp.sum(jnp.where(lane == i2, before, 0.0), axis=-1, keepdims=True)

    cnt = jnp.sum(ohf, axis=0, keepdims=True)
    seg = jnp.floor((cnt + (ROW_GROUP - 1)) * (1.0 / ROW_GROUP)) * ROW_GROUP
    e_a = lax.broadcasted_iota(jnp.int32, (LANES, LANES), 0)
    e_b = lax.broadcasted_iota(jnp.int32, (LANES, LANES), 1)
    seg_start = _dot(jnp.broadcast_to(seg, (8, LANES)).astype(BF16), (e_a < e_b).astype(BF16))[0:1, :]
    pos1 = jnp.sum(jnp.where(lane == i1, seg_start, 0.0), axis=-1, keepdims=True) + r1
    pos2 = jnp.sum(jnp.where(lane == i2, seg_start, 0.0), axis=-1, keepdims=True) + r2
    cnt_ref[...] = jnp.broadcast_to(cnt, cnt_ref.shape)

    pos = jnp.where(lane == 0, pos1, pos2)
    pos_ref[...] = pos
    pos_rows = pos.T[0:8, :]

    srow = lax.broadcasted_iota(jnp.int32, (xs_ref.shape[0], t), 0).astype(F32)
    perm1 = (srow == pos_rows[0:1, :]).astype(BF16)
    perm2 = (srow == pos_rows[1:2, :]).astype(BF16)
    xs_ref[:, 0:D_MODEL] = _dot(perm1 + perm2, x_hi)
    xs_ref[:, D_MODEL:] = (_dot(perm1, _split3(w1, lane)) + _dot(perm2, _split3(w2, lane)))


def _mix_route(yc, ya, x2d, w_out_bf, g, b, w_r, b_r):
    n = x2d.shape[0]
    t = TOKEN_TILE
    nt = n // t
    row = lambda w: pl.BlockSpec((t, w), lambda i: (i, 0))
    full = lambda a: pl.BlockSpec(a.shape, lambda i: (0, 0))
    return pl.pallas_call(
        _mix_route_kernel,
        grid=(nt,),
        in_specs=[row(D_CONV), row(D_ATTN), row(D_MODEL), full(w_out_bf), full(g), full(b),
                  full(w_r), full(b_r)],
        out_specs=[row(D_MODEL), pl.BlockSpec((SORTED_TILE_ROWS, SORTED_ROW_WIDTH), lambda i: (i, 0)),
                   row(LANES), pl.BlockSpec((8, LANES), lambda i: (i, 0))],
        out_shape=[
            jax.ShapeDtypeStruct((n, D_MODEL), F32),
            jax.ShapeDtypeStruct((nt * SORTED_TILE_ROWS, SORTED_ROW_WIDTH), F32),
            jax.ShapeDtypeStruct((n, LANES), F32),
            jax.ShapeDtypeStruct((nt * 8, LANES), F32),
        ],
        compiler_params=pltpu.CompilerParams(
            dimension_semantics=("arbitrary",), vmem_limit_bytes=VMEM_LIMIT),
        name="mix_route",
    )(yc, ya, x2d, w_out_bf, g, b, w_r, b_r)


def _row_copies_start(rows, src_of, dst_of, sem):
    for r in rows:
        pltpu.make_async_copy(src_of(r), dst_of(r), sem).start()


def _row_copies_wait(src_rows, dst_rows, sem):
    pltpu.make_async_copy(src_rows, dst_rows, sem).wait()


def _experts_kernel(be_ref, first_ref, nxt_ref, par_ref, nu_ref, cur_ref, nxg_ref, xs_hbm,
                    wg_hbm, wu_hbm, wd_hbm, yb_ref, xbuf, wg_buf, wu_buf, wd_buf, wg_bf, wu_bf, wd_bf,
                    xsem, wsem):
    i = pl.program_id(0)
    used = i < nu_ref[0]
    slot = par_ref[i]
    xslot = i % 2
    g = ROW_GROUP
    weights = ((wg_hbm, wg_buf), (wu_hbm, wu_buf), (wd_hbm, wd_buf))

    def fetch(e, s):
        for k, (w_hbm, w_buf) in enumerate(weights):
            pltpu.make_async_copy(w_hbm.at[e], w_buf.at[s], wsem.at[s, k]).start()

    def fetch_wait(s):
        for k, (w_hbm, w_buf) in enumerate(weights):
            pltpu.make_async_copy(w_hbm.at[0], w_buf.at[s], wsem.at[s, k]).wait()

    def gather_rows(grp_ref, s):
        _row_copies_start(
            range(DISPATCH_BLOCK // g),
            lambda q: xs_hbm.at[pl.ds(pl.multiple_of(grp_ref[0, q], g), g)],
            lambda q: xbuf.at[s, pl.ds(q * g, g)], xsem.at[s])

    @pl.when(i == 0)
    def _():
        fetch(be_ref[0], slot)
        gather_rows(cur_ref, 0)

    @pl.when(i + 1 < nu_ref[0])
    def _():
        gather_rows(nxg_ref, 1 - xslot)

    @pl.when(used & (first_ref[i] == 1))
    def _():
        fetch_wait(slot)

        @pl.when(nxt_ref[i] >= 0)
        def _():
            fetch(nxt_ref[i], 1 - slot)

        wg_bf[...] = wg_buf[slot].astype(BF16)
        wu_bf[...] = wu_buf[slot].astype(BF16)
        wd_bf[...] = wd_buf[slot].astype(BF16)

    @pl.when(used)
    def _():
        _row_copies_wait(xs_hbm.at[pl.ds(0, DISPATCH_BLOCK)], xbuf.at[xslot], xsem.at[xslot])
        xb = xbuf[xslot, :, 0:D_MODEL].astype(BF16)
        wrow = xbuf[xslot, :, D_MODEL:]
        weight = wrow[:, 0:1] + wrow[:, 1:2] + wrow[:, 2:3]
        gate = _dot(xb, wg_bf[...])
        up = _dot(xb, wu_bf[...])
        hid = (jax.nn.silu(gate) * up).astype(BF16)
        yb_ref[...] = _dot(hid, wd_bf[...]) * weight

    @pl.when(jnp.logical_not(used))
    def _():
        yb_ref[...] = jnp.zeros_like(yb_ref)


def _experts(blk_expert, blk_first, blk_next, blk_slot, n_used, grp_row, xs, w_gate, w_up, w_down):
    n_blk = blk_expert.shape[0]
    r = DISPATCH_BLOCK
    gpb = r // ROW_GROUP
    grp3 = grp_row.reshape(n_blk, 1, gpb)
    smem_blk = lambda f: pl.BlockSpec((None, 1, gpb), f, memory_space=pltpu.SMEM)
    grid_spec = pltpu.PrefetchScalarGridSpec(
        num_scalar_prefetch=5,
        grid=(n_blk,),
        in_specs=[
            smem_blk(lambda i, *_: (i, 0, 0)),
            smem_blk(lambda i, *_: (jnp.minimum(i + 1, n_blk - 1), 0, 0)),
            pl.BlockSpec(memory_space=pl.ANY),
            pl.BlockSpec(memory_space=pl.ANY),
            pl.BlockSpec(memory_space=pl.ANY),
            pl.BlockSpec(memory_space=pl.ANY),
        ],
        out_specs=pl.BlockSpec((r, D_MODEL), lambda i, *_: (i, 0)),
        scratch_shapes=[
            pltpu.VMEM((2, r, xs.shape[1]), F32),
            pltpu.VMEM((2,) + w_gate.shape[1:], w_gate.dtype),
            pltpu.VMEM((2,) + w_up.shape[1:], w_up.dtype),
            pltpu.VMEM((2,) + w_down.shape[1:], w_down.dtype),
            pltpu.VMEM(w_gate.shape[1:], BF16),
            pltpu.VMEM(w_up.shape[1:], BF16),
            pltpu.VMEM(w_down.shape[1:], BF16),
            pltpu.SemaphoreType.DMA((2,)),
            pltpu.SemaphoreType.DMA((2, 3)),
        ],
    )
    return pl.pallas_call(
        _experts_kernel,
        grid_spec=grid_spec,
        out_shape=jax.ShapeDtypeStruct((n_blk * r, D_MODEL), F32),
        compiler_params=pltpu.CompilerParams(
            dimension_semantics=("arbitrary",), vmem_limit_bytes=VMEM_LIMIT),
        name="experts",
    )(blk_expert, blk_first, blk_next, blk_slot, n_used, grp3, grp3, xs, w_gate, w_up, w_down)


def _combine_kernel(cur_ref, nxt_ref, x1_ref, p_ref, pos_ref, wpg_ref, wpp_ref, g_ref, b_ref,
                    yb_hbm, o_ref, ybuf, sem):
    i = pl.program_id(0)
    n_steps = pl.num_programs(0)
    t = x1_ref.shape[0]
    slot = i % 2
    g = ROW_GROUP
    n_groups = ybuf.shape[1] // g

    def gather(grp_ref, s):
        _row_copies_start(
            range(n_groups),
            lambda q: yb_hbm.at[pl.ds(pl.multiple_of(grp_ref[0, q], g), g)],
            lambda q: ybuf.at[s, pl.ds(q * g, g)], sem.at[s])

    @pl.when(i == 0)
    def _():
        gather(cur_ref, 0)

    @pl.when(i + 1 < n_steps)
    def _():
        gather(nxt_ref, 1 - slot)

    x1 = x1_ref[...]
    ple = jax.nn.sigmoid(_dot(x1.astype(BF16), wpg_ref[...])) * _dot(p_ref[...].astype(BF16), wpp_ref[...])

    _row_copies_wait(yb_hbm.at[pl.ds(0, ybuf.shape[1])], ybuf.at[slot], sem.at[slot])
    pos = pos_ref[...]
    scol = lax.broadcasted_iota(jnp.int32, (t, ybuf.shape[1]), 1).astype(F32)
    pick = ((scol == pos[:, 0:1]) | (scol == pos[:, 1:2])).astype(BF16)
    ffn = None
    for c in range(0, ybuf.shape[1], UNSORT_CHUNK):
        ys = ybuf[slot, c:c + UNSORT_CHUNK, :]
        hi = ys.astype(BF16)
        lo = (ys - hi.astype(F32)).astype(BF16)
        pc = pick[:, c:c + UNSORT_CHUNK]
        part = _dot(pc, hi) + _dot(pc, lo)
        ffn = part if ffn is None else ffn + part
    o_ref[...] = _layer_norm(DEEPNORM_ALPHA * x1 + ffn + ple, g_ref[...], b_ref[...])


def _combine(tile_grp, x1, p2d, pos, w_pg_bf, w_pp_bf, g, b, yb):
    n = x1.shape[0]
    t = TOKEN_TILE
    n_steps = n // t
    gpt = SORTED_TILE_ROWS // ROW_GROUP
    grp3 = tile_grp.reshape(n_steps, 1, gpt)
    smem_blk = lambda f: pl.BlockSpec((None, 1, gpt), f, memory_space=pltpu.SMEM)
    row = lambda w: pl.BlockSpec((t, w), lambda i: (i, 0))
    full = lambda a: pl.BlockSpec(a.shape, lambda i: (0, 0))
    return pl.pallas_call(
        _combine_kernel,
        grid=(n_steps,),
        in_specs=[
            smem_blk(lambda i: (i, 0, 0)),
            smem_blk(lambda i: (jnp.minimum(i + 1, n_steps - 1), 0, 0)),
            row(D_MODEL), row(PLE_DIM), row(LANES), full(w_pg_bf), full(w_pp_bf), full(g), full(b),
            pl.BlockSpec(memory_space=pl.ANY),
        ],
        out_specs=row(D_MODEL),
        out_shape=jax.ShapeDtypeStruct((n, D_MODEL), F32),
        scratch_shapes=[pltpu.VMEM((2, SORTED_TILE_ROWS, D_MODEL), F32), pltpu.SemaphoreType.DMA((2,))],
        compiler_params=pltpu.CompilerParams(
            dimension_semantics=("arbitrary",), vmem_limit_bytes=VMEM_LIMIT),
        name="combine",
    )(grp3, grp3, x1, p2d, pos, w_pg_bf, w_pp_bf, g, b, yb)


def _layer(x, p, w_in, w_conv, w_out, ln1_g, ln1_b, w_router_g, b_router_g, w_router_e,
           b_router_e, w_gate, w_up, w_down, w_ple_gate, w_ple_proj, ln2_g, ln2_b):
    b, s, d = x.shape
    n = b * s
    r = DISPATCH_BLOCK

    yc, q, k, vt, kmean = _proj_conv(x, w_in.astype(BF16), w_conv)
    ya = _moba(q, k, vt, kmean.reshape(b, s // MOBA_BLOCK, D_ATTN))

    w_r = jnp.zeros((d, LANES), F32).at[:, :N_EXPERTS].set(w_router_e)
    w_r = w_r.at[:, GROUP_LANE0:GROUP_LANE0 + N_GROUPS].set(w_router_g)
    b_r = jnp.zeros((1, LANES), F32).at[0, :N_EXPERTS].set(b_router_e)
    b_r = b_r.at[0, GROUP_LANE0:GROUP_LANE0 + N_GROUPS].set(b_router_g)
    x1, xs, pos, cnt = _mix_route(
        yc.reshape(n, D_CONV), ya.reshape(n, D_ATTN), x.reshape(n, d), w_out.astype(BF16),
        ln1_g.reshape(1, d), ln1_b.reshape(1, d), w_r, b_r)

    nt = n // TOKEN_TILE
    g = ROW_GROUP
    seg = (cnt.reshape(nt, 8, LANES)[:, 0, :N_EXPERTS].astype(jnp.int32) + g - 1) // g * g
    rows_e = jnp.sum(seg, axis=0)
    padded = (rows_e + r - 1) // r * r
    row_end = jnp.cumsum(padded)
    row_start = row_end - padded
    seg_dst = row_start[None, :] + jnp.cumsum(seg, axis=0) - seg
    seg_off = jnp.cumsum(seg, axis=1) - seg
    seg_src = (jnp.arange(nt, dtype=jnp.int32) * SORTED_TILE_ROWS)[:, None] + seg_off
    n_blk = -(-(2 * n + (g - 1) * N_EXPERTS * nt) // r) + N_EXPERTS

    blk_row0 = jnp.arange(n_blk, dtype=jnp.int32) * r
    blk_expert = jnp.minimum(
        jnp.sum((row_end[None, :] <= blk_row0[:, None]).astype(jnp.int32), axis=1), N_EXPERTS - 1)
    n_used = row_end[-1:] // r
    eid = jnp.arange(N_EXPERTS, dtype=jnp.int32)
    blk_onehot = (blk_expert[:, None] == eid[None, :]).astype(jnp.int32)

    def of_block(tab):
        return jnp.sum(blk_onehot[:, :, None] * tab.T[None, :, :], axis=1)
    dst_b, src_b, len_b = of_block(seg_dst)[:, None, :], of_block(seg_src)[:, None, :], of_block(seg)[:, None, :]
    grp_first = (blk_row0[:, None] + jnp.arange(r // g, dtype=jnp.int32)[None, :] * g)[:, :, None]
    inside = (grp_first >= dst_b) & (grp_first < dst_b + len_b)
    grp_row = jnp.where(jnp.any(inside, axis=2),
                        jnp.sum(jnp.where(inside, src_b + grp_first - dst_b, 0), axis=2),
                        SORTED_TILE_ROWS - g)

    tile_first = (jnp.arange(SORTED_TILE_ROWS // g, dtype=jnp.int32) * g)[None, :, None]
    t_off, t_len, t_dst = seg_off[:, None, :], seg[:, None, :], seg_dst[:, None, :]
    within = (tile_first >= t_off) & (tile_first < t_off + t_len)
    tile_grp = jnp.sum(jnp.where(within, t_dst + tile_first - t_off, 0), axis=2)

    nonempty = rows_e > 0
    later = jnp.where((eid[None, :] > eid[:, None]) & nonempty[None, :], eid[None, :], N_EXPERTS)
    next_expert = jnp.min(later, axis=1)
    next_expert = jnp.where(next_expert == N_EXPERTS, -1, next_expert)
    ordinal = jnp.cumsum(nonempty.astype(jnp.int32)) - 1
    blk_first = jnp.concatenate(
        [jnp.ones((1,), jnp.int32), (blk_expert[1:] != blk_expert[:-1]).astype(jnp.int32)])
    blk_next = jnp.sum(blk_onehot * next_expert[None, :], axis=1)
    blk_slot = jnp.sum(blk_onehot * ordinal[None, :], axis=1) % 2

    yb = _experts(blk_expert, blk_first, blk_next, blk_slot, n_used, grp_row, xs, w_gate, w_up, w_down)
    out = _combine(tile_grp, x1, p.reshape(n, PLE_DIM), pos, w_ple_gate.astype(BF16),
                   w_ple_proj.astype(BF16), ln2_g.reshape(1, d), ln2_b.reshape(1, d), yb)
    return out.reshape(b, s, d)


def kernel(x, p, w_in, w_conv, w_out, ln1_g, ln1_b, w_router_g, b_router_g, w_router_e,
           b_router_e, w_gate, w_up, w_down, w_ple_gate, w_ple_proj, ln2_g, ln2_b):
    for i in range(DEPTH):
        x = _layer(x, p[i], w_in[i], w_conv[i], w_out[i], ln1_g[i], ln1_b[i], w_router_g[i],
                   b_router_g[i], w_router_e[i], b_router_e[i], w_gate[i], w_up[i], w_down[i],
                   w_ple_gate[i], w_ple_proj[i], ln2_g[i], ln2_b[i])
    return x
```

```python
import functools

import jax
import jax.numpy as jnp
from jax import lax
from jax.experimental import pallas as pl
from jax.experimental.pallas import tpu as pltpu

D_MODEL = 1024
D_CONV = 512
N_HEADS = 8
HEAD_DIM = 64
D_ATTN = N_HEADS * HEAD_DIM
D_IN = 3 * D_CONV + 3 * D_ATTN
MOBA_BLOCK = 256
MOBA_TOPK = 3
N_GROUPS = 4
EXPERTS_PER_GROUP = 8
N_EXPERTS = N_GROUPS * EXPERTS_PER_GROUP
DISPATCH_BLOCK = 256
ROW_GROUP = 8
PLE_DIM = 256
LN_EPS = 1e-5
DEPTH = 1
DEEPNORM_ALPHA = (2 * DEPTH) ** 0.25

LANES = 128
PROJ_ROWS = 512
TOKEN_TILE = 256
SORTED_TILE_ROWS = 2 * TOKEN_TILE + ROW_GROUP * N_EXPERTS
UNSORT_CHUNK = 256
SORTED_ROW_WIDTH = D_MODEL + LANES
KV_CHUNK_SHIFT = 2
KV_CHUNK_BLOCKS = 1 << KV_CHUNK_SHIFT
LOOP_CHUNKS = 4
assert LOOP_CHUNKS % 2 == 0
MOBA_HEADS_PER_STEP = 4
assert MOBA_HEADS_PER_STEP % KV_CHUNK_BLOCKS == 0
BF16_SUBLANES = 16
ACC_ROWS = HEAD_DIM + BF16_SUBLANES
Q_SCALE = HEAD_DIM ** -0.5 * 1.4426950408889634
VMEM_LIMIT = 48 * 1024 * 1024

F32 = jnp.float32
BF16 = jnp.bfloat16
NEG_INF = float("-inf")


def _dot(a, b):
    return jnp.dot(a, b, preferred_element_type=F32)


def _layer_norm(h, g, b):
    mu = jnp.mean(h, axis=-1, keepdims=True)
    d = h - mu
    var = jnp.mean(d * d, axis=-1, keepdims=True)
    return d * lax.rsqrt(var + LN_EPS) * g + b


def _proj_conv_kernel(x_ref, w_ref, wc_ref, yc_ref, q_ref, k_ref, vt_ref, km_ref, ubuf):
    s = pl.program_id(1)
    rows = x_ref.shape[0]
    xb = x_ref[...].astype(BF16)

    def sect(i):
        return _dot(xb, w_ref[:, i * D_CONV:(i + 1) * D_CONV])

    @pl.when(s == 0)
    def _():
        ubuf[0:8, :] = jnp.zeros((8, D_CONV), F32)

    u = sect(1) * sect(2)
    ubuf[8:8 + rows, :] = u
    wc = wc_ref[...]
    conv = wc[0:1, :] * ubuf[6:6 + rows, :] + wc[1:2, :] * ubuf[7:7 + rows, :] + wc[2:3, :] * u
    yc_ref[...] = (sect(0) * conv).astype(BF16)
    ubuf[0:8, :] = ubuf[rows:rows + 8, :]

    q_ref[...] = (sect(3) * Q_SCALE).astype(BF16)
    k = sect(4)
    k_ref[...] = k.astype(BF16)
    for i in range(rows // MOBA_BLOCK):
        km_ref[i] = jnp.mean(k[i * MOBA_BLOCK:(i + 1) * MOBA_BLOCK, :], axis=0, keepdims=True)
    vt = sect(5).T
    ones = jnp.ones((ACC_ROWS - HEAD_DIM, rows), F32)
    vt = jnp.concatenate(
        [piece for h in range(N_HEADS) for piece in (vt[h * HEAD_DIM:(h + 1) * HEAD_DIM, :], ones)],
        axis=0).astype(BF16)
    for i in range(rows // MOBA_BLOCK):
        vt_ref[i] = vt[:, i * MOBA_BLOCK:(i + 1) * MOBA_BLOCK]


def _proj_conv(x, w_in_bf, w_conv):
    b, s, _ = x.shape
    nb = s // MOBA_BLOCK
    grid = (b, s // PROJ_ROWS)
    seq_spec = pl.BlockSpec((None, PROJ_ROWS, D_CONV), lambda i, j: (i, j, 0))
    return pl.pallas_call(
        _proj_conv_kernel,
        grid=grid,
        in_specs=[
            pl.BlockSpec((None, PROJ_ROWS, D_MODEL), lambda i, j: (i, j, 0)),
            pl.BlockSpec((D_MODEL, D_IN), lambda i, j: (0, 0)),
            pl.BlockSpec((3, D_CONV), lambda i, j: (0, 0)),
        ],
        out_specs=[
            seq_spec, seq_spec, seq_spec,
            pl.BlockSpec((None, PROJ_ROWS // MOBA_BLOCK, N_HEADS * ACC_ROWS, MOBA_BLOCK),
                         lambda i, j: (i, j, 0, 0)),
            pl.BlockSpec((None, PROJ_ROWS // MOBA_BLOCK, 1, D_ATTN), lambda i, j: (i, j, 0, 0)),
        ],
        out_shape=[
            jax.ShapeDtypeStruct((b, s, D_CONV), BF16),
            jax.ShapeDtypeStruct((b, s, D_ATTN), BF16),
            jax.ShapeDtypeStruct((b, s, D_ATTN), BF16),
            jax.ShapeDtypeStruct((b, nb, N_HEADS * ACC_ROWS, MOBA_BLOCK), BF16),
            jax.ShapeDtypeStruct((b, nb, 1, D_ATTN), F32),
        ],
        scratch_shapes=[pltpu.VMEM((PROJ_ROWS + 8, D_CONV), F32)],
        compiler_params=pltpu.CompilerParams(
            dimension_semantics=("arbitrary", "arbitrary"), vmem_limit_bytes=VMEM_LIMIT),
        name="proj_conv",
    )(x, w_in_bf, w_conv)


def _moba_kernel(q_ref, k_ref, vt_ref, km_ref, o_ref, bias_ref, sd_ref, s_ref, mt_ref,
                 m_ref, acc_ref, qt_ref):
    j = pl.program_id(2)
    blk = MOBA_BLOCK
    nb = km_ref.shape[0]
    n_heads = vt_ref.shape[1] // ACC_ROWS
    heads = range(n_heads)
    heads_per_block = n_heads // KV_CHUNK_BLOCKS
    lead = j & (KV_CHUNK_BLOCKS - 1)
    n_chunks = lax.shift_right_logical(j, KV_CHUNK_SHIFT)
    lead0 = n_chunks * KV_CHUNK_BLOCKS

    def pair_lanes(h):
        return slice((h // 2) * LANES, (h // 2 + 1) * LANES)

    def prepare():
        row = lax.broadcasted_iota(jnp.int32, (LANES, blk), 0)
        qt_pairs = [q_ref[:, pair_lanes(2 * pr)].astype(F32).T for pr in range(n_heads // 2)]
        bidx = lax.broadcasted_iota(jnp.int32, (nb, blk), 0)
        for h in heads:
            qt = jnp.where((row >= HEAD_DIM) == bool(h % 2), qt_pairs[h // 2], 0.0).astype(BF16)
            qt_ref[h] = qt
            km = km_ref[:, pair_lanes(h)]
            km_hi = km.astype(BF16)
            km_lo = (km - km_hi.astype(F32)).astype(BF16)
            g = _dot(km_hi, qt) + _dot(km_lo, qt)
            g = jnp.where(bidx < j, g, NEG_INF)
            sel = jnp.zeros((nb, blk), jnp.bool_)
            for _ in range(MOBA_TOPK):
                mx = jnp.max(g, axis=0, keepdims=True)
                cand = jnp.where((g == mx) & (mx > NEG_INF), bidx, nb)
                pick = bidx == jnp.min(cand, axis=0, keepdims=True)
                sel = sel | pick
                g = jnp.where(pick, NEG_INF, g)
            bias_ref[h] = jnp.where(sel, 0.0, NEG_INF)

    def k_block(n, h):
        return k_ref[pl.ds(pl.multiple_of(n * blk, blk), blk), pair_lanes(h)]

    def vt_block(n, h):
        return vt_ref[n, h * ACC_ROWS:(h + 1) * ACC_ROWS, :]

    def stage_head(c, u, slot, h):
        n = c * KV_CHUNK_BLOCKS + u
        sc = _dot(k_block(n, h), qt_ref[h]) + bias_ref[h, pl.ds(n, 1), :]
        s_ref[slot, h, u * blk:(u + 1) * blk, :] = sc
        return jnp.max(sc, axis=0, keepdims=True)

    def stage_block(c, u, slot):
        return [stage_head(c, u, slot, h) for h in heads]

    def store_staged_max(slot, mts):
        for h in heads:
            mt_ref[slot, h] = functools.reduce(jnp.maximum, [m[h] for m in mts])

    def pv(vt_h, sc, m_new):
        return _dot(vt_h, jnp.exp2(sc - m_new).astype(BF16))

    def rescale(h, mt):
        m_old = m_ref[h]
        m_new = jnp.maximum(m_old, mt)
        m_ref[h] = m_new
        return m_new, jnp.exp2(m_old - m_new)

    def step(consume, stage):
        if consume is not None:
            cc, cs = consume
            scale = [rescale(h, mt_ref[cs, h]) for h in heads]
        mts = []
        for u in range(KV_CHUNK_BLOCKS):
            staged = heads if stage is not None else []
            updated = heads[u * heads_per_block:(u + 1) * heads_per_block] if consume is not None else []
            products = [(h, v) for h in updated for v in range(KV_CHUNK_BLOCKS)]
            mt_u = []
            parts = {h: [] for h in updated}
            for i in range(max(len(staged), len(products))):
                if i < len(staged):
                    mt_u.append(stage_head(stage[0], u, stage[1], staged[i]))
                if i < len(products):
                    h, v = products[i]
                    parts[h].append(pv(vt_block(cc * KV_CHUNK_BLOCKS + v, h),
                                       s_ref[cs, h, v * blk:(v + 1) * blk, :], scale[h][0]))
            mts.append(mt_u)
            for h in updated:
                acc_ref[h] = scale[h][1] * acc_ref[h] + functools.reduce(jnp.add, parts[h])
        if stage is not None:
            store_staged_max(stage[1], mts)

    kpos = lax.broadcasted_iota(jnp.int32, (blk, blk), 0)
    qpos = lax.broadcasted_iota(jnp.int32, (blk, blk), 1)
    causal = kpos <= qpos

    def first_update(n_lead):
        prepare()
        m_first = []
        for h in heads:
            sc = jnp.where(causal, _dot(k_block(j, h), qt_ref[h]), NEG_INF)
            sd_ref[h] = sc
            m_first.append(jnp.max(sc, axis=0, keepdims=True))
        for u in range(n_lead):
            for h in heads:
                sc = _dot(k_block(lead0 + u, h), qt_ref[h]) + bias_ref[h, pl.ds(lead0 + u, 1), :]
                s_ref[1, h, u * blk:(u + 1) * blk, :] = sc
                m_first[h] = jnp.maximum(m_first[h], jnp.max(sc, axis=0, keepdims=True))
        mts = []
        for u in range(KV_CHUNK_BLOCKS):
            mts.append(stage_block(0, u, 0))
            for h in heads[u * heads_per_block:(u + 1) * heads_per_block]:
                parts = [pv(vt_block(j, h), sd_ref[h], m_first[h])]
                parts += [pv(vt_block(lead0 + v, h), s_ref[1, h, v * blk:(v + 1) * blk, :], m_first[h])
                          for v in range(n_lead)]
                m_ref[h] = m_first[h]
                acc_ref[h] = functools.reduce(jnp.add, parts)
        store_staged_max(0, mts)

    for n_lead in range(KV_CHUNK_BLOCKS):
        pl.when(lead == n_lead)(functools.partial(first_update, n_lead))

    steps = jnp.maximum(n_chunks - 1, 0)

    def body(i, carry):
        c = LOOP_CHUNKS * i
        for d in range(LOOP_CHUNKS):
            step((c + d, d % 2), (c + d + 1, (d + 1) % 2))
        return carry

    lax.fori_loop(0, steps // LOOP_CHUNKS, body, 0)

    c0 = steps // LOOP_CHUNKS * LOOP_CHUNKS
    for rem in range(LOOP_CHUNKS):
        @pl.when((n_chunks > 0) & (steps % LOOP_CHUNKS == rem))
        def _(rem=rem):
            for d in range(rem):
                step((c0 + d, d % 2), (c0 + d + 1, (d + 1) % 2))
            step((c0 + rem, rem % 2), None)

    outs = [acc_ref[h, 0:HEAD_DIM, :] / acc_ref[h, HEAD_DIM:HEAD_DIM + 1, :] for h in heads]
    o_ref[...] = jnp.concatenate(outs, axis=0).T.astype(BF16)


def _moba(q, k, vt, kmean):
    b, s, _ = q.shape
    nb = s // MOBA_BLOCK
    w = MOBA_HEADS_PER_STEP * HEAD_DIM
    hs = MOBA_HEADS_PER_STEP
    return pl.pallas_call(
        _moba_kernel,
        grid=(b, D_ATTN // w, nb),
        in_specs=[
            pl.BlockSpec((None, MOBA_BLOCK, w), lambda i, p, j: (i, j, p)),
            pl.BlockSpec((None, s, w), lambda i, p, j: (i, 0, p)),
            pl.BlockSpec((None, nb, hs * ACC_ROWS, MOBA_BLOCK), lambda i, p, j: (i, 0, p, 0)),
            pl.BlockSpec((None, nb, w), lambda i, p, j: (i, 0, p)),
        ],
        out_specs=pl.BlockSpec((None, MOBA_BLOCK, w), lambda i, p, j: (i, j, p)),
        out_shape=jax.ShapeDtypeStruct((b, s, D_ATTN), BF16),
        scratch_shapes=[
            pltpu.VMEM((hs, nb, MOBA_BLOCK), F32),
            pltpu.VMEM((hs, MOBA_BLOCK, MOBA_BLOCK), F32),
            pltpu.VMEM((2, hs, KV_CHUNK_BLOCKS * MOBA_BLOCK, MOBA_BLOCK), F32),
            pltpu.VMEM((3, hs, 1, MOBA_BLOCK), F32),
            pltpu.VMEM((hs, 1, MOBA_BLOCK), F32),
            pltpu.VMEM((hs, ACC_ROWS, MOBA_BLOCK), F32),
            pltpu.VMEM((hs, LANES, MOBA_BLOCK), BF16),
        ],
        compiler_params=pltpu.CompilerParams(
            dimension_semantics=("arbitrary", "arbitrary", "arbitrary"),
            vmem_limit_bytes=VMEM_LIMIT),
        name="moba",
    )(q, k, vt, kmean)


GROUP_LANE0 = N_EXPERTS


def _split3(col, lane):
    hi = col.astype(BF16).astype(F32)
    mid = (col - hi).astype(BF16).astype(F32)
    lo = col - hi - mid
    return jnp.where(lane == 0, hi, jnp.where(lane == 1, mid, jnp.where(lane == 2, lo, 0.0))).astype(BF16)


def _mix_route_kernel(yc_ref, ya_ref, x_ref, wo_ref, g_ref, b_ref, wr_ref, br_ref,
                      x1_ref, xs_ref, pos_ref, cnt_ref):
    t = x_ref.shape[0]

    mix = _dot(yc_ref[...], wo_ref[0:D_CONV, :]) + _dot(ya_ref[...], wo_ref[D_CONV:, :])
    x1 = _layer_norm(DEEPNORM_ALPHA * x_ref[...] + mix, g_ref[...], b_ref[...])
    x1_ref[...] = x1

    x_hi = x1.astype(BF16)
    x_lo = (x1 - x_hi.astype(F32)).astype(BF16)
    wr = wr_ref[...]
    w_hi = wr.astype(BF16)
    w_lo = (wr - w_hi.astype(F32)).astype(BF16)
    logits = _dot(x_hi, w_hi) + _dot(x_lo, w_hi) + _dot(x_hi, w_lo) + br_ref[...]

    lane = lax.broadcasted_iota(jnp.int32, (t, LANES), 1).astype(F32)
    big = float(LANES)

    lg = jnp.where((lane >= GROUP_LANE0) & (lane < GROUP_LANE0 + N_GROUPS), logits, NEG_INF)
    gmax = jnp.max(lg, axis=-1, keepdims=True)
    g_p = 1.0 / jnp.sum(jnp.exp(lg - gmax), axis=-1, keepdims=True)
    g_idx = jnp.min(jnp.where(lg == gmax, lane, big), axis=-1, keepdims=True) - GROUP_LANE0

    lo = g_idx * EXPERTS_PER_GROUP
    le = jnp.where((lane >= lo) & (lane < lo + EXPERTS_PER_GROUP), logits, NEG_INF)
    m1 = jnp.max(le, axis=-1, keepdims=True)
    i1 = jnp.min(jnp.where(le == m1, lane, big), axis=-1, keepdims=True)
    le2 = jnp.where(lane == i1, NEG_INF, le)
    m2 = jnp.max(le2, axis=-1, keepdims=True)
    i2 = jnp.min(jnp.where(le2 == m2, lane, big), axis=-1, keepdims=True)
    p2 = jnp.exp(m2 - m1)
    den = 1.0 + p2
    w1 = g_p * (1.0 / den)
    w2 = g_p * (p2 / den)

    oh = (lane == i1) | (lane == i2)
    ohf = oh.astype(F32)
    r_i = lax.broadcasted_iota(jnp.int32, (t, t), 0)
    c_i = lax.broadcasted_iota(jnp.int32, (t, t), 1)
    before = _dot((r_i > c_i).astype(BF16), ohf.astype(BF16))
    r1 = jnp.sum(jnp.where(lane == i1, before, 0.0), axis=-1, keepdims=True)
    r2 = jnp.sum(jnp.where(lane == i2, before, 0.0), axis=-1, keepdims=True)

    cnt = jnp.sum(ohf, axis=0, keepdims=True)
    seg = jnp.floor((cnt + (ROW_GROUP - 1)) * (1.0 / ROW_GROUP)) * ROW_GROUP
    e_a = lax.broadcasted_iota(jnp.int32, (LANES, LANES), 0)
    e_b = lax.broadcasted_iota(jnp.int32, (LANES, LANES), 1)
    seg_start = _dot(jnp.broadcast_to(seg, (8, LANES)).astype(BF16), (e_a < e_b).astype(BF16))[0:1, :]
    pos1 = jnp.sum(jnp.where(lane == i1, seg_start, 0.0), axis=-1, keepdims=True) + r1
    pos2 = jnp.sum(jnp.where(lane == i2, seg_start, 0.0), axis=-1, keepdims=True) + r2
    cnt_ref[...] = jnp.broadcast_to(cnt, cnt_ref.shape)

    pos = jnp.where(lane == 0, pos1, pos2)
    pos_ref[...] = pos
    pos_rows = pos.T[0:8, :]

    srow = lax.broadcasted_iota(jnp.int32, (xs_ref.shape[0], t), 0).astype(F32)
    perm1 = (srow == pos_rows[0:1, :]).astype(BF16)
    perm2 = (srow == pos_rows[1:2, :]).astype(BF16)
    xs_ref[:, 0:D_MODEL] = _dot(perm1 + perm2, x_hi)
    xs_ref[:, D_MODEL:] = (_dot(perm1, _split3(w1, lane)) + _dot(perm2, _split3(w2, lane)))


def _mix_route(yc, ya, x2d, w_out_bf, g, b, w_r, b_r):
    n = x2d.shape[0]
    t = TOKEN_TILE
    nt = n // t
    row = lambda w: pl.BlockSpec((t, w), lambda i: (i, 0))
    full = lambda a: pl.BlockSpec(a.shape, lambda i: (0, 0))
    return pl.pallas_call(
        _mix_route_kernel,
        grid=(nt,),
        in_specs=[row(D_CONV), row(D_ATTN), row(D_MODEL), full(w_out_bf), full(g), full(b),
                  full(w_r), full(b_r)],
        out_specs=[row(D_MODEL), pl.BlockSpec((SORTED_TILE_ROWS, SORTED_ROW_WIDTH), lambda i: (i, 0)),
                   row(LANES), pl.BlockSpec((8, LANES), lambda i: (i, 0))],
        out_shape=[
            jax.ShapeDtypeStruct((n, D_MODEL), F32),
            jax.ShapeDtypeStruct((nt * SORTED_TILE_ROWS, SORTED_ROW_WIDTH), F32),
            jax.ShapeDtypeStruct((n, LANES), F32),
            jax.ShapeDtypeStruct((nt * 8, LANES), F32),
        ],
        compiler_params=pltpu.CompilerParams(
            dimension_semantics=("arbitrary",), vmem_limit_bytes=VMEM_LIMIT),
        name="mix_route",
    )(yc, ya, x2d, w_out_bf, g, b, w_r, b_r)


def _row_copies_start(rows, src_of, dst_of, sem):
    for r in rows:
        pltpu.make_async_copy(src_of(r), dst_of(r), sem).start()


def _row_copies_wait(src_rows, dst_rows, sem):
    pltpu.make_async_copy(src_rows, dst_rows, sem).wait()


def _experts_kernel(be_ref, first_ref, nxt_ref, par_ref, nu_ref, cur_ref, nxg_ref, xs_hbm,
                    wg_hbm, wu_hbm, wd_hbm, yb_ref, xbuf, wg_buf, wu_buf, wd_buf, wg_bf, wu_bf, wd_bf,
                    xsem, wsem):
    i = pl.program_id(0)
    used = i < nu_ref[0]
    slot = par_ref[i]
    xslot = i % 2
    g = ROW_GROUP
    weights = ((wg_hbm, wg_buf), (wu_hbm, wu_buf), (wd_hbm, wd_buf))

    def fetch(e, s):
        for k, (w_hbm, w_buf) in enumerate(weights):
            pltpu.make_async_copy(w_hbm.at[e], w_buf.at[s], wsem.at[s, k]).start()

    def fetch_wait(s):
        for k, (w_hbm, w_buf) in enumerate(weights):
            pltpu.make_async_copy(w_hbm.at[0], w_buf.at[s], wsem.at[s, k]).wait()

    def gather_rows(grp_ref, s):
        _row_copies_start(
            range(DISPATCH_BLOCK // g),
            lambda q: xs_hbm.at[pl.ds(pl.multiple_of(grp_ref[0, q], g), g)],
            lambda q: xbuf.at[s, pl.ds(q * g, g)], xsem.at[s])

    @pl.when(i == 0)
    def _():
        fetch(be_ref[0], slot)
        gather_rows(cur_ref, 0)

    @pl.when(i + 1 < nu_ref[0])
    def _():
        gather_rows(nxg_ref, 1 - xslot)

    @pl.when(used & (first_ref[i] == 1))
    def _():
        fetch_wait(slot)

        @pl.when(nxt_ref[i] >= 0)
        def _():
            fetch(nxt_ref[i], 1 - slot)

        wg_bf[...] = wg_buf[slot].astype(BF16)
        wu_bf[...] = wu_buf[slot].astype(BF16)
        wd_bf[...] = wd_buf[slot].astype(BF16)

    @pl.when(used)
    def _():
        _row_copies_wait(xs_hbm.at[pl.ds(0, DISPATCH_BLOCK)], xbuf.at[xslot], xsem.at[xslot])
        xb = xbuf[xslot, :, 0:D_MODEL].astype(BF16)
        wrow = xbuf[xslot, :, D_MODEL:]
        weight = wrow[:, 0:1] + wrow[:, 1:2] + wrow[:, 2:3]
        gate = _dot(xb, wg_bf[...])
        up = _dot(xb, wu_bf[...])
        hid = (jax.nn.silu(gate) * up).astype(BF16)
        yb_ref[...] = _dot(hid, wd_bf[...]) * weight

    @pl.when(jnp.logical_not(used))
    def _():
        yb_ref[...] = jnp.zeros_like(yb_ref)


def _experts(blk_expert, blk_first, blk_next, blk_slot, n_used, grp_row, xs, w_gate, w_up, w_down):
    n_blk = blk_expert.shape[0]
    r = DISPATCH_BLOCK
    gpb = r // ROW_GROUP
    grp3 = grp_row.reshape(n_blk, 1, gpb)
    smem_blk = lambda f: pl.BlockSpec((None, 1, gpb), f, memory_space=pltpu.SMEM)
    grid_spec = pltpu.PrefetchScalarGridSpec(
        num_scalar_prefetch=5,
        grid=(n_blk,),
        in_specs=[
            smem_blk(lambda i, *_: (i, 0, 0)),
            smem_blk(lambda i, *_: (jnp.minimum(i + 1, n_blk - 1), 0, 0)),
            pl.BlockSpec(memory_space=pl.ANY),
            pl.BlockSpec(memory_space=pl.ANY),
            pl.BlockSpec(memory_space=pl.ANY),
            pl.BlockSpec(memory_space=pl.ANY),
        ],
        out_specs=pl.BlockSpec((r, D_MODEL), lambda i, *_: (i, 0)),
        scratch_shapes=[
            pltpu.VMEM((2, r, xs.shape[1]), F32),
            pltpu.VMEM((2,) + w_gate.shape[1:], w_gate.dtype),
            pltpu.VMEM((2,) + w_up.shape[1:], w_up.dtype),
            pltpu.VMEM((2,) + w_down.shape[1:], w_down.dtype),
            pltpu.VMEM(w_gate.shape[1:], BF16),
            pltpu.VMEM(w_up.shape[1:], BF16),
            pltpu.VMEM(w_down.shape[1:], BF16),
            pltpu.SemaphoreType.DMA((2,)),
            pltpu.SemaphoreType.DMA((2, 3)),
        ],
    )
    return pl.pallas_call(
        _experts_kernel,
        grid_spec=grid_spec,
        out_shape=jax.ShapeDtypeStruct((n_blk * r, D_MODEL), F32),
        compiler_params=pltpu.CompilerParams(
            dimension_semantics=("arbitrary",), vmem_limit_bytes=VMEM_LIMIT),
        name="experts",
    )(blk_expert, blk_first, blk_next, blk_slot, n_used, grp3, grp3, xs, w_gate, w_up, w_down)


def _combine_kernel(cur_ref, nxt_ref, x1_ref, p_ref, pos_ref, wpg_ref, wpp_ref, g_ref, b_ref,
                    yb_hbm, o_ref, ybuf, sem):
    i = pl.program_id(0)
    n_steps = pl.num_programs(0)
    t = x1_ref.shape[0]
    slot = i % 2
    g = ROW_GROUP
    n_groups = ybuf.shape[1] // g

    def gather(grp_ref, s):
        _row_copies_start(
            range(n_groups),
            lambda q: yb_hbm.at[pl.ds(pl.multiple_of(grp_ref[0, q], g), g)],
            lambda q: ybuf.at[s, pl.ds(q * g, g)], sem.at[s])

    @pl.when(i == 0)
    def _():
        gather(cur_ref, 0)

    @pl.when(i + 1 < n_steps)
    def _():
        gather(nxt_ref, 1 - slot)

    x1 = x1_ref[...]
    ple = jax.nn.sigmoid(_dot(x1.astype(BF16), wpg_ref[...])) * _dot(p_ref[...].astype(BF16), wpp_ref[...])

    _row_copies_wait(yb_hbm.at[pl.ds(0, ybuf.shape[1])], ybuf.at[slot], sem.at[slot])
    pos = pos_ref[...]
    scol = lax.broadcasted_iota(jnp.int32, (t, ybuf.shape[1]), 1).astype(F32)
    pick = ((scol == pos[:, 0:1]) | (scol == pos[:, 1:2])).astype(BF16)
    ffn = None
    for c in range(0, ybuf.shape[1], UNSORT_CHUNK):
        ys = ybuf[slot, c:c + UNSORT_CHUNK, :]
        hi = ys.astype(BF16)
        lo = (ys - hi.astype(F32)).astype(BF16)
        pc = pick[:, c:c + UNSORT_CHUNK]
        part = _dot(pc, hi) + _dot(pc, lo)
        ffn = part if ffn is None else ffn + part
    o_ref[...] = _layer_norm(DEEPNORM_ALPHA * x1 + ffn + ple, g_ref[...], b_ref[...])


def _combine(tile_grp, x1, p2d, pos, w_pg_bf, w_pp_bf, g, b, yb):
    n = x1.shape[0]
    t = TOKEN_TILE
    n_steps = n // t
    gpt = SORTED_TILE_ROWS // ROW_GROUP
    grp3 = tile_grp.reshape(n_steps, 1, gpt)
    smem_blk = lambda f: pl.BlockSpec((None, 1, gpt), f, memory_space=pltpu.SMEM)
    row = lambda w: pl.BlockSpec((t, w), lambda i: (i, 0))
    full = lambda a: pl.BlockSpec(a.shape, lambda i: (0, 0))
    return pl.pallas_call(
        _combine_kernel,
        grid=(n_steps,),
        in_specs=[
            smem_blk(lambda i: (i, 0, 0)),
            smem_blk(lambda i: (jnp.minimum(i + 1, n_steps - 1), 0, 0)),
            row(D_MODEL), row(PLE_DIM), row(LANES), full(w_pg_bf), full(w_pp_bf), full(g), full(b),
            pl.BlockSpec(memory_space=pl.ANY),
        ],
        out_specs=row(D_MODEL),
        out_shape=jax.ShapeDtypeStruct((n, D_MODEL), F32),
        scratch_shapes=[pltpu.VMEM((2, SORTED_TILE_ROWS, D_MODEL), F32), pltpu.SemaphoreType.DMA((2,))],
        compiler_params=pltpu.CompilerParams(
            dimension_semantics=("arbitrary",), vmem_limit_bytes=VMEM_LIMIT),
        name="combine",
    )(grp3, grp3, x1, p2d, pos, w_pg_bf, w_pp_bf, g, b, yb)


def _layer(x, p, w_in, w_conv, w_out, ln1_g, ln1_b, w_router_g, b_router_g, w_router_e,
           b_router_e, w_gate, w_up, w_down, w_ple_gate, w_ple_proj, ln2_g, ln2_b):
    b, s, d = x.shape
    n = b * s
    r = DISPATCH_BLOCK

    yc, q, k, vt, kmean = _proj_conv(x, w_in.astype(BF16), w_conv)
    ya = _moba(q, k, vt, kmean.reshape(b, s // MOBA_BLOCK, D_ATTN))

    w_r = jnp.zeros((d, LANES), F32).at[:, :N_EXPERTS].set(w_router_e)
    w_r = w_r.at[:, GROUP_LANE0:GROUP_LANE0 + N_GROUPS].set(w_router_g)
    b_r = jnp.zeros((1, LANES), F32).at[0, :N_EXPERTS].set(b_router_e)
    b_r = b_r.at[0, GROUP_LANE0:GROUP_LANE0 + N_GROUPS].set(b_router_g)
    x1, xs, pos, cnt = _mix_route(
        yc.reshape(n, D_CONV), ya.reshape(n, D_ATTN), x.reshape(n, d), w_out.astype(BF16),
        ln1_g.reshape(1, d), ln1_b.reshape(1, d), w_r, b_r)

    nt = n // TOKEN_TILE
    g = ROW_GROUP
    seg = (cnt.reshape(nt, 8, LANES)[:, 0, :N_EXPERTS].astype(jnp.int32) + g - 1) // g * g
    rows_e = jnp.sum(seg, axis=0)
    padded = (rows_e + r - 1) // r * r
    row_end = jnp.cumsum(padded)
    row_start = row_end - padded
    seg_dst = row_start[None, :] + jnp.cumsum(seg, axis=0) - seg
    seg_off = jnp.cumsum(seg, axis=1) - seg
    seg_src = (jnp.arange(nt, dtype=jnp.int32) * SORTED_TILE_ROWS)[:, None] + seg_off
    n_blk = -(-(2 * n + (g - 1) * N_EXPERTS * nt) // r) + N_EXPERTS

    blk_row0 = jnp.arange(n_blk, dtype=jnp.int32) * r
    blk_expert = jnp.minimum(
        jnp.sum((row_end[None, :] <= blk_row0[:, None]).astype(jnp.int32), axis=1), N_EXPERTS - 1)
    n_used = row_end[-1:] // r
    eid = jnp.arange(N_EXPERTS, dtype=jnp.int32)
    blk_onehot = (blk_expert[:, None] == eid[None, :]).astype(jnp.int32)

    def of_block(tab):
        return jnp.sum(blk_onehot[:, :, None] * tab.T[None, :, :], axis=1)
    dst_b, src_b, len_b = of_block(seg_dst)[:, None, :], of_block(seg_src)[:, None, :], of_block(seg)[:, None, :]
    grp_first = (blk_row0[:, None] + jnp.arange(r // g, dtype=jnp.int32)[None, :] * g)[:, :, None]
    inside = (grp_first >= dst_b) & (grp_first < dst_b + len_b)
    grp_row = jnp.where(jnp.any(inside, axis=2),
                        jnp.sum(jnp.where(inside, src_b + grp_first - dst_b, 0), axis=2),
                        SORTED_TILE_ROWS - g)

    tile_first = (jnp.arange(SORTED_TILE_ROWS // g, dtype=jnp.int32) * g)[None, :, None]
    t_off, t_len, t_dst = seg_off[:, None, :], seg[:, None, :], seg_dst[:, None, :]
    within = (tile_first >= t_off) & (tile_first < t_off + t_len)
    tile_grp = jnp.sum(jnp.where(within, t_dst + tile_first - t_off, 0), axis=2)

    nonempty = rows_e > 0
    later = jnp.where((eid[None, :] > eid[:, None]) & nonempty[None, :], eid[None, :], N_EXPERTS)
    next_expert = jnp.min(later, axis=1)
    next_expert = jnp.where(next_expert == N_EXPERTS, -1, next_expert)
    ordinal = jnp.cumsum(nonempty.astype(jnp.int32)) - 1
    blk_first = jnp.concatenate(
        [jnp.ones((1,), jnp.int32), (blk_expert[1:] != blk_expert[:-1]).astype(jnp.int32)])
    blk_next = jnp.sum(blk_onehot * next_expert[None, :], axis=1)
    blk_slot = jnp.sum(blk_onehot * ordinal[None, :], axis=1) % 2

    yb = _experts(blk_expert, blk_first, blk_next, blk_slot, n_used, grp_row, xs, w_gate, w_up, w_down)
    out = _combine(tile_grp, x1, p.reshape(n, PLE_DIM), pos, w_ple_gate.astype(BF16),
                   w_ple_proj.astype(BF16), ln2_g.reshape(1, d), ln2_b.reshape(1, d), yb)
    return out.reshape(b, s, d)


def kernel(x, p, w_in, w_conv, w_out, ln1_g, ln1_b, w_router_g, b_router_g, w_router_e,
           b_router_e, w_gate, w_up, w_down, w_ple_gate, w_ple_proj, ln2_g, ln2_b):
    for i in range(DEPTH):
        x = _layer(x, p[i], w_in[i], w_conv[i], w_out[i], ln1_g[i], ln1_b[i], w_router_g[i],
                   b_router_g[i], w_router_e[i], b_router_e[i], w_gate[i], w_up[i], w_down[i],
                   w_ple_gate[i], w_ple_proj[i], ln2_g[i], ln2_b[i])
    return x
```

```python
import functools

import jax
import jax.numpy as jnp
from jax import lax
from jax.experimental import pallas as pl
from jax.experimental.pallas import tpu as pltpu

D_MODEL = 1024
D_CONV = 512
N_HEADS = 8
HEAD_DIM = 64
D_ATTN = N_HEADS * HEAD_DIM
D_IN = 3 * D_CONV + 3 * D_ATTN
MOBA_BLOCK = 256
MOBA_TOPK = 3
N_GROUPS = 4
EXPERTS_PER_GROUP = 8
N_EXPERTS = N_GROUPS * EXPERTS_PER_GROUP
DISPATCH_BLOCK = 256
ROW_GROUP = 8
PLE_DIM = 256
LN_EPS = 1e-5
DEPTH = 1
DEEPNORM_ALPHA = (2 * DEPTH) ** 0.25

LANES = 128
PROJ_ROWS = 512
TOKEN_TILE = 256
SORTED_TILE_ROWS = 2 * TOKEN_TILE + ROW_GROUP * N_EXPERTS
UNSORT_CHUNK = 256
EXPERT_ROW_SLOTS = 4
COMBINE_SLOTS = 3
SORTED_ROW_WIDTH = D_MODEL + LANES
KV_CHUNK_SHIFT = 2
KV_CHUNK_BLOCKS = 1 << KV_CHUNK_SHIFT
LOOP_CHUNKS = 4
assert LOOP_CHUNKS % 2 == 0
MOBA_HEADS_PER_STEP = 4
assert MOBA_HEADS_PER_STEP % KV_CHUNK_BLOCKS == 0
BF16_SUBLANES = 16
ACC_ROWS = HEAD_DIM + BF16_SUBLANES
Q_SCALE = HEAD_DIM ** -0.5 * 1.4426950408889634
VMEM_LIMIT = 48 * 1024 * 1024

F32 = jnp.float32
BF16 = jnp.bfloat16
NEG_INF = float("-inf")


def _dot(a, b):
    return jnp.dot(a, b, preferred_element_type=F32)


def _layer_norm(h, g, b):
    mu = jnp.mean(h, axis=-1, keepdims=True)
    d = h - mu
    var = jnp.mean(d * d, axis=-1, keepdims=True)
    return d * lax.rsqrt(var + LN_EPS) * g + b


def _proj_conv_kernel(x_ref, w_ref, wc_ref, yc_ref, q_ref, k_ref, vt_ref, km_ref, ubuf):
    s = pl.program_id(1)
    rows = x_ref.shape[0]
    xb = x_ref[...].astype(BF16)

    def sect(i):
        return _dot(xb, w_ref[:, i * D_CONV:(i + 1) * D_CONV])

    @pl.when(s == 0)
    def _():
        ubuf[0:8, :] = jnp.zeros((8, D_CONV), F32)

    u = sect(1) * sect(2)
    ubuf[8:8 + rows, :] = u
    wc = wc_ref[...]
    conv = wc[0:1, :] * ubuf[6:6 + rows, :] + wc[1:2, :] * ubuf[7:7 + rows, :] + wc[2:3, :] * u
    yc_ref[...] = (sect(0) * conv).astype(BF16)
    ubuf[0:8, :] = ubuf[rows:rows + 8, :]

    q_ref[...] = (sect(3) * Q_SCALE).astype(BF16)
    k = sect(4)
    k_ref[...] = k.astype(BF16)
    for i in range(rows // MOBA_BLOCK):
        km_ref[i] = jnp.mean(k[i * MOBA_BLOCK:(i + 1) * MOBA_BLOCK, :], axis=0, keepdims=True)
    vt = sect(5).T
    ones = jnp.ones((ACC_ROWS - HEAD_DIM, rows), F32)
    vt = jnp.concatenate(
        [piece for h in range(N_HEADS) for piece in (vt[h * HEAD_DIM:(h + 1) * HEAD_DIM, :], ones)],
        axis=0).astype(BF16)
    for i in range(rows // MOBA_BLOCK):
        vt_ref[i] = vt[:, i * MOBA_BLOCK:(i + 1) * MOBA_BLOCK]


def _proj_conv(x, w_in_bf, w_conv):
    b, s, _ = x.shape
    nb = s // MOBA_BLOCK
    grid = (b, s // PROJ_ROWS)
    seq_spec = pl.BlockSpec((None, PROJ_ROWS, D_CONV), lambda i, j: (i, j, 0))
    return pl.pallas_call(
        _proj_conv_kernel,
        grid=grid,
        in_specs=[
            pl.BlockSpec((None, PROJ_ROWS, D_MODEL), lambda i, j: (i, j, 0)),
            pl.BlockSpec((D_MODEL, D_IN), lambda i, j: (0, 0)),
            pl.BlockSpec((3, D_CONV), lambda i, j: (0, 0)),
        ],
        out_specs=[
            seq_spec, seq_spec, seq_spec,
            pl.BlockSpec((None, PROJ_ROWS // MOBA_BLOCK, N_HEADS * ACC_ROWS, MOBA_BLOCK),
                         lambda i, j: (i, j, 0, 0)),
            pl.BlockSpec((None, PROJ_ROWS // MOBA_BLOCK, 1, D_ATTN), lambda i, j: (i, j, 0, 0)),
        ],
        out_shape=[
            jax.ShapeDtypeStruct((b, s, D_CONV), BF16),
            jax.ShapeDtypeStruct((b, s, D_ATTN), BF16),
            jax.ShapeDtypeStruct((b, s, D_ATTN), BF16),
            jax.ShapeDtypeStruct((b, nb, N_HEADS * ACC_ROWS, MOBA_BLOCK), BF16),
            jax.ShapeDtypeStruct((b, nb, 1, D_ATTN), F32),
        ],
        scratch_shapes=[pltpu.VMEM((PROJ_ROWS + 8, D_CONV), F32)],
        compiler_params=pltpu.CompilerParams(
            dimension_semantics=("arbitrary", "arbitrary"), vmem_limit_bytes=VMEM_LIMIT),
        name="proj_conv",
    )(x, w_in_bf, w_conv)


def _moba_kernel(q_ref, k_ref, vt_ref, km_ref, o_ref, bias_ref, sd_ref, s_ref, mt_ref,
                 m_ref, acc_ref, qt_ref):
    j = pl.program_id(2)
    blk = MOBA_BLOCK
    nb = km_ref.shape[0]
    n_heads = vt_ref.shape[1] // ACC_ROWS
    heads = range(n_heads)
    heads_per_block = n_heads // KV_CHUNK_BLOCKS
    lead = j & (KV_CHUNK_BLOCKS - 1)
    n_chunks = lax.shift_right_logical(j, KV_CHUNK_SHIFT)
    lead0 = n_chunks * KV_CHUNK_BLOCKS

    def pair_lanes(h):
        return slice((h // 2) * LANES, (h // 2 + 1) * LANES)

    def prepare():
        row = lax.broadcasted_iota(jnp.int32, (LANES, blk), 0)
        qt_pairs = [q_ref[:, pair_lanes(2 * pr)].astype(F32).T for pr in range(n_heads // 2)]
        bidx = lax.broadcasted_iota(jnp.int32, (nb, blk), 0)
        for h in heads:
            qt = jnp.where((row >= HEAD_DIM) == bool(h % 2), qt_pairs[h // 2], 0.0).astype(BF16)
            qt_ref[h] = qt
            km = km_ref[:, pair_lanes(h)]
            km_hi = km.astype(BF16)
            km_lo = (km - km_hi.astype(F32)).astype(BF16)
            g = _dot(km_hi, qt) + _dot(km_lo, qt)
            g = jnp.where(bidx < j, g, NEG_INF)
            sel = jnp.zeros((nb, blk), jnp.bool_)
            for _ in range(MOBA_TOPK):
                mx = jnp.max(g, axis=0, keepdims=True)
                cand = jnp.where((g == mx) & (mx > NEG_INF), bidx, nb)
                pick = bidx == jnp.min(cand, axis=0, keepdims=True)
                sel = sel | pick
                g = jnp.where(pick, NEG_INF, g)
            bias_ref[h] = jnp.where(sel, 0.0, NEG_INF)

    def k_block(n, h):
        return k_ref[pl.ds(pl.multiple_of(n * blk, blk), blk), pair_lanes(h)]

    def vt_block(n, h):
        return vt_ref[n, h * ACC_ROWS:(h + 1) * ACC_ROWS, :]

    def stage_head(c, u, slot, h):
        n = c * KV_CHUNK_BLOCKS + u
        sc = _dot(k_block(n, h), qt_ref[h]) + bias_ref[h, pl.ds(n, 1), :]
        s_ref[slot, h, u * blk:(u + 1) * blk, :] = sc
        return jnp.max(sc, axis=0, keepdims=True)

    def stage_block(c, u, slot):
        return [stage_head(c, u, slot, h) for h in heads]

    def store_staged_max(slot, mts):
        for h in heads:
            mt_ref[slot, h] = functools.reduce(jnp.maximum, [m[h] for m in mts])

    def pv(vt_h, sc, m_new):
        return _dot(vt_h, jnp.exp2(sc - m_new).astype(BF16))

    def rescale(h, mt):
        m_old = m_ref[h]
        m_new = jnp.maximum(m_old, mt)
        m_ref[h] = m_new
        return m_new, jnp.exp2(m_old - m_new)

    def step(consume, stage):
        if consume is not None:
            cc, cs = consume
            scale = [rescale(h, mt_ref[cs, h]) for h in heads]
        mts = []
        for u in range(KV_CHUNK_BLOCKS):
            staged = heads if stage is not None else []
            updated = heads[u * heads_per_block:(u + 1) * heads_per_block] if consume is not None else []
            products = [(h, v) for h in updated for v in range(KV_CHUNK_BLOCKS)]
            mt_u = []
            parts = {h: [] for h in updated}
            for i in range(max(len(staged), len(products))):
                if i < len(staged):
                    mt_u.append(stage_head(stage[0], u, stage[1], staged[i]))
                if i < len(products):
                    h, v = products[i]
                    parts[h].append(pv(vt_block(cc * KV_CHUNK_BLOCKS + v, h),
                                       s_ref[cs, h, v * blk:(v + 1) * blk, :], scale[h][0]))
            mts.append(mt_u)
            for h in updated:
                acc_ref[h] = scale[h][1] * acc_ref[h] + functools.reduce(jnp.add, parts[h])
        if stage is not None:
            store_staged_max(stage[1], mts)

    kpos = lax.broadcasted_iota(jnp.int32, (blk, blk), 0)
    qpos = lax.broadcasted_iota(jnp.int32, (blk, blk), 1)
    causal = kpos <= qpos

    def first_update(n_lead):
        prepare()
        m_first = []
        for h in heads:
            sc = jnp.where(causal, _dot(k_block(j, h), qt_ref[h]), NEG_INF)
            sd_ref[h] = sc
            m_first.append(jnp.max(sc, axis=0, keepdims=True))
        for u in range(n_lead):
            for h in heads:
                sc = _dot(k_block(lead0 + u, h), qt_ref[h]) + bias_ref[h, pl.ds(lead0 + u, 1), :]
                s_ref[1, h, u * blk:(u + 1) * blk, :] = sc
                m_first[h] = jnp.maximum(m_first[h], jnp.max(sc, axis=0, keepdims=True))
        mts = []
        for u in range(KV_CHUNK_BLOCKS):
            mts.append(stage_block(0, u, 0))
            for h in heads[u * heads_per_block:(u + 1) * heads_per_block]:
                parts = [pv(vt_block(j, h), sd_ref[h], m_first[h])]
                parts += [pv(vt_block(lead0 + v, h), s_ref[1, h, v * blk:(v + 1) * blk, :], m_first[h])
                          for v in range(n_lead)]
                m_ref[h] = m_first[h]
                acc_ref[h] = functools.reduce(jnp.add, parts)
        store_staged_max(0, mts)

    for n_lead in range(KV_CHUNK_BLOCKS):
        pl.when(lead == n_lead)(functools.partial(first_update, n_lead))

    steps = jnp.maximum(n_chunks - 1, 0)

    def body(i, carry):
        c = LOOP_CHUNKS * i
        for d in range(LOOP_CHUNKS):
            step((c + d, d % 2), (c + d + 1, (d + 1) % 2))
        return carry

    lax.fori_loop(0, steps // LOOP_CHUNKS, body, 0)

    c0 = steps // LOOP_CHUNKS * LOOP_CHUNKS
    for rem in range(LOOP_CHUNKS):
        @pl.when((n_chunks > 0) & (steps % LOOP_CHUNKS == rem))
        def _(rem=rem):
            for d in range(rem):
                step((c0 + d, d % 2), (c0 + d + 1, (d + 1) % 2))
            step((c0 + rem, rem % 2), None)

    outs = [acc_ref[h, 0:HEAD_DIM, :] / acc_ref[h, HEAD_DIM:HEAD_DIM + 1, :] for h in heads]
    o_ref[...] = jnp.concatenate(outs, axis=0).T.astype(BF16)


def _moba(q, k, vt, kmean):
    b, s, _ = q.shape
    nb = s // MOBA_BLOCK
    w = MOBA_HEADS_PER_STEP * HEAD_DIM
    hs = MOBA_HEADS_PER_STEP
    return pl.pallas_call(
        _moba_kernel,
        grid=(b, D_ATTN // w, nb),
        in_specs=[
            pl.BlockSpec((None, MOBA_BLOCK, w), lambda i, p, j: (i, j, p)),
            pl.BlockSpec((None, s, w), lambda i, p, j: (i, 0, p)),
            pl.BlockSpec((None, nb, hs * ACC_ROWS, MOBA_BLOCK), lambda i, p, j: (i, 0, p, 0)),
            pl.BlockSpec((None, nb, w), lambda i, p, j: (i, 0, p)),
        ],
        out_specs=pl.BlockSpec((None, MOBA_BLOCK, w), lambda i, p, j: (i, j, p)),
        out_shape=jax.ShapeDtypeStruct((b, s, D_ATTN), BF16),
        scratch_shapes=[
            pltpu.VMEM((hs, nb, MOBA_BLOCK), F32),
            pltpu.VMEM((hs, MOBA_BLOCK, MOBA_BLOCK), F32),
            pltpu.VMEM((2, hs, KV_CHUNK_BLOCKS * MOBA_BLOCK, MOBA_BLOCK), F32),
            pltpu.VMEM((3, hs, 1, MOBA_BLOCK), F32),
            pltpu.VMEM((hs, 1, MOBA_BLOCK), F32),
            pltpu.VMEM((hs, ACC_ROWS, MOBA_BLOCK), F32),
            pltpu.VMEM((hs, LANES, MOBA_BLOCK), BF16),
        ],
        compiler_params=pltpu.CompilerParams(
            dimension_semantics=("arbitrary", "arbitrary", "arbitrary"),
            vmem_limit_bytes=VMEM_LIMIT),
        name="moba",
    )(q, k, vt, kmean)


GROUP_LANE0 = N_EXPERTS


def _split3(col, lane):
    hi = col.astype(BF16).astype(F32)
    mid = (col - hi).astype(BF16).astype(F32)
    lo = col - hi - mid
    return jnp.where(lane == 0, hi, jnp.where(lane == 1, mid, jnp.where(lane == 2, lo, 0.0))).astype(BF16)


def _mix_route_kernel(yc_ref, ya_ref, x_ref, wo_ref, g_ref, b_ref, wr_ref, br_ref,
                      x1_ref, xs_ref, pos_ref, cnt_ref):
    t = x_ref.shape[0]

    mix = _dot(yc_ref[...], wo_ref[0:D_CONV, :]) + _dot(ya_ref[...], wo_ref[D_CONV:, :])
    x1 = _layer_norm(DEEPNORM_ALPHA * x_ref[...] + mix, g_ref[...], b_ref[...])
    x1_ref[...] = x1

    x_hi = x1.astype(BF16)
    x_lo = (x1 - x_hi.astype(F32)).astype(BF16)
    wr = wr_ref[...]
    w_hi = wr.astype(BF16)
    w_lo = (wr - w_hi.astype(F32)).astype(BF16)
    logits = _dot(x_hi, w_hi) + _dot(x_lo, w_hi) + _dot(x_hi, w_lo) + br_ref[...]

    lane = lax.broadcasted_iota(jnp.int32, (t, LANES), 1).astype(F32)
    big = float(LANES)

    lg = jnp.where((lane >= GROUP_LANE0) & (lane < GROUP_LANE0 + N_GROUPS), logits, NEG_INF)
    gmax = jnp.max(lg, axis=-1, keepdims=True)
    g_p = 1.0 / jnp.sum(jnp.exp(lg - gmax), axis=-1, keepdims=True)
    g_idx = jnp.min(jnp.where(lg == gmax, lane, big), axis=-1, keepdims=True) - GROUP_LANE0

    lo = g_idx * EXPERTS_PER_GROUP
    le = jnp.where((lane >= lo) & (lane < lo + EXPERTS_PER_GROUP), logits, NEG_INF)
    m1 = jnp.max(le, axis=-1, keepdims=True)
    i1 = jnp.min(jnp.where(le == m1, lane, big), axis=-1, keepdims=True)
    le2 = jnp.where(lane == i1, NEG_INF, le)
    m2 = jnp.max(le2, axis=-1, keepdims=True)
    i2 = jnp.min(jnp.where(le2 == m2, lane, big), axis=-1, keepdims=True)
    p2 = jnp.exp(m2 - m1)
    den = 1.0 + p2
    w1 = g_p * (1.0 / den)
    w2 = g_p * (p2 / den)

    oh = (lane == i1) | (lane == i2)
    ohf = oh.astype(F32)
    r_i = lax.broadcasted_iota(jnp.int32, (t, t), 0)
    c_i = lax.broadcasted_iota(jnp.int32, (t, t), 1)
    before = _dot((r_i > c_i).astype(BF16), ohf.astype(BF16))
    r1 = jnp.sum(jnp.where(lane == i1, before, 0.0), axis=-1, keepdims=True)
    r2 = jnp.sum(jnp.where(lane == i2, before, 0.0), axis=-1, keepdims=True)

    cnt = jnp.sum(ohf, axis=0, keepdims=True)
    seg = jnp.floor((cnt + (ROW_GROUP - 1)) * (1.0 / ROW_GROUP)) * ROW_GROUP
    e_a = lax.broadcasted_iota(jnp.int32, (LANES, LANES), 0)
    e_b = lax.broadcasted_iota(jnp.int32, (LANES, LANES), 1)
    seg_start = _dot(jnp.broadcast_to(seg, (8, LANES)).astype(BF16), (e_a < e_b).astype(BF16))[0:1, :]
    pos1 = jnp.sum(jnp.where(lane == i1, seg_start, 0.0), axis=-1, keepdims=True) + r1
    pos2 = jnp.sum(jnp.where(lane == i2, seg_start, 0.0), axis=-1, keepdims=True) + r2
    cnt_ref[...] = jnp.broadcast_to(cnt, cnt_ref.shape)

    pos = jnp.where(lane == 0, pos1, pos2)
    pos_ref[...] = pos
    pos_rows = pos.T[0:8, :]

    srow = lax.broadcasted_iota(jnp.int32, (xs_ref.shape[0], t), 0).astype(F32)
    perm1 = (srow == pos_rows[0:1, :]).astype(BF16)
    perm2 = (srow == pos_rows[1:2, :]).astype(BF16)
    xs_ref[:, 0:D_MODEL] = _dot(perm1 + perm2, x_hi)
    xs_ref[:, D_MODEL:] = (_dot(perm1, _split3(w1, lane)) + _dot(perm2, _split3(w2, lane)))


def _mix_route(yc, ya, x2d, w_out_bf, g, b, w_r, b_r):
    n = x2d.shape[0]
    t = TOKEN_TILE
    nt = n // t
    row = lambda w: pl.BlockSpec((t, w), lambda i: (i, 0))
    full = lambda a: pl.BlockSpec(a.shape, lambda i: (0, 0))
    return pl.pallas_call(
        _mix_route_kernel,
        grid=(nt,),
        in_specs=[row(D_CONV), row(D_ATTN), row(D_MODEL), full(w_out_bf), full(g), full(b),
                  full(w_r), full(b_r)],
        out_specs=[row(D_MODEL), pl.BlockSpec((SORTED_TILE_ROWS, SORTED_ROW_WIDTH), lambda i: (i, 0)),
                   row(LANES), pl.BlockSpec((8, LANES), lambda i: (i, 0))],
        out_shape=[
            jax.ShapeDtypeStruct((n, D_MODEL), F32),
            jax.ShapeDtypeStruct((nt * SORTED_TILE_ROWS, SORTED_ROW_WIDTH), F32),
            jax.ShapeDtypeStruct((n, LANES), F32),
            jax.ShapeDtypeStruct((nt * 8, LANES), F32),
        ],
        compiler_params=pltpu.CompilerParams(
            dimension_semantics=("arbitrary",), vmem_limit_bytes=VMEM_LIMIT),
        name="mix_route",
    )(yc, ya, x2d, w_out_bf, g, b, w_r, b_r)


def _row_copies_start(rows, src_of, dst_of, sem):
    for r in rows:
        pltpu.make_async_copy(src_of(r), dst_of(r), sem).start()


def _row_copies_wait(src_rows, dst_rows, sem):
    pltpu.make_async_copy(src_rows, dst_rows, sem).wait()


def _experts_kernel(be_ref, first_ref, nxt_ref, par_ref, nu_ref, *refs):
    ahead = refs[:EXPERT_ROW_SLOTS]
    (xs_hbm, wg_hbm, wu_hbm, wd_hbm, yb_ref, xbuf, wg_buf, wu_buf, wd_buf, wg_bf, wu_bf, wd_bf,
     xsem, wsem) = refs[EXPERT_ROW_SLOTS:]
    i = pl.program_id(0)
    used = i < nu_ref[0]
    slot = par_ref[i]
    xslot = i % EXPERT_ROW_SLOTS
    g = ROW_GROUP
    weights = ((wg_hbm, wg_buf), (wu_hbm, wu_buf), (wd_hbm, wd_buf))

    def fetch(e, s):
        for k, (w_hbm, w_buf) in enumerate(weights):
            pltpu.make_async_copy(w_hbm.at[e], w_buf.at[s], wsem.at[s, k]).start(priority=1)

    def fetch_wait(s):
        for k, (w_hbm, w_buf) in enumerate(weights):
            pltpu.make_async_copy(w_hbm.at[0], w_buf.at[s], wsem.at[s, k]).wait()

    def gather_rows(grp_ref, s):
        _row_copies_start(
            range(DISPATCH_BLOCK // g),
            lambda q: xs_hbm.at[pl.ds(pl.multiple_of(grp_ref[0, q], g), g)],
            lambda q: xbuf.at[s, pl.ds(q * g, g)], xsem.at[s])

    @pl.when(i == 0)
    def _():
        fetch(be_ref[0], slot)
        gather_rows(ahead[0], 0)
        for k in range(1, EXPERT_ROW_SLOTS - 1):
            @pl.when(k < nu_ref[0])
            def _(k=k):
                gather_rows(ahead[k], k)

    @pl.when(i + (EXPERT_ROW_SLOTS - 1) < nu_ref[0])
    def _():
        gather_rows(ahead[EXPERT_ROW_SLOTS - 1], (i + (EXPERT_ROW_SLOTS - 1)) % EXPERT_ROW_SLOTS)

    @pl.when(used & (first_ref[i] == 1))
    def _():
        fetch_wait(slot)

        @pl.when(nxt_ref[i] >= 0)
        def _():
            fetch(nxt_ref[i], 1 - slot)

        wg_bf[...] = wg_buf[slot].astype(BF16)
        wu_bf[...] = wu_buf[slot].astype(BF16)
        wd_bf[...] = wd_buf[slot].astype(BF16)

    @pl.when(used)
    def _():
        _row_copies_wait(xs_hbm.at[pl.ds(0, DISPATCH_BLOCK)], xbuf.at[xslot], xsem.at[xslot])
        xb = xbuf[xslot, :, 0:D_MODEL].astype(BF16)
        wrow = xbuf[xslot, :, D_MODEL:]
        weight = wrow[:, 0:1] + wrow[:, 1:2] + wrow[:, 2:3]
        gate = _dot(xb, wg_bf[...])
        up = _dot(xb, wu_bf[...])
        hid = (jax.nn.silu(gate) * up).astype(BF16)
        yb_ref[...] = _dot(hid, wd_bf[...]) * weight

    @pl.when(jnp.logical_not(used))
    def _():
        yb_ref[...] = jnp.zeros_like(yb_ref)


def _experts(blk_expert, blk_first, blk_next, blk_slot, n_used, grp_row, xs, w_gate, w_up, w_down):
    n_blk = blk_expert.shape[0]
    r = DISPATCH_BLOCK
    gpb = r // ROW_GROUP
    grp3 = grp_row.reshape(n_blk, 1, gpb)
    smem_blk = lambda f: pl.BlockSpec((None, 1, gpb), f, memory_space=pltpu.SMEM)
    grid_spec = pltpu.PrefetchScalarGridSpec(
        num_scalar_prefetch=5,
        grid=(n_blk,),
        in_specs=[smem_blk(lambda i, *_, k=k: (jnp.minimum(i + k, n_blk - 1), 0, 0))
                  for k in range(EXPERT_ROW_SLOTS)] + [
            pl.BlockSpec(memory_space=pl.ANY),
            pl.BlockSpec(memory_space=pl.ANY),
            pl.BlockSpec(memory_space=pl.ANY),
            pl.BlockSpec(memory_space=pl.ANY),
        ],
        out_specs=pl.BlockSpec((r, D_MODEL), lambda i, *_: (i, 0)),
        scratch_shapes=[
            pltpu.VMEM((EXPERT_ROW_SLOTS, r, xs.shape[1]), F32),
            pltpu.VMEM((2,) + w_gate.shape[1:], w_gate.dtype),
            pltpu.VMEM((2,) + w_up.shape[1:], w_up.dtype),
            pltpu.VMEM((2,) + w_down.shape[1:], w_down.dtype),
            pltpu.VMEM(w_gate.shape[1:], BF16),
            pltpu.VMEM(w_up.shape[1:], BF16),
            pltpu.VMEM(w_down.shape[1:], BF16),
            pltpu.SemaphoreType.DMA((EXPERT_ROW_SLOTS,)),
            pltpu.SemaphoreType.DMA((2, 3)),
        ],
    )
    return pl.pallas_call(
        _experts_kernel,
        grid_spec=grid_spec,
        out_shape=jax.ShapeDtypeStruct((n_blk * r, D_MODEL), F32),
        compiler_params=pltpu.CompilerParams(
            dimension_semantics=("arbitrary",), vmem_limit_bytes=VMEM_LIMIT),
        name="experts",
    )(blk_expert, blk_first, blk_next, blk_slot, n_used, *([grp3] * EXPERT_ROW_SLOTS),
      xs, w_gate, w_up, w_down)


def _combine_kernel(*refs):
    ahead = refs[:COMBINE_SLOTS]
    x1_ref, p_ref, pos_ref, wpg_ref, wpp_ref, g_ref, b_ref, yb_hbm, o_ref, ybuf, sem = refs[COMBINE_SLOTS:]
    i = pl.program_id(0)
    n_steps = pl.num_programs(0)
    t = x1_ref.shape[0]
    slot = i % COMBINE_SLOTS
    g = ROW_GROUP
    n_groups = ybuf.shape[1] // g

    def gather(grp_ref, s):
        _row_copies_start(
            range(n_groups),
            lambda q: yb_hbm.at[pl.ds(pl.multiple_of(grp_ref[0, q], g), g)],
            lambda q: ybuf.at[s, pl.ds(q * g, g)], sem.at[s])

    @pl.when(i == 0)
    def _():
        for k in range(COMBINE_SLOTS - 1):
            gather(ahead[k], k)

    x1 = x1_ref[...]
    ple = jax.nn.sigmoid(_dot(x1.astype(BF16), wpg_ref[...])) * _dot(p_ref[...].astype(BF16), wpp_ref[...])

    gather(ahead[COMBINE_SLOTS - 1], (i + (COMBINE_SLOTS - 1)) % COMBINE_SLOTS)

    _row_copies_wait(yb_hbm.at[pl.ds(0, ybuf.shape[1])], ybuf.at[slot], sem.at[slot])
    pos = pos_ref[...]
    scol = lax.broadcasted_iota(jnp.int32, (t, ybuf.shape[1]), 1).astype(F32)
    pick = ((scol == pos[:, 0:1]) | (scol == pos[:, 1:2])).astype(BF16)
    ffn = None
    for c in range(0, ybuf.shape[1], UNSORT_CHUNK):
        ys = ybuf[slot, c:c + UNSORT_CHUNK, :]
        hi = ys.astype(BF16)
        lo = (ys - hi.astype(F32)).astype(BF16)
        pc = pick[:, c:c + UNSORT_CHUNK]
        part = _dot(pc, hi) + _dot(pc, lo)
        ffn = part if ffn is None else ffn + part
    o_ref[...] = _layer_norm(DEEPNORM_ALPHA * x1 + ffn + ple, g_ref[...], b_ref[...])

    @pl.when(i == n_steps - 1)
    def _():
        for k in range(1, COMBINE_SLOTS):
            s = (i + k) % COMBINE_SLOTS
            _row_copies_wait(yb_hbm.at[pl.ds(0, ybuf.shape[1])], ybuf.at[s], sem.at[s])


def _combine(tile_grp, x1, p2d, pos, w_pg_bf, w_pp_bf, g, b, yb):
    n = x1.shape[0]
    t = TOKEN_TILE
    n_steps = n // t
    assert n_steps >= COMBINE_SLOTS - 1
    gpt = SORTED_TILE_ROWS // ROW_GROUP
    grp3 = tile_grp.reshape(n_steps, 1, gpt)
    smem_blk = lambda f: pl.BlockSpec((None, 1, gpt), f, memory_space=pltpu.SMEM)
    row = lambda w: pl.BlockSpec((t, w), lambda i: (i, 0))
    full = lambda a: pl.BlockSpec(a.shape, lambda i: (0, 0))
    return pl.pallas_call(
        _combine_kernel,
        grid=(n_steps,),
        in_specs=[smem_blk(lambda i, k=k: (jnp.minimum(i + k, n_steps - 1), 0, 0))
                  for k in range(COMBINE_SLOTS)] + [
            row(D_MODEL), row(PLE_DIM), row(LANES), full(w_pg_bf), full(w_pp_bf), full(g), full(b),
            pl.BlockSpec(memory_space=pl.ANY),
        ],
        out_specs=row(D_MODEL),
        out_shape=jax.ShapeDtypeStruct((n, D_MODEL), F32),
        scratch_shapes=[pltpu.VMEM((COMBINE_SLOTS, SORTED_TILE_ROWS, D_MODEL), F32),
                        pltpu.SemaphoreType.DMA((COMBINE_SLOTS,))],
        compiler_params=pltpu.CompilerParams(
            dimension_semantics=("arbitrary",), vmem_limit_bytes=VMEM_LIMIT),
        name="combine",
    )(*([grp3] * COMBINE_SLOTS), x1, p2d, pos, w_pg_bf, w_pp_bf, g, b, yb)


def _layer(x, p, w_in, w_conv, w_out, ln1_g, ln1_b, w_router_g, b_router_g, w_router_e,
           b_router_e, w_gate, w_up, w_down, w_ple_gate, w_ple_proj, ln2_g, ln2_b):
    b, s, d = x.shape
    n = b * s
    r = DISPATCH_BLOCK

    yc, q, k, vt, kmean = _proj_conv(x, w_in.astype(BF16), w_conv)
    ya = _moba(q, k, vt, kmean.reshape(b, s // MOBA_BLOCK, D_ATTN))

    w_r = jnp.zeros((d, LANES), F32).at[:, :N_EXPERTS].set(w_router_e)
    w_r = w_r.at[:, GROUP_LANE0:GROUP_LANE0 + N_GROUPS].set(w_router_g)
    b_r = jnp.zeros((1, LANES), F32).at[0, :N_EXPERTS].set(b_router_e)
    b_r = b_r.at[0, GROUP_LANE0:GROUP_LANE0 + N_GROUPS].set(b_router_g)
    x1, xs, pos, cnt = _mix_route(
        yc.reshape(n, D_CONV), ya.reshape(n, D_ATTN), x.reshape(n, d), w_out.astype(BF16),
        ln1_g.reshape(1, d), ln1_b.reshape(1, d), w_r, b_r)

    nt = n // TOKEN_TILE
    g = ROW_GROUP
    seg = (cnt.reshape(nt, 8, LANES)[:, 0, :N_EXPERTS].astype(jnp.int32) + g - 1) // g * g
    rows_e = jnp.sum(seg, axis=0)
    padded = (rows_e + r - 1) // r * r
    row_end = jnp.cumsum(padded)
    row_start = row_end - padded
    seg_dst = row_start[None, :] + jnp.cumsum(seg, axis=0) - seg
    seg_off = jnp.cumsum(seg, axis=1) - seg
    seg_src = (jnp.arange(nt, dtype=jnp.int32) * SORTED_TILE_ROWS)[:, None] + seg_off
    n_blk = -(-(2 * n + (g - 1) * N_EXPERTS * nt) // r) + N_EXPERTS

    blk_row0 = jnp.arange(n_blk, dtype=jnp.int32) * r
    blk_expert = jnp.minimum(
        jnp.sum((row_end[None, :] <= blk_row0[:, None]).astype(jnp.int32), axis=1), N_EXPERTS - 1)
    n_used = row_end[-1:] // r
    eid = jnp.arange(N_EXPERTS, dtype=jnp.int32)
    blk_onehot = (blk_expert[:, None] == eid[None, :]).astype(jnp.int32)

    def of_block(tab):
        return jnp.sum(blk_onehot[:, :, None] * tab.T[None, :, :], axis=1)
    dst_b, src_b, len_b = of_block(seg_dst)[:, None, :], of_block(seg_src)[:, None, :], of_block(seg)[:, None, :]
    grp_first = (blk_row0[:, None] + jnp.arange(r // g, dtype=jnp.int32)[None, :] * g)[:, :, None]
    inside = (grp_first >= dst_b) & (grp_first < dst_b + len_b)
    grp_row = jnp.where(jnp.any(inside, axis=2),
                        jnp.sum(jnp.where(inside, src_b + grp_first - dst_b, 0), axis=2),
                        SORTED_TILE_ROWS - g)

    tile_first = (jnp.arange(SORTED_TILE_ROWS // g, dtype=jnp.int32) * g)[None, :, None]
    t_off, t_len, t_dst = seg_off[:, None, :], seg[:, None, :], seg_dst[:, None, :]
    within = (tile_first >= t_off) & (tile_first < t_off + t_len)
    tile_grp = jnp.sum(jnp.where(within, t_dst + tile_first - t_off, 0), axis=2)

    nonempty = rows_e > 0
    later = jnp.where((eid[None, :] > eid[:, None]) & nonempty[None, :], eid[None, :], N_EXPERTS)
    next_expert = jnp.min(later, axis=1)
    next_expert = jnp.where(next_expert == N_EXPERTS, -1, next_expert)
    ordinal = jnp.cumsum(nonempty.astype(jnp.int32)) - 1
    blk_first = jnp.concatenate(
        [jnp.ones((1,), jnp.int32), (blk_expert[1:] != blk_expert[:-1]).astype(jnp.int32)])
    blk_next = jnp.sum(blk_onehot * next_expert[None, :], axis=1)
    blk_slot = jnp.sum(blk_onehot * ordinal[None, :], axis=1) % 2

    yb = _experts(blk_expert, blk_first, blk_next, blk_slot, n_used, grp_row, xs, w_gate, w_up, w_down)
    out = _combine(tile_grp, x1, p.reshape(n, PLE_DIM), pos, w_ple_gate.astype(BF16),
                   w_ple_proj.astype(BF16), ln2_g.reshape(1, d), ln2_b.reshape(1, d), yb)
    return out.reshape(b, s, d)


def kernel(x, p, w_in, w_conv, w_out, ln1_g, ln1_b, w_router_g, b_router_g, w_router_e,
           b_router_e, w_gate, w_up, w_down, w_ple_gate, w_ple_proj, ln2_g, ln2_b):
    for i in range(DEPTH):
        x = _layer(x, p[i], w_in[i], w_conv[i], w_out[i], ln1_g[i], ln1_b[i], w_router_g[i],
                   b_router_g[i], w_router_e[i], b_router_e[i], w_gate[i], w_up[i], w_down[i],
                   w_ple_gate[i], w_ple_proj[i], ln2_g[i], ln2_b[i])
    return x
```
